```python
import functools
import jax, jax.numpy as jnp
from jax import lax
import numpy as np

D_MODEL = 1024
BATCH = 16
SEQ = 2048
DEPTH = 1
DEC_BATCH = 32
DEC_SEQ = 1
PAST_LEN = 16384
PAGE_SIZE = 128

N_HEADS = 8
HEAD_DIM = 64
ATTN_WIDTH = N_HEADS * HEAD_DIM
CONV_WIDTH = D_MODEL - ATTN_WIDTH
CONV_GROUPS = 8
CONV_GROUP_DIM = CONV_WIDTH // CONV_GROUPS
CONV_K = 3
MIX_WIDTH = ATTN_WIDTH + CONV_WIDTH
IN_WIDTH = 3 * ATTN_WIDTH + N_HEADS + 3 * CONV_WIDTH
N_GROUPS = 4
EXPERTS_PER_GROUP = 8
N_EXPERTS = N_GROUPS * EXPERTS_PER_GROUP
TOP_K = 2
D_EXPERT = 256
Q_BLOCK = 128
EPS = 1e-6
FORGET_BIAS_INIT = 2.0

kernel_name = "hymba_fox_shortconv_hmoe_adaln_step"


def _rms_norm(x, g):
    x32 = x.astype(jnp.float32)
    y = x32 * lax.rsqrt(jnp.mean(x32 * x32, axis=-1, keepdims=True) + EPS)
    return y.astype(x.dtype) * g


def _adaln(c, w, b, n):
    mod = jax.nn.silu(c) @ w + b
    return jnp.split(mod[:, None, :], n, axis=-1)


def _mixer_inputs(h, w_in, b_forget):
    bsz, t, _ = h.shape
    proj = h @ w_in
    a, c = ATTN_WIDTH, CONV_WIDTH
    cuts = [a, 2 * a, 3 * a, 3 * a + N_HEADS, 3 * a + N_HEADS + c, 3 * a + N_HEADS + 2 * c]
    q, k, v, f, u, gate_b, gate_c = jnp.split(proj, cuts, axis=-1)
    heads = lambda z: z.reshape(bsz, t, N_HEADS, HEAD_DIM)
    logf = jax.nn.log_sigmoid((f + b_forget).astype(jnp.float32))
    return heads(q), heads(k), heads(v), logf, gate_c * u, gate_b


def _short_conv(conv_in, buf, conv_w):
    t = conv_in.shape[1]
    full = jnp.concatenate([buf.astype(conv_in.dtype), conv_in], axis=1)
    y = sum(conv_w[j] * full[:, j:j + t] for j in range(CONV_K))
    return y, full[:, t:]


def _fox_prompt(q, k, v, logf):
    bsz, t, h, hd = q.shape
    nb = t // Q_BLOCK
    cum = jnp.cumsum(logf, axis=1)
    cum_k = cum.transpose(0, 2, 1)[:, :, None, :]
    qb = q.reshape(bsz, nb, Q_BLOCK, h, hd).transpose(1, 0, 2, 3, 4)
    cqb = cum.reshape(bsz, nb, Q_BLOCK, h).transpose(1, 0, 2, 3)
    key_pos = jnp.arange(t)
    scale = HEAD_DIM ** -0.5

    def block(args):
        i, qi, ci = args
        s = jnp.einsum('bqhd,bkhd->bhqk', qi, k).astype(jnp.float32) * scale
        s = s + ci.transpose(0, 2, 1)[..., None] - cum_k
        qpos = i * Q_BLOCK + jnp.arange(Q_BLOCK)
        s = jnp.where(key_pos[None, :] <= qpos[:, None], s, -jnp.inf)
        p = jax.nn.softmax(s, axis=-1).astype(v.dtype)
        return jnp.einsum('bhqk,bkhd->bqhd', p, v)

    out = lax.map(block, (jnp.arange(nb), qb, cqb))
    return out.transpose(1, 0, 2, 3, 4).reshape(bsz, t, h, hd)


def _fox_sample(q, k_new, v_new, logf_new, cache_k, cache_v, cache_logf, page_table):
    bsz, n_pages = page_table.shape
    past = n_pages * PAGE_SIZE
    s_new = q.shape[1]
    k_past = cache_k[page_table].reshape(bsz, past, N_HEADS, HEAD_DIM)
    v_past = cache_v[page_table].reshape(bsz, past, N_HEADS, HEAD_DIM)
    f_past = cache_logf[page_table].reshape(bsz, past, N_HEADS).astype(jnp.float32)
    k_all = jnp.concatenate([k_past.astype(k_new.dtype), k_new], axis=1)
    v_all = jnp.concatenate([v_past.astype(v_new.dtype), v_new], axis=1)
    cum = jnp.cumsum(jnp.concatenate([f_past, logf_new], axis=1), axis=1)
    cq = cum[:, past:].transpose(0, 2, 1)[..., None]
    ck = cum.transpose(0, 2, 1)[:, :, None, :]
    s = jnp.einsum('bqhd,bkhd->bhqk', q, k_all).astype(jnp.float32) * (HEAD_DIM ** -0.5)
    s = s + cq - ck
    key_pos = jnp.arange(past + s_new)
    qpos = past + jnp.arange(s_new)
    s = jnp.where(key_pos[None, :] <= qpos[:, None], s, -jnp.inf)
    p = jax.nn.softmax(s, axis=-1).astype(v_all.dtype)
    return jnp.einsum('bhqk,bkhd->bqhd', p, v_all)


def _merge_heads(attn, conv_y, gate_b, g_attn, g_conv, w_out):
    bsz, t = attn.shape[:2]
    a = _rms_norm(attn, g_attn).reshape(bsz, t, ATTN_WIDTH)
    cv = (gate_b * conv_y).reshape(bsz, t, CONV_GROUPS, CONV_GROUP_DIM)
    cv = _rms_norm(cv, g_conv).reshape(bsz, t, CONV_WIDTH)
    return jnp.concatenate([a, cv], axis=-1) @ w_out


def _hier_moe(h, w_rg, b_rg, w_re, b_re, w_eg, w_eu, w_ed):
    shape = h.shape
    x = h.reshape(-1, D_MODEL)
    n = x.shape[0]
    glog = (x @ w_rg + b_rg).astype(jnp.float32)
    gprob = jax.nn.softmax(glog, axis=-1)
    g_oh = jax.nn.one_hot(jnp.argmax(glog, axis=-1), N_GROUPS, dtype=jnp.float32)
    gp = jnp.sum(gprob * g_oh, axis=-1, keepdims=True)
    elog = (x @ w_re + b_re).astype(jnp.float32).reshape(n, N_GROUPS, EXPERTS_PER_GROUP)
    elog_sel = jnp.einsum('ng,nge->ne', g_oh, elog)
    topv, topi = lax.top_k(jax.nn.softmax(elog_sel, axis=-1), TOP_K)
    topv = topv / jnp.sum(topv, axis=-1, keepdims=True) * gp
    ew = jnp.sum(jax.nn.one_hot(topi, EXPERTS_PER_GROUP, dtype=jnp.float32) * topv[..., None], axis=1)
    weights = (g_oh[:, :, None] * ew[:, None, :]).astype(x.dtype)
    wg = w_eg.reshape(N_GROUPS, EXPERTS_PER_GROUP, D_MODEL, D_EXPERT)
    wu = w_eu.reshape(N_GROUPS, EXPERTS_PER_GROUP, D_MODEL, D_EXPERT)
    wd = w_ed.reshape(N_GROUPS, EXPERTS_PER_GROUP, D_EXPERT, D_MODEL)
    y = jnp.zeros_like(x)
    for g in range(N_GROUPS):
        hg = jax.nn.silu(jnp.einsum('nd,edf->nef', x, wg[g])) * jnp.einsum('nd,edf->nef', x, wu[g])
        y = y + jnp.einsum('nef,efd->nd', hg * weights[:, g, :, None], wd[g])
    return y.reshape(shape)


def _layer(x, c, conv_buf, attend, w_ada, b_ada, g_norm1, w_in, b_forget, conv_w, g_attn_out,
           g_conv_out, w_out, g_norm2, w_rg, b_rg, w_re, b_re, w_eg, w_eu, w_ed):
    sh1, sc1, ga1, sh2, sc2, ga2 = _adaln(c, w_ada, b_ada, 6)
    h = _rms_norm(x, g_norm1) * (1.0 + sc1) + sh1
    q, k, v, logf, conv_in, gate_b = _mixer_inputs(h, w_in, b_forget)
    attn = attend(q, k, v, logf)
    conv_y, new_buf = _short_conv(conv_in, conv_buf, conv_w)
    x = x + ga1 * _merge_heads(attn, conv_y, gate_b, g_attn_out, g_conv_out, w_out)
    h = _rms_norm(x, g_norm2) * (1.0 + sc2) + sh2
    x = x + ga2 * _hier_moe(h, w_rg, b_rg, w_re, b_re, w_eg, w_eu, w_ed)
    return x, k, v, logf, new_buf


def setup_inputs(seed: int = 0) -> dict:
    key = jax.random.key(seed)
    ks = jax.random.split(key, 32)
    D = D_MODEL
    n_pages = PAST_LEN // PAGE_SIZE
    n_used = DEC_BATCH * n_pages
    n_pool = n_used + max(1, n_used // 4)
    nrm = lambda k, shape, s: s * jax.random.normal(k, shape, jnp.float32)
    page_table = jax.random.permutation(ks[8], n_pool)[:n_used].reshape(DEC_BATCH, n_pages).astype(jnp.int32)
    return {
        "x_prompt": nrm(ks[0], (BATCH, SEQ, D), 1.0),
        "x_sample": nrm(ks[1], (DEC_BATCH, DEC_SEQ, D), 1.0),
        "c_prompt": nrm(ks[2], (BATCH, D), 1.0),
        "c_sample": nrm(ks[3], (DEC_BATCH, D), 1.0),
        "cache_k": nrm(ks[4], (DEPTH, n_pool, PAGE_SIZE, N_HEADS, HEAD_DIM), 1.0),
        "cache_v": nrm(ks[5], (DEPTH, n_pool, PAGE_SIZE, N_HEADS, HEAD_DIM), 1.0),
        "cache_logf": jax.nn.log_sigmoid(FORGET_BIAS_INIT + nrm(ks[6], (DEPTH, n_pool, PAGE_SIZE, N_HEADS), 1.0)),
        "state_conv": nrm(ks[7], (DEPTH, DEC_BATCH, CONV_K - 1, CONV_WIDTH), 1.0),
        "page_table": page_table,
        "w_ada": nrm(ks[9], (DEPTH, D, 6 * D), 0.5 * D ** -0.5),
        "b_ada": nrm(ks[10], (DEPTH, 6 * D), 0.02),
        "g_norm1": 1.0 + nrm(ks[11], (DEPTH, D), 0.01),
        "w_in": nrm(ks[12], (DEPTH, D, IN_WIDTH), D ** -0.5),
        "b_forget": FORGET_BIAS_INIT + nrm(ks[13], (DEPTH, N_HEADS), 0.5),
        "conv_w": nrm(ks[14], (DEPTH, CONV_K, CONV_WIDTH), CONV_K ** -0.5),
        "g_attn_out": 1.0 + nrm(ks[15], (DEPTH, N_HEADS, HEAD_DIM), 0.01),
        "g_conv_out": 1.0 + nrm(ks[16], (DEPTH, CONV_GROUPS, CONV_GROUP_DIM), 0.01),
        "w_out": nrm(ks[17], (DEPTH, MIX_WIDTH, D), MIX_WIDTH ** -0.5),
        "g_norm2": 1.0 + nrm(ks[18], (DEPTH, D), 0.01),
        "w_router_group": nrm(ks[19], (DEPTH, D, N_GROUPS), D ** -0.5),
        "b_router_group": nrm(ks[20], (DEPTH, N_GROUPS), 0.01),
        "w_router_expert": nrm(ks[21], (DEPTH, D, N_EXPERTS), D ** -0.5),
        "b_router_expert": nrm(ks[22], (DEPTH, N_EXPERTS), 0.01),
        "w_expert_gate": nrm(ks[23], (DEPTH, N_EXPERTS, D, D_EXPERT), D ** -0.5),
        "w_expert_up": nrm(ks[24], (DEPTH, N_EXPERTS, D, D_EXPERT), D ** -0.5),
        "w_expert_down": nrm(ks[25], (DEPTH, N_EXPERTS, D_EXPERT, D), D_EXPERT ** -0.5),
        "w_ada_final": nrm(ks[26], (D, 2 * D), 0.5 * D ** -0.5),
        "b_ada_final": nrm(ks[27], (2 * D,), 0.02),
        "g_final": 1.0 + nrm(ks[28], (D,), 0.01),
    }


def reference(x_prompt, x_sample, c_prompt, c_sample, cache_k, cache_v, cache_logf, state_conv,
              page_table, w_ada, b_ada, g_norm1, w_in, b_forget, conv_w, g_attn_out, g_conv_out,
              w_out, g_norm2, w_router_group, b_router_group, w_router_expert, b_router_expert,
              w_expert_gate, w_expert_up, w_expert_down, w_ada_final, b_ada_final, g_final):
    xp, xs = x_prompt, x_sample
    kp_l, vp_l, fp_l, cp_l, ks_l, vs_l, fs_l, cs_l = [], [], [], [], [], [], [], []
    for l in range(DEPTH):
        lw = (w_ada[l], b_ada[l], g_norm1[l], w_in[l], b_forget[l], conv_w[l], g_attn_out[l],
              g_conv_out[l], w_out[l], g_norm2[l], w_router_group[l], b_router_group[l],
              w_router_expert[l], b_router_expert[l], w_expert_gate[l], w_expert_up[l],
              w_expert_down[l])
        zero_buf = jnp.zeros((xp.shape[0], CONV_K - 1, CONV_WIDTH), xp.dtype)
        xp, kp, vp, fp, cp = _layer(xp, c_prompt, zero_buf, _fox_prompt, *lw)
        attend_s = functools.partial(_fox_sample, cache_k=cache_k[l], cache_v=cache_v[l],
                                     cache_logf=cache_logf[l], page_table=page_table)
        xs, ksn, vsn, fsn, csn = _layer(xs, c_sample, state_conv[l], attend_s, *lw)
        kp_l.append(kp); vp_l.append(vp); fp_l.append(fp); cp_l.append(cp)
        ks_l.append(ksn); vs_l.append(vsn); fs_l.append(fsn); cs_l.append(csn)
    shp, scp = _adaln(c_prompt, w_ada_final, b_ada_final, 2)
    shs, scs = _adaln(c_sample, w_ada_final, b_ada_final, 2)
    y_prompt = _rms_norm(xp, g_final) * (1.0 + scp) + shp
    y_sample = _rms_norm(xs, g_final) * (1.0 + scs) + shs
    k_prompt = jnp.stack(kp_l); v_prompt = jnp.stack(vp_l)
    logf_prompt = jnp.stack(fp_l); conv_prompt = jnp.stack(cp_l)
    k_sample = jnp.stack(ks_l); v_sample = jnp.stack(vs_l)
    logf_sample = jnp.stack(fs_l); conv_sample = jnp.stack(cs_l)
    return (y_prompt, y_sample, k_prompt, v_prompt, logf_prompt, conv_prompt,
            k_sample, v_sample, logf_sample, conv_sample)
```

```python
import functools

import jax
import jax.numpy as jnp
from jax import lax
from jax.experimental import pallas as pl
from jax.experimental.pallas import tpu as pltpu

F32 = jnp.float32
BF16 = jnp.bfloat16

D_MODEL = 1024
N_HEADS = 8
HEAD_DIM = 64
ATTN_WIDTH = N_HEADS * HEAD_DIM
CONV_WIDTH = D_MODEL - ATTN_WIDTH
CONV_GROUP_DIM = 64
CONV_K = 3
N_GROUPS = 4
EXPERTS_PER_GROUP = 8
N_EXPERTS = N_GROUPS * EXPERTS_PER_GROUP
D_EXPERT = 256
PAGE_SIZE = 128
EPS = 1e-6

LANES = 128
ROUTER_GROUP_LANE0 = 32
VMEM_LIMIT = 56 * 1024 * 1024
TOKEN_TILE = 512
ATTN_TILE = 512
PAGES_PER_STEP = 8


def _params(*sem):
    return pltpu.CompilerParams(dimension_semantics=sem, vmem_limit_bytes=VMEM_LIMIT)


def _rms(x, g):
    return x * lax.rsqrt(jnp.mean(x * x, axis=-1, keepdims=True) + EPS) * g


def _split3(a):
    hi = a.astype(BF16)
    r = a - hi.astype(F32)
    mid = r.astype(BF16)
    lo = (r - mid.astype(F32)).astype(BF16)
    return hi, mid, lo


def _dot3(a, b_exact):
    hi, mid, lo = _split3(a)
    d = lambda t: jnp.dot(t, b_exact, preferred_element_type=F32)
    return d(hi) + d(mid) + d(lo)


def _ada_kernel(c_ref, w_ref, b_ref, o_ref):
    c = c_ref[...]
    s = (c * jax.nn.sigmoid(c)).astype(BF16)
    o_ref[...] = jnp.dot(s, w_ref[...].astype(BF16), preferred_element_type=F32) + b_ref[...]


def _ada(c, w, b):
    m, d = c.shape
    n = w.shape[1]
    tn = 1024
    return pl.pallas_call(
        _ada_kernel,
        out_shape=jax.ShapeDtypeStruct((m, n), F32),
        grid=(n // tn,),
        in_specs=[pl.BlockSpec((m, d), lambda j: (0, 0)),
                  pl.BlockSpec((d, tn), lambda j: (0, j)),
                  pl.BlockSpec((1, tn), lambda j: (0, j))],
        out_specs=pl.BlockSpec((m, tn), lambda j: (0, j)),
        compiler_params=_params("arbitrary"),
        name="adaln_mod",
    )(c, w, b.reshape(1, n))


def _inproj_kernel(x_ref, mod_ref, g1_ref, wqkv_ref, wft_ref, bf_ref, wc_ref,
                   q_ref, k32_ref, v32_ref, kb_ref, vb_ref, lft_ref, cin_ref, gb_ref):
    a, c, d = ATTN_WIDTH, CONV_WIDTH, D_MODEL
    x = x_ref[0]
    sh1 = mod_ref[0, :, 0:d]
    sc1 = mod_ref[0, :, d:2 * d]
    h = _rms(x, g1_ref[...]) * (1.0 + sc1) + sh1
    hb = h.astype(BF16)
    qkv = jnp.dot(hb, wqkv_ref[...], preferred_element_type=F32)
    q_ref[0] = (qkv[:, 0:a] * (HEAD_DIM ** -0.5)).astype(BF16)
    k = qkv[:, a:2 * a]
    v = qkv[:, 2 * a:3 * a]
    k32_ref[0] = k
    v32_ref[0] = v
    kb_ref[0] = k.astype(BF16)
    vb_ref[0] = v.astype(BF16)
    ft = lax.dot_general(wft_ref[...], hb, (((1,), (1,)), ((), ())), preferred_element_type=F32)
    z = ft[0:N_HEADS, :] + bf_ref[...]
    lft_ref[0] = jnp.minimum(z, 0.0) - jnp.log1p(jnp.exp(-jnp.abs(z)))
    cv = jnp.dot(hb, wc_ref[...], preferred_element_type=F32)
    cin_ref[0] = cv[:, 2 * c:3 * c] * cv[:, 0:c]
    gb_ref[0] = cv[:, c:2 * c]


def _inproj(x, mod, g1, wqkv, wft, bf, wc, tm):
    g, t, d = x.shape
    r = mod.shape[1]
    a, c = ATTN_WIDTH, CONV_WIDTH
    row = lambda w, dt: jax.ShapeDtypeStruct((g, t, w), dt)
    rspec = lambda w: pl.BlockSpec((1, tm, w), lambda b, i: (b, i, 0))
    const = lambda shp: pl.BlockSpec(shp, lambda b, i: (0,) * len(shp))
    return pl.pallas_call(
        _inproj_kernel,
        out_shape=(row(a, BF16), row(a, F32), row(a, F32), row(a, BF16), row(a, BF16),
                   jax.ShapeDtypeStruct((g, N_HEADS, t), F32), row(c, F32), row(c, F32)),
        grid=(g, t // tm),
        in_specs=[rspec(d),
                  pl.BlockSpec((1, r, 6 * d), lambda b, i: (b, 0, 0)),
                  const((1, d)), const(wqkv.shape), const(wft.shape), const(bf.shape),
                  const(wc.shape)],
        out_specs=(rspec(a), rspec(a), rspec(a), rspec(a), rspec(a),
                   pl.BlockSpec((1, N_HEADS, tm), lambda b, i: (b, 0, i)), rspec(c), rspec(c)),
        compiler_params=_params("arbitrary", "arbitrary"),
        name="norm_inproj",
    )(x, mod, g1, wqkv, wft, bf, wc)


def _cumsum_kernel(lf_ref, u_ref, o_ref):
    t = lf_ref.shape[2]
    tc = u_ref.shape[0]
    carry = jnp.zeros((N_HEADS, 1), F32)
    for i in range(t // tc):
        blk = lf_ref[0, :, i * tc:(i + 1) * tc]
        cs = _dot3(blk, u_ref[...]) + carry
        o_ref[0, :, i * tc:(i + 1) * tc] = cs
        carry = cs[:, tc - 1:tc]


def _cumsum(lft):
    g, h, t = lft.shape
    tc = 512
    idx = jnp.arange(tc)
    upper = (idx[:, None] <= idx[None, :]).astype(BF16)
    return pl.pallas_call(
        _cumsum_kernel,
        out_shape=jax.ShapeDtypeStruct((g, h, t), F32),
        grid=(g,),
        in_specs=[pl.BlockSpec((1, h, t), lambda b: (b, 0, 0)),
                  pl.BlockSpec((tc, tc), lambda b: (0, 0))],
        out_specs=pl.BlockSpec((1, h, t), lambda b: (b, 0, 0)),
        compiler_params=_params("arbitrary"),
        name="logf_cumsum",
    )(lft, upper)


def _attn_prompt_kernel(q_ref, k_ref, v_ref, ck_ref, cq_ref, o_ref):
    t = q_ref.shape[1]
    ta = ATTN_TILE
    n = t // ta
    row = lax.broadcasted_iota(jnp.int32, (ta, ta), 0)
    col = lax.broadcasted_iota(jnp.int32, (ta, ta), 1)
    causal = col <= row
    for i in range(n):
        outs = []
        for hh in range(2):
            ls = slice(hh * HEAD_DIM, (hh + 1) * HEAD_DIM)
            qh = q_ref[0, i * ta:(i + 1) * ta, ls]
            cq = cq_ref[0, 0, i * ta:(i + 1) * ta, hh:hh + 1]
            m = jnp.full((ta, 1), -jnp.inf, F32)
            l = jnp.zeros((ta, 1), F32)
            acc = jnp.zeros((ta, HEAD_DIM), F32)
            for j in range(i + 1):
                kh = k_ref[0, j * ta:(j + 1) * ta, ls]
                vh = v_ref[0, j * ta:(j + 1) * ta, ls]
                s = lax.dot_general(qh, kh, (((1,), (1,)), ((), ())), preferred_element_type=F32)
                s = s + cq - ck_ref[0, 0, hh:hh + 1, j * ta:(j + 1) * ta]
                if j == i:
                    s = jnp.where(causal, s, -jnp.inf)
                m_new = jnp.maximum(m, jnp.max(s, axis=-1, keepdims=True))
                alpha = jnp.exp(m - m_new)
                p = jnp.exp(s - m_new)
                l = alpha * l + jnp.sum(p, axis=-1, keepdims=True)
                acc = alpha * acc + jnp.dot(p.astype(BF16), vh, preferred_element_type=F32)
                m = m_new
            outs.append(acc / l)
        o_ref[0, i * ta:(i + 1) * ta, :] = jnp.concatenate(outs, axis=-1)


def _attn_prompt(q, kb, vb, cum_rows, cum_cols):
    g, t, a = q.shape
    hp = a // LANES
    spec = pl.BlockSpec((1, t, LANES), lambda b, p: (b, 0, p))
    return pl.pallas_call(
        _attn_prompt_kernel,
        out_shape=jax.ShapeDtypeStruct((g, t, a), F32),
        grid=(g, hp),
        in_specs=[spec, spec, spec,
                  pl.BlockSpec((1, 1, 2, t), lambda b, p: (b, p, 0, 0)),
                  pl.BlockSpec((1, 1, t, 2), lambda b, p: (b, p, 0, 0))],
        out_specs=spec,
        compiler_params=_params("arbitrary", "arbitrary"),
        name="fox_prompt_attn",
    )(q, kb, vb, cum_rows, cum_cols)


def _attn_sample_kernel(pt_ref, qbd_ref, kn_ref, vn_ref, fn_ref, lmat_ref, *rest):
    np_ = PAGES_PER_STEP
    k_refs, v_refs, f_refs = rest[0:np_], rest[np_:2 * np_], rest[2 * np_:3 * np_]
    o_ref, m_sc, l_sc, acc_sc, carry_sc = rest[3 * np_:]
    step = pl.program_id(1)
    qbd = qbd_ref[0]

    @pl.when(step == 0)
    def _():
        kn = kn_ref[0].astype(BF16).astype(F32)
        m_sc[...] = jnp.sum(qbd.astype(F32) * kn, axis=-1, keepdims=True)
        l_sc[...] = jnp.ones_like(l_sc)
        acc_sc[...] = jnp.broadcast_to(vn_ref[0].astype(BF16).astype(F32), acc_sc.shape)
        carry_sc[...] = fn_ref[0]

    m, l, acc, carry = m_sc[...], l_sc[...], acc_sc[...], carry_sc[...]
    for i in range(np_):
        k = k_refs[i][0].astype(BF16)
        v = v_refs[i][0].astype(BF16)
        ft = f_refs[i][0]
        s = lax.dot_general(qbd, k, (((1,), (1,)), ((), ())), preferred_element_type=F32)
        s = s + carry + _dot3(ft, lmat_ref[...])
        carry = carry + jnp.sum(ft, axis=-1, keepdims=True)
        m_new = jnp.maximum(m, jnp.max(s, axis=-1, keepdims=True))
        alpha = jnp.exp(m - m_new)
        p = jnp.exp(s - m_new)
        l = alpha * l + jnp.sum(p, axis=-1, keepdims=True)
        acc = alpha * acc + jnp.dot(p.astype(BF16), v, preferred_element_type=F32)
        m = m_new
    m_sc[...], l_sc[...], acc_sc[...], carry_sc[...] = m, l, acc, carry

    @pl.when(step == pl.num_programs(1) - 1)
    def _():
        res = acc / l
        head = lax.broadcasted_iota(jnp.int32, res.shape, 0)
        lane_head = lax.broadcasted_iota(jnp.int32, res.shape, 1) // HEAD_DIM
        o_ref[0] = jnp.sum(jnp.where(head == lane_head, res, 0.0), axis=0, keepdims=True)


def _attn_sample(page_table, qbd, k_new, v_new, f_new, cache_k, cache_v, cache_ft):
    bsz, n_pages = page_table.shape
    a = ATTN_WIDTH
    np_ = PAGES_PER_STEP
    idx = jnp.arange(PAGE_SIZE)
    lmat = (idx[:, None] > idx[None, :]).astype(BF16)
    per_b = lambda shp: pl.BlockSpec(shp, lambda b, s, pt: (b, 0, 0))

    def page_spec(shape, i):
        return pl.BlockSpec(shape, lambda b, s, pt: (pt[b * n_pages + n_pages - 1 - (s * np_ + i)], 0, 0))

    in_specs = [per_b((1, N_HEADS, a)), per_b((1, 1, a)), per_b((1, 1, a)), per_b((1, N_HEADS, 1)),
                pl.BlockSpec((PAGE_SIZE, PAGE_SIZE), lambda b, s, pt: (0, 0))]
    in_specs += [page_spec((1, PAGE_SIZE, a), i) for i in range(np_)]
    in_specs += [page_spec((1, PAGE_SIZE, a), i) for i in range(np_)]
    in_specs += [page_spec((1, N_HEADS, PAGE_SIZE), i) for i in range(np_)]
    return pl.pallas_call(
        _attn_sample_kernel,
        out_shape=jax.ShapeDtypeStruct((bsz, 1, a), F32),
        grid_spec=pltpu.PrefetchScalarGridSpec(
            num_scalar_prefetch=1,
            grid=(bsz, n_pages // np_),
            in_specs=in_specs,
            out_specs=per_b((1, 1, a)),
            scratch_shapes=[pltpu.VMEM((N_HEADS, 1), F32), pltpu.VMEM((N_HEADS, 1), F32),
                            pltpu.VMEM((N_HEADS, a), F32), pltpu.VMEM((N_HEADS, 1), F32)]),
        compiler_params=_params("arbitrary", "arbitrary"),
        name="fox_sample_attn",
    )(page_table.reshape(-1), qbd, k_new, v_new, f_new, lmat,
      *([cache_k] * np_), *([cache_v] * np_), *([cache_ft] * np_))


def _group_rms(x, bd, g):
    ss = x * x
    hi = ss.astype(BF16)
    lo = (ss - hi.astype(F32)).astype(BF16)
    ms = (jnp.dot(hi, bd, preferred_element_type=F32) + jnp.dot(lo, bd, preferred_element_type=F32))
    return x * lax.rsqrt(ms * (1.0 / HEAD_DIM) + EPS) * g


def _route(logits):
    lane = lax.broadcasted_iota(jnp.int32, logits.shape, 1)
    big = jnp.int32(2 ** 30)
    g0 = ROUTER_GROUP_LANE0
    glog = jnp.where((lane >= g0) & (lane < g0 + N_GROUPS), logits, -jnp.inf)
    gmax = jnp.max(glog, axis=-1, keepdims=True)
    gidx = jnp.min(jnp.where(glog == gmax, lane, big), axis=-1, keepdims=True) - g0
    gp = 1.0 / jnp.sum(jnp.exp(glog - gmax), axis=-1, keepdims=True)
    lo = gidx * EXPERTS_PER_GROUP
    el = jnp.where((lane >= lo) & (lane < lo + EXPERTS_PER_GROUP), logits, -jnp.inf)
    m1 = jnp.max(el, axis=-1, keepdims=True)
    i1 = jnp.min(jnp.where(el == m1, lane, big), axis=-1, keepdims=True)
    el2 = jnp.where(lane == i1, -jnp.inf, el)
    m2 = jnp.max(el2, axis=-1, keepdims=True)
    i2 = jnp.min(jnp.where(el2 == m2, lane, big), axis=-1, keepdims=True)
    t = jnp.exp(m2 - m1)
    w1 = gp / (1.0 + t)
    w2 = w1 * t
    return jnp.where(lane == i1, w1, 0.0) + jnp.where(lane == i2, w2, 0.0)


def _merge_kernel(sample, attn_ref, cin_ref, prev_ref, prev2_ref, gb_ref, x_ref, mod_ref, cw_ref,
                  ga_ref, gc_ref, bd_ref, wout_ref, g2_ref, wrh_ref, wrl_ref, br_ref,
                  x1_ref, h2_ref, wd_ref):
    d, a = D_MODEL, ATTN_WIDTH
    cin = cin_ref[0]
    tm = cin.shape[0]
    w0, w1, w2 = cw_ref[0:1, :], cw_ref[1:2, :], cw_ref[2:3, :]
    if sample:
        conv_y = w0 * prev2_ref[0] + w1 * prev_ref[0] + w2 * cin
    else:
        first = pl.program_id(1) == 0
        tail = jnp.where(first, 0.0, prev_ref[0])
        ext = jnp.concatenate([tail, cin], axis=0)
        conv_y = w0 * ext[6:6 + tm] + w1 * ext[7:7 + tm] + w2 * cin
    bd = bd_ref[...]
    an = _group_rms(attn_ref[0], bd, ga_ref[...]).astype(BF16)
    cn = _group_rms(gb_ref[0] * conv_y, bd, gc_ref[...]).astype(BF16)
    mix = (jnp.dot(an, wout_ref[0:a, :], preferred_element_type=F32)
           + jnp.dot(cn, wout_ref[a:, :], preferred_element_type=F32))
    ga1 = mod_ref[0, :, 2 * d:3 * d]
    sh2 = mod_ref[0, :, 3 * d:4 * d]
    sc2 = mod_ref[0, :, 4 * d:5 * d]
    x1 = x_ref[0] + ga1 * mix
    x1_ref[0] = x1
    h2 = _rms(x1, g2_ref[...]) * (1.0 + sc2) + sh2
    hb = h2.astype(BF16)
    h2_ref[0] = hb
    hl = (h2 - hb.astype(F32)).astype(BF16)
    wrh = wrh_ref[...]
    logits = (jnp.dot(hb, wrh, preferred_element_type=F32)
              + jnp.dot(hl, wrh, preferred_element_type=F32)
              + jnp.dot(hb, wrl_ref[...], preferred_element_type=F32)) + br_ref[...]
    wd_ref[0] = _route(logits)


def _merge(sample, attn, cin, prev, prev2, gb, x, mod, cw, ga, gc, bd, wout, g2, wrh, wrl, br, tm):
    g, t, d = x.shape
    r = mod.shape[1]
    c = CONV_WIDTH
    rspec = lambda w: pl.BlockSpec((1, tm, w), lambda b, i: (b, i, 0))
    const = lambda shp: pl.BlockSpec(shp, lambda b, i: (0,) * len(shp))
    if sample:
        prev_spec = rspec(c)
    else:
        prev_spec = pl.BlockSpec((1, 8, c), lambda b, i: (b, jnp.maximum(i * (tm // 8) - 1, 0), 0))
    return pl.pallas_call(
        functools.partial(_merge_kernel, sample),
        out_shape=(jax.ShapeDtypeStruct((g, t, d), F32), jax.ShapeDtypeStruct((g, t, d), BF16),
                   jax.ShapeDtypeStruct((g, t, LANES), F32)),
        grid=(g, t // tm),
        in_specs=[rspec(ATTN_WIDTH), rspec(c), prev_spec, prev_spec, rspec(c), rspec(d),
                  pl.BlockSpec((1, r, 6 * d), lambda b, i: (b, 0, 0)),
                  const(cw.shape), const(ga.shape), const(gc.shape), const(bd.shape),
                  const(wout.shape), const(g2.shape), const(wrh.shape), const(wrl.shape),
                  const(br.shape)],
        out_specs=(rspec(d), rspec(d), rspec(LANES)),
        compiler_params=_params("arbitrary", "arbitrary"),
        name="mix_out_router",
    )(attn, cin, prev, prev2, gb, x, mod, cw, ga, gc, bd, wout, g2, wrh, wrl, br)


def _moe_kernel(h_ref, wd_ref, x1_ref, mod_ref, modf_ref, gf_ref, wg_ref, wu_ref, wdn_ref,
                y_ref, acc_ref):
    d = D_MODEL
    e = pl.program_id(2)

    @pl.when(e == 0)
    def _():
        acc_ref[...] = jnp.zeros_like(acc_ref)

    hb = h_ref[0]
    wdense = wd_ref[0]
    lane = lax.broadcasted_iota(jnp.int32, wdense.shape, 1)
    wcol = jnp.sum(jnp.where(lane == e, wdense, 0.0), axis=-1, keepdims=True)
    gt = jnp.dot(hb, wg_ref[0], preferred_element_type=F32)
    up = jnp.dot(hb, wu_ref[0], preferred_element_type=F32)
    hg = gt * jax.nn.sigmoid(gt) * up * wcol
    acc_ref[...] += jnp.dot(hg.astype(BF16), wdn_ref[0], preferred_element_type=F32)

    @pl.when(e == pl.num_programs(2) - 1)
    def _():
        ga2 = mod_ref[0, :, 5 * d:6 * d]
        x2 = x1_ref[0] + ga2 * acc_ref[...]
        shf = modf_ref[0, :, 0:d]
        scf = modf_ref[0, :, d:2 * d]
        y_ref[0] = _rms(x2, gf_ref[...]) * (1.0 + scf) + shf


def _moe(h2, wdense, x1, mod, modf, gf, wg, wu, wdn, tm):
    g, t, d = x1.shape
    r = mod.shape[1]
    rspec = lambda w: pl.BlockSpec((1, tm, w), lambda b, i, e: (b, i, 0))
    return pl.pallas_call(
        _moe_kernel,
        out_shape=jax.ShapeDtypeStruct((g, t, d), F32),
        grid=(g, t // tm, N_EXPERTS),
        in_specs=[rspec(d), rspec(LANES), rspec(d),
                  pl.BlockSpec((1, r, 6 * d), lambda b, i, e: (b, 0, 0)),
                  pl.BlockSpec((1, r, 2 * d), lambda b, i, e: (b, 0, 0)),
                  pl.BlockSpec((1, d), lambda b, i, e: (0, 0)),
                  pl.BlockSpec((1, d, D_EXPERT), lambda b, i, e: (e, 0, 0)),
                  pl.BlockSpec((1, d, D_EXPERT), lambda b, i, e: (e, 0, 0)),
                  pl.BlockSpec((1, D_EXPERT, d), lambda b, i, e: (e, 0, 0))],
        out_specs=rspec(d),
        scratch_shapes=[pltpu.VMEM((tm, d), F32)],
        compiler_params=_params("arbitrary", "arbitrary", "arbitrary"),
        name="moe_final",
    )(h2, wdense, x1, mod, modf, gf, wg, wu, wdn)


def kernel(x_prompt, x_sample, c_prompt, c_sample, cache_k, cache_v, cache_logf, state_conv, page_table, w_ada, b_ada, g_norm1, w_in, b_forget, conv_w, g_attn_out, g_conv_out, w_out, g_norm2, w_router_group, b_router_group, w_router_expert, b_router_expert, w_expert_gate, w_expert_up, w_expert_down, w_ada_final, b_ada_final, g_final):
    d, a, c, h = D_MODEL, ATTN_WIDTH, CONV_WIDTH, N_HEADS
    bsz, t, _ = x_prompt.shape
    dbs = x_sample.shape[0]
    assert w_ada.shape[0] == 1 and x_sample.shape[1] == 1

    c_all = jnp.concatenate([c_prompt, c_sample], axis=0)
    mod = _ada(c_all, w_ada[0], b_ada[0])
    modf = _ada(c_all, w_ada_final, b_ada_final)
    mod_p, mod_s = mod[:bsz].reshape(bsz, 1, 6 * d), mod[bsz:].reshape(1, dbs, 6 * d)
    modf_p, modf_s = modf[:bsz].reshape(bsz, 1, 2 * d), modf[bsz:].reshape(1, dbs, 2 * d)

    w = w_in[0]
    wqkv = w[:, 0:3 * a].astype(BF16)
    wft = jnp.zeros((16, d), BF16).at[0:h].set(w[:, 3 * a:3 * a + h].T.astype(BF16))
    wc = w[:, 3 * a + h:].astype(BF16)
    bf = b_forget[0].reshape(h, 1)
    g1 = g_norm1[0].reshape(1, d)
    g2 = g_norm2[0].reshape(1, d)
    gf = g_final.reshape(1, d)
    ga = g_attn_out[0].reshape(1, a)
    gc = g_conv_out[0].reshape(1, c)
    cw = conv_w[0]
    wout = w_out[0].astype(BF16)
    lane_group = jnp.arange(a) // HEAD_DIM
    bd = (lane_group[:, None] == lane_group[None, :]).astype(BF16)
    wr = jnp.zeros((d, LANES), F32)
    wr = wr.at[:, 0:N_EXPERTS].set(w_router_expert[0])
    wr = wr.at[:, ROUTER_GROUP_LANE0:ROUTER_GROUP_LANE0 + N_GROUPS].set(w_router_group[0])
    wrh = wr.astype(BF16)
    wrl = (wr - wrh.astype(F32)).astype(BF16)
    br = jnp.zeros((1, LANES), F32)
    br = br.at[0, 0:N_EXPERTS].set(b_router_expert[0])
    br = br.at[0, ROUTER_GROUP_LANE0:ROUTER_GROUP_LANE0 + N_GROUPS].set(b_router_group[0])
    wg = w_expert_gate[0].astype(BF16)
    wu = w_expert_up[0].astype(BF16)
    wdn = w_expert_down[0].astype(BF16)

    q_p, k_p, v_p, kb_p, vb_p, lft_p, cin_p, gb_p = _inproj(
        x_prompt, mod_p, g1, wqkv, wft, bf, wc, TOKEN_TILE)
    cum = _cumsum(lft_p)
    cum_rows = cum.reshape(bsz, h // 2, 2, t)
    cum_cols = cum_rows.transpose(0, 1, 3, 2)
    attn_p = _attn_prompt(q_p, kb_p, vb_p, cum_rows, cum_cols)
    x1_p, h2_p, wd_p = _merge(False, attn_p, cin_p, cin_p, cin_p, gb_p, x_prompt, mod_p, cw, ga, gc,
                              bd, wout, g2, wrh, wrl, br, TOKEN_TILE)
    y_p = _moe(h2_p, wd_p, x1_p, mod_p, modf_p, gf, wg, wu, wdn, TOKEN_TILE)

    xs = x_sample.reshape(1, dbs, d)
    q_s, k_s, v_s, _, _, lft_s, cin_s, gb_s = _inproj(xs, mod_s, g1, wqkv, wft, bf, wc, dbs)
    head_of_lane = jnp.arange(a) // HEAD_DIM
    qbd = jnp.where(head_of_lane[None, None, :] == jnp.arange(h)[None, :, None],
                    q_s.reshape(dbs, 1, a), jnp.zeros((), BF16))
    n_pool = cache_k.shape[1]
    attn_s = _attn_sample(
        page_table, qbd, k_s.reshape(dbs, 1, a), v_s.reshape(dbs, 1, a),
        lft_s[0].T.reshape(dbs, h, 1),
        cache_k[0].reshape(n_pool, PAGE_SIZE, a), cache_v[0].reshape(n_pool, PAGE_SIZE, a),
        cache_logf[0].transpose(0, 2, 1))
    st = state_conv[0]
    x1_s, h2_s, wd_s = _merge(True, attn_s.reshape(1, dbs, a), cin_s, st[:, 1][None], st[:, 0][None],
                              gb_s, xs, mod_s, cw, ga, gc, bd, wout, g2, wrh, wrl, br, dbs)
    y_s = _moe(h2_s, wd_s, x1_s, mod_s, modf_s, gf, wg, wu, wdn, dbs)

    heads = lambda z, n, tt: z.reshape(1, n, tt, h, HEAD_DIM)
    return (y_p, y_s.reshape(dbs, 1, d),
            heads(k_p, bsz, t), heads(v_p, bsz, t),
            lft_p.transpose(0, 2, 1)[None],
            cin_p[:, t - (CONV_K - 1):, :][None],
            heads(k_s, dbs, 1), heads(v_s, dbs, 1),
            lft_s[0].T.reshape(1, dbs, 1, h),
            jnp.stack([st[:, 1], cin_s[0]], axis=1)[None])
```

```python
import functools

import numpy as np
import jax
import jax.numpy as jnp
from jax import lax
from jax.experimental import pallas as pl
from jax.experimental.pallas import tpu as pltpu

F32 = jnp.float32
BF16 = jnp.bfloat16
I32 = jnp.int32

D_MODEL = 1024
N_HEADS = 8
HEAD_DIM = 64
ATTN_WIDTH = N_HEADS * HEAD_DIM
CONV_WIDTH = D_MODEL - ATTN_WIDTH
CONV_K = 3
N_GROUPS = 4
EXPERTS_PER_GROUP = 8
N_EXPERTS = N_GROUPS * EXPERTS_PER_GROUP
D_EXPERT = 256
PAGE_SIZE = 128
EPS = 1e-6

LANES = 128
ROUTER_GROUP_LANE0 = 32
VMEM_LIMIT = 56 * 1024 * 1024
TOKEN_TILE = 512
ATTN_TILE = 512
PAGES_PER_STEP = 16

PAIRS_PER_GROUP = EXPERTS_PER_GROUP * (EXPERTS_PER_GROUP - 1) // 2
N_BUCKETS = N_GROUPS * PAIRS_PER_GROUP
MOE_TILE = 256
PACKED_WIDTH = D_MODEL // 2
ROW_WORDS = PACKED_WIDTH + LANES


def _pair_tables():
    lo, hi = [], []
    for g in range(N_GROUPS):
        for a in range(EXPERTS_PER_GROUP):
            for b in range(a + 1, EXPERTS_PER_GROUP):
                lo.append(g * EXPERTS_PER_GROUP + a)
                hi.append(g * EXPERTS_PER_GROUP + b)
    return np.asarray(lo, np.int32), np.asarray(hi, np.int32)


def _params(*sem):
    return pltpu.CompilerParams(dimension_semantics=sem, vmem_limit_bytes=VMEM_LIMIT)


def _rms(x, g):
    return x * lax.rsqrt(jnp.mean(x * x, axis=-1, keepdims=True) + EPS) * g


def _split3(a):
    hi = a.astype(BF16)
    r = a - hi.astype(F32)
    mid = r.astype(BF16)
    lo = (r - mid.astype(F32)).astype(BF16)
    return hi, mid, lo


def _dot3(a, b_exact):
    hi, mid, lo = _split3(a)
    d = lambda t: jnp.dot(t, b_exact, preferred_element_type=F32)
    return d(hi) + d(mid) + d(lo)


def _dot_nt(a, b):
    return lax.dot_general(a, b, (((1,), (1,)), ((), ())), preferred_element_type=F32)


def _ada_kernel(c_ref, w_ref, b_ref, o_ref):
    c = c_ref[...]
    s = (c * jax.nn.sigmoid(c)).astype(BF16)
    o_ref[...] = jnp.dot(s, w_ref[...].astype(BF16), preferred_element_type=F32) + b_ref[...]


def _ada(c, w, b):
    m, d = c.shape
    n = w.shape[1]
    tn = 1024
    return pl.pallas_call(
        _ada_kernel,
        out_shape=jax.ShapeDtypeStruct((m, n), F32),
        grid=(n // tn,),
        in_specs=[pl.BlockSpec((m, d), lambda j: (0, 0)),
                  pl.BlockSpec((d, tn), lambda j: (0, j)),
                  pl.BlockSpec((1, tn), lambda j: (0, j))],
        out_specs=pl.BlockSpec((m, tn), lambda j: (0, j)),
        compiler_params=_params("arbitrary"),
        name="adaln_mod",
    )(c, w, b.reshape(1, n))


def _inproj_kernel(prompt, x_ref, mod_ref, g1_ref, wq_ref, wkt_ref, wvt_ref, wv_ref, wft_ref, bf_ref,
                   wc_ref, *outs):
    c, d = CONV_WIDTH, D_MODEL
    x = x_ref[0]
    sh1 = mod_ref[0, :, 0:d]
    sc1 = mod_ref[0, :, d:2 * d]
    h = _rms(x, g1_ref[...]) * (1.0 + sc1) + sh1
    hb = h.astype(BF16)
    q = jnp.dot(hb, wq_ref[...], preferred_element_type=F32)
    if prompt:
        q_ref, kt32_ref, vt32_ref, ktb_ref, vb_ref, lft_ref, cin_ref, gb_ref = outs
        kt = _dot_nt(wkt_ref[...], hb)
        vt = _dot_nt(wvt_ref[...], hb)
        kt32_ref[0] = kt
        vt32_ref[0] = vt
        ktb_ref[0] = kt.astype(BF16)
        vb_ref[0] = jnp.dot(hb, wv_ref[...], preferred_element_type=F32).astype(BF16)
    else:
        q_ref, k32_ref, v32_ref, lft_ref, cin_ref, gb_ref = outs
        k32_ref[0] = _dot_nt(hb, wkt_ref[...])
        v32_ref[0] = jnp.dot(hb, wv_ref[...], preferred_element_type=F32)
    q_ref[0] = (q * (HEAD_DIM ** -0.5)).astype(BF16)
    z = _dot_nt(wft_ref[...], hb)[0:N_HEADS, :] + bf_ref[...]
    lft_ref[0] = jnp.minimum(z, 0.0) - jnp.log1p(jnp.exp(-jnp.abs(z)))
    cv = jnp.dot(hb, wc_ref[...], preferred_element_type=F32)
    cin_ref[0] = cv[:, 2 * c:3 * c] * cv[:, 0:c]
    gb_ref[0] = cv[:, c:2 * c]


def _inproj(prompt, x, mod, g1, wq, wkt, wvt, wv, wft, bf, wc, tm):
    g, t, d = x.shape
    r = mod.shape[1]
    a, c = ATTN_WIDTH, CONV_WIDTH
    row = lambda w, dt: jax.ShapeDtypeStruct((g, t, w), dt)
    col = lambda w, dt: jax.ShapeDtypeStruct((g, w, t), dt)
    rspec = lambda w: pl.BlockSpec((1, tm, w), lambda b, i: (b, i, 0))
    cspec = lambda w: pl.BlockSpec((1, w, tm), lambda b, i: (b, 0, i))
    const = lambda arr: pl.BlockSpec(arr.shape, lambda b, i: (0,) * arr.ndim)
    if prompt:
        out_shape = (row(a, BF16), col(a, F32), col(a, F32), col(a, BF16), row(a, BF16),
                     col(N_HEADS, F32), row(c, F32), row(c, F32))
        out_specs = (rspec(a), cspec(a), cspec(a), cspec(a), rspec(a), cspec(N_HEADS), rspec(c), rspec(c))
    else:
        out_shape = (row(a, BF16), row(a, F32), row(a, F32), col(N_HEADS, F32), row(c, F32), row(c, F32))
        out_specs = (rspec(a), rspec(a), rspec(a), cspec(N_HEADS), rspec(c), rspec(c))
    return pl.pallas_call(
        functools.partial(_inproj_kernel, prompt),
        out_shape=out_shape,
        grid=(g, t // tm),
        in_specs=[rspec(d),
                  pl.BlockSpec((1, r, 6 * d), lambda b, i: (b, 0, 0)),
                  const(g1), const(wq), const(wkt), const(wvt), const(wv), const(wft), const(bf),
                  const(wc)],
        out_specs=out_specs,
        compiler_params=_params("arbitrary", "arbitrary"),
        name="norm_inproj",
    )(x, mod, g1, wq, wkt, wvt, wv, wft, bf, wc)


def _cumsum_kernel(lf_ref, u_ref, o_ref):
    t = lf_ref.shape[2]
    tc = u_ref.shape[0]
    carry = jnp.zeros((N_HEADS, 1), F32)
    for i in range(t // tc):
        blk = lf_ref[0, :, i * tc:(i + 1) * tc]
        cs = _dot3(blk, u_ref[...]) + carry
        o_ref[0, :, i * tc:(i + 1) * tc] = cs
        carry = cs[:, tc - 1:tc]


def _cumsum(lft):
    g, h, t = lft.shape
    tc = 512
    idx = jnp.arange(tc)
    upper = (idx[:, None] <= idx[None, :]).astype(BF16)
    return pl.pallas_call(
        _cumsum_kernel,
        out_shape=jax.ShapeDtypeStruct((g, h, t), F32),
        grid=(g,),
        in_specs=[pl.BlockSpec((1, h, t), lambda b: (b, 0, 0)),
                  pl.BlockSpec((tc, tc), lambda b: (0, 0))],
        out_specs=pl.BlockSpec((1, h, t), lambda b: (b, 0, 0)),
        compiler_params=_params("arbitrary"),
        name="logf_cumsum",
    )(lft, upper)


def _attn_prompt_kernel(q_ref, kt_ref, v_ref, ck_ref, cq_ref, o_ref):
    t = q_ref.shape[1]
    ta = ATTN_TILE
    n = t // ta
    row = lax.broadcasted_iota(jnp.int32, (ta, ta), 0)
    col = lax.broadcasted_iota(jnp.int32, (ta, ta), 1)
    causal = col <= row
    for i in range(n):
        outs = []
        for hh in range(2):
            ls = slice(hh * HEAD_DIM, (hh + 1) * HEAD_DIM)
            qh = q_ref[0, i * ta:(i + 1) * ta, ls]
            cq = cq_ref[0, 0, i * ta:(i + 1) * ta, hh:hh + 1]
            m = jnp.full((ta, 1), -jnp.inf, F32)
            l = jnp.zeros((ta, 1), F32)
            acc = jnp.zeros((ta, HEAD_DIM), F32)
            for j in range(i + 1):
                kth = kt_ref[0, ls, j * ta:(j + 1) * ta]
                vh = v_ref[0, j * ta:(j + 1) * ta, ls]
                s = jnp.dot(qh, kth, preferred_element_type=F32)
                s = s + cq - ck_ref[0, 0, hh:hh + 1, j * ta:(j + 1) * ta]
                if j == i:
                    s = jnp.where(causal, s, -jnp.inf)
                m_new = jnp.maximum(m, jnp.max(s, axis=-1, keepdims=True))
                alpha = jnp.exp(m - m_new)
                p = jnp.exp(s - m_new)
                l = alpha * l + jnp.sum(p, axis=-1, keepdims=True)
                acc = alpha * acc + jnp.dot(p.astype(BF16), vh, preferred_element_type=F32)
                m = m_new
            outs.append(acc / l)
        o_ref[0, i * ta:(i + 1) * ta, :] = jnp.concatenate(outs, axis=-1)


def _attn_prompt(q, ktb, vb, cum_rows, cum_cols):
    g, t, a = q.shape
    hp = a // LANES
    spec = pl.BlockSpec((1, t, LANES), lambda b, p: (b, 0, p))
    return pl.pallas_call(
        _attn_prompt_kernel,
        out_shape=jax.ShapeDtypeStruct((g, t, a), F32),
        grid=(g, hp),
        in_specs=[spec, pl.BlockSpec((1, LANES, t), lambda b, p: (b, p, 0)), spec,
                  pl.BlockSpec((1, 1, 2, t), lambda b, p: (b, p, 0, 0)),
                  pl.BlockSpec((1, 1, t, 2), lambda b, p: (b, p, 0, 0))],
        out_specs=spec,
        compiler_params=_params("arbitrary", "arbitrary"),
        name="fox_prompt_attn",
    )(q, ktb, vb, cum_rows, cum_cols)


def _attn_sample_kernel(pt_ref, qbd_ref, kn_ref, vn_ref, fn_ref, lmat_ref, *rest):
    np_ = PAGES_PER_STEP
    kt_refs, vt_refs, f_refs = rest[0:np_], rest[np_:2 * np_], rest[2 * np_:3 * np_]
    o_ref, m_sc, l_sc, acc_sc, carry_sc = rest[3 * np_:]
    step = pl.program_id(1)
    qbd = qbd_ref[0]

    @pl.when(step == 0)
    def _():
        kn = kn_ref[0].astype(BF16).astype(F32)
        m_sc[...] = jnp.sum(qbd.astype(F32) * kn, axis=-1, keepdims=True)
        l_sc[...] = jnp.ones_like(l_sc)
        acc_sc[...] = jnp.broadcast_to(vn_ref[0].astype(BF16).astype(F32), acc_sc.shape)
        carry_sc[...] = fn_ref[0]

    f_all = jnp.concatenate([f_refs[i][0] for i in range(np_)], axis=0)
    suffix = _dot3(f_all, lmat_ref[...])
    totals = jnp.sum(f_all, axis=-1, keepdims=True)
    carry = carry_sc[...]
    s = []
    for i in range(np_):
        sl = slice(i * N_HEADS, (i + 1) * N_HEADS)
        qk = jnp.dot(qbd, kt_refs[i][0].astype(BF16), preferred_element_type=F32)
        s.append(qk + carry + suffix[sl])
        carry = carry + totals[sl]
    carry_sc[...] = carry
    smax = functools.reduce(jnp.maximum, s)
    m = m_sc[...]
    m_new = jnp.maximum(m, jnp.max(smax, axis=-1, keepdims=True))
    alpha = jnp.exp(m - m_new)
    p = [jnp.exp(si - m_new) for si in s]
    l_sc[...] = alpha * l_sc[...] + jnp.sum(functools.reduce(jnp.add, p), axis=-1, keepdims=True)
    pv = functools.reduce(jnp.add, [_dot_nt(p[i].astype(BF16), vt_refs[i][0].astype(BF16))
                                    for i in range(np_)])
    acc_sc[...] = alpha * acc_sc[...] + pv
    m_sc[...] = m_new

    @pl.when(step == pl.num_programs(1) - 1)
    def _():
        res = acc_sc[...] / l_sc[...]
        head = lax.broadcasted_iota(jnp.int32, res.shape, 0)
        lane_head = lax.broadcasted_iota(jnp.int32, res.shape, 1) // HEAD_DIM
        o_ref[0] = jnp.sum(jnp.where(head == lane_head, res, 0.0), axis=0, keepdims=True)


def _attn_sample(page_table, qbd, k_new, v_new, f_new, cache_kt, cache_vt, cache_ft):
    bsz, n_pages = page_table.shape
    a = ATTN_WIDTH
    np_ = PAGES_PER_STEP
    idx = jnp.arange(PAGE_SIZE)
    lmat = (idx[:, None] > idx[None, :]).astype(BF16)
    per_b = lambda shp: pl.BlockSpec(shp, lambda b, s, pt: (b, 0, 0))

    def page_spec(shape, i):
        return pl.BlockSpec(shape, lambda b, s, pt: (pt[b * n_pages + n_pages - 1 - (s * np_ + i)], 0, 0))

    in_specs = [per_b((1, N_HEADS, a)), per_b((1, 1, a)), per_b((1, 1, a)), per_b((1, N_HEADS, 1)),
                pl.BlockSpec((PAGE_SIZE, PAGE_SIZE), lambda b, s, pt: (0, 0))]
    in_specs += [page_spec((1, a, PAGE_SIZE), i) for i in range(np_)]
    in_specs += [page_spec((1, a, PAGE_SIZE), i) for i in range(np_)]
    in_specs += [page_spec((1, N_HEADS, PAGE_SIZE), i) for i in range(np_)]
    return pl.pallas_call(
        _attn_sample_kernel,
        out_shape=jax.ShapeDtypeStruct((bsz, 1, a), F32),
        grid_spec=pltpu.PrefetchScalarGridSpec(
            num_scalar_prefetch=1,
            grid=(bsz, n_pages // np_),
            in_specs=in_specs,
            out_specs=per_b((1, 1, a)),
            scratch_shapes=[pltpu.VMEM((N_HEADS, 1), F32), pltpu.VMEM((N_HEADS, 1), F32),
                            pltpu.VMEM((N_HEADS, a), F32), pltpu.VMEM((N_HEADS, 1), F32)]),
        compiler_params=_params("arbitrary", "arbitrary"),
        name="fox_sample_attn",
    )(page_table.reshape(-1), qbd, k_new, v_new, f_new, lmat,
      *([cache_kt] * np_), *([cache_vt] * np_), *([cache_ft] * np_))


def _group_rms(x, bd, g):
    ss = x * x
    hi = ss.astype(BF16)
    lo = (ss - hi.astype(F32)).astype(BF16)
    ms = (jnp.dot(hi, bd, preferred_element_type=F32) + jnp.dot(lo, bd, preferred_element_type=F32))
    return x * lax.rsqrt(ms * (1.0 / HEAD_DIM) + EPS) * g


def _route(logits):
    lane = lax.broadcasted_iota(jnp.int32, logits.shape, 1).astype(F32)
    big = jnp.float32(1e9)
    g0 = float(ROUTER_GROUP_LANE0)
    epg = float(EXPERTS_PER_GROUP)
    glog = jnp.where((lane >= g0) & (lane < g0 + N_GROUPS), logits, -jnp.inf)
    gmax = jnp.max(glog, axis=-1, keepdims=True)
    gidx = jnp.min(jnp.where(glog == gmax, lane, big), axis=-1, keepdims=True) - g0
    gp = 1.0 / jnp.sum(jnp.exp(glog - gmax), axis=-1, keepdims=True)
    lo = gidx * epg
    el = jnp.where((lane >= lo) & (lane < lo + epg), logits, -jnp.inf)
    m1 = jnp.max(el, axis=-1, keepdims=True)
    i1 = jnp.min(jnp.where(el == m1, lane, big), axis=-1, keepdims=True)
    el2 = jnp.where(lane == i1, -jnp.inf, el)
    m2 = jnp.max(el2, axis=-1, keepdims=True)
    i2 = jnp.min(jnp.where(el2 == m2, lane, big), axis=-1, keepdims=True)
    t = jnp.exp(m2 - m1)
    w1 = gp / (1.0 + t)
    w2 = w1 * t
    first_low = i1 < i2
    ea = jnp.minimum(i1, i2) - lo
    eb = jnp.maximum(i1, i2) - lo
    pair = ea * (2.0 * epg - 1.0 - ea) * 0.5 + (eb - ea - 1.0)
    bucket = gidx * float(PAIRS_PER_GROUP) + pair
    return bucket, jnp.where(first_low, w1, w2), jnp.where(first_low, w2, w1)


def _merge_kernel(sample, attn_ref, cin_ref, prev_ref, prev2_ref, gb_ref, x_ref, mod_ref, cw_ref,
                  ga_ref, gc_ref, bd_ref, wout_ref, g2_ref, wrh_ref, wrl_ref, br_ref, tri_ref, cnt_ref,
                  x1_ref, hpk_ref, info_ref, cnto_ref, cnt_sc):
    d, a = D_MODEL, ATTN_WIDTH
    cin = cin_ref[0]
    tm = cin.shape[0]
    first_step = (pl.program_id(0) == 0) & (pl.program_id(1) == 0)

    @pl.when(first_step)
    def _():
        cnt_sc[...] = cnt_ref[...]

    w0, w1, w2 = cw_ref[0:1, :], cw_ref[1:2, :], cw_ref[2:3, :]
    if sample:
        conv_y = w0 * prev2_ref[0] + w1 * prev_ref[0] + w2 * cin
    else:
        first = pl.program_id(1) == 0
        tail = jnp.where(first, 0.0, prev_ref[0])
        ext = jnp.concatenate([tail, cin], axis=0)
        conv_y = w0 * ext[6:6 + tm] + w1 * ext[7:7 + tm] + w2 * cin
    bd = bd_ref[...]
    an = _group_rms(attn_ref[0], bd, ga_ref[...]).astype(BF16)
    cn = _group_rms(gb_ref[0] * conv_y, bd, gc_ref[...]).astype(BF16)
    mix = (jnp.dot(an, wout_ref[0:a, :], preferred_element_type=F32)
           + jnp.dot(cn, wout_ref[a:, :], preferred_element_type=F32))
    ga1 = mod_ref[0, :, 2 * d:3 * d]
    sh2 = mod_ref[0, :, 3 * d:4 * d]
    sc2 = mod_ref[0, :, 4 * d:5 * d]
    x1 = x_ref[0] + ga1 * mix
    x1_ref[0] = x1
    h2 = _rms(x1, g2_ref[...]) * (1.0 + sc2) + sh2
    hb = h2.astype(BF16)
    hbf = hb.astype(F32)
    hl = (h2 - hbf).astype(BF16)
    wrh = wrh_ref[...]
    logits = (jnp.dot(hb, wrh, preferred_element_type=F32)
              + jnp.dot(hl, wrh, preferred_element_type=F32)
              + jnp.dot(hb, wrl_ref[...], preferred_element_type=F32)) + br_ref[...]
    bucket, w_lo, w_hi = _route(logits)

    lane = lax.broadcasted_iota(jnp.int32, (tm, LANES), 1)
    lanef = lane.astype(F32)
    onehot = lanef == bucket
    incl = jnp.dot(tri_ref[...], jnp.where(onehot, 1.0, 0.0).astype(BF16), preferred_element_type=F32)
    cnt = cnt_sc[...]
    rank = jnp.sum(jnp.where(onehot, incl - 1.0 + cnt, 0.0), axis=-1, keepdims=True)
    cnt_new = cnt + incl[tm - 1:tm, :]
    cnt_sc[...] = cnt_new
    cnto_ref[...] = cnt_new
    info_ref[0] = jnp.where(lane == 0, bucket, jnp.where(lane == 1, rank, 0.0))

    half = PACKED_WIDTH
    bits = lambda v: lax.bitcast_convert_type(v, I32)
    hpk_ref[0, :, 0:half] = bits(hbf[:, 0:half]) | lax.shift_right_logical(bits(hbf[:, half:]), 16)
    hpk_ref[0, :, half:] = jnp.where(lane == 0, bits(w_lo), jnp.where(lane == 1, bits(w_hi), 0))


def _merge(sample, attn, cin, prev, prev2, gb, x, mod, cw, ga, gc, bd, wout, g2, wrh, wrl, br, tri, cnt, tm):
    g, t, d = x.shape
    r = mod.shape[1]
    c = CONV_WIDTH
    rspec = lambda w: pl.BlockSpec((1, tm, w), lambda b, i: (b, i, 0))
    const = lambda arr: pl.BlockSpec(arr.shape, lambda b, i: (0,) * arr.ndim)
    if sample:
        prev_spec = rspec(c)
    else:
        prev_spec = pl.BlockSpec((1, 8, c), lambda b, i: (b, jnp.maximum(i * (tm // 8) - 1, 0), 0))
    return pl.pallas_call(
        functools.partial(_merge_kernel, sample),
        out_shape=(jax.ShapeDtypeStruct((g, t, d), F32), jax.ShapeDtypeStruct((g, t, ROW_WORDS), I32),
                   jax.ShapeDtypeStruct((g, t, LANES), F32), jax.ShapeDtypeStruct((1, LANES), F32)),
        grid=(g, t // tm),
        in_specs=[rspec(ATTN_WIDTH), rspec(c), prev_spec, prev_spec, rspec(c), rspec(d),
                  pl.BlockSpec((1, r, 6 * d), lambda b, i: (b, 0, 0)),
                  const(cw), const(ga), const(gc), const(bd), const(wout), const(g2), const(wrh),
                  const(wrl), const(br), const(tri), const(cnt)],
        out_specs=(rspec(d), rspec(ROW_WORDS), rspec(LANES), pl.BlockSpec((1, LANES), lambda b, i: (0, 0))),
        scratch_shapes=[pltpu.VMEM((1, LANES), F32)],
        compiler_params=_params("arbitrary", "arbitrary"),
        name="mix_out_router",
    )(attn, cin, prev, prev2, gb, x, mod, cw, ga, gc, bd, wout, g2, wrh, wrl, br, tri, cnt)


def _dispatch_kernel(pos_ref, h_ref, xs_in_ref, xs_ref, sem):
    del xs_in_ref
    tm = h_ref.shape[0]
    base = pl.program_id(0) * tm

    def issue(r, c):
        pltpu.make_async_copy(h_ref.at[pl.ds(r, 1), :], xs_ref.at[pl.ds(pos_ref[base + r], 1), :],
                              sem).start()
        return c

    lax.fori_loop(0, tm, issue, 0, unroll=8)
    pltpu.make_async_copy(h_ref, xs_ref.at[pl.ds(0, tm), :], sem).wait()


def _dispatch(pos, hpk, xs, tm):
    n, w = hpk.shape
    return pl.pallas_call(
        _dispatch_kernel,
        out_shape=jax.ShapeDtypeStruct(xs.shape, xs.dtype),
        grid_spec=pltpu.PrefetchScalarGridSpec(
            num_scalar_prefetch=1,
            grid=(n // tm,),
            in_specs=[pl.BlockSpec((tm, w), lambda i, pos: (i, 0)),
                      pl.BlockSpec(memory_space=pl.ANY)],
            out_specs=pl.BlockSpec(memory_space=pl.ANY),
            scratch_shapes=[pltpu.SemaphoreType.DMA]),
        input_output_aliases={2: 0},
        compiler_params=_params("arbitrary"),
        name="moe_dispatch",
    )(pos, hpk, xs)


def _experts_kernel(lo_ref, hi_ref, nt_ref, xs_ref, wgl_ref, wul_ref, wdl_ref, wgh_ref, wuh_ref, wdh_ref,
                    o_ref):
    del lo_ref, hi_ref
    live = pl.program_id(0) < nt_ref[0]

    @pl.when(jnp.logical_not(live))
    def _():
        o_ref[...] = jnp.zeros_like(o_ref)

    @pl.when(live)
    def _():
        half = PACKED_WIDTH
        f32 = lambda v: lax.bitcast_convert_type(v, F32)
        pk = xs_ref[:, 0:half]
        x = jnp.concatenate([f32(pk & jnp.int32(-65536)).astype(BF16),
                             f32(lax.shift_left(pk, 16)).astype(BF16)], axis=-1)
        w_lo = f32(xs_ref[:, half:half + 1])
        w_hi = f32(xs_ref[:, half + 1:half + 2])

        def hidden(wg_ref, wu_ref, w):
            gt = jnp.dot(x, wg_ref[0], preferred_element_type=F32)
            up = jnp.dot(x, wu_ref[0], preferred_element_type=F32)
            return (gt * jax.nn.sigmoid(gt) * up * w).astype(BF16)

        o_ref[...] = (jnp.dot(hidden(wgl_ref, wul_ref, w_lo), wdl_ref[0], preferred_element_type=F32)
                      + jnp.dot(hidden(wgh_ref, wuh_ref, w_hi), wdh_ref[0], preferred_element_type=F32))


def _experts(tile_lo, tile_hi, n_tiles, xs, wg, wu, wdn):
    rows, w = xs.shape
    d = D_MODEL
    tg = MOE_TILE
    tile = lambda i, lo, hi, nt: (jnp.minimum(i, nt[0] - 1), 0)
    wspec = lambda shp, which: pl.BlockSpec(
        (1,) + shp, (lambda i, lo, hi, nt: (lo[i], 0, 0)) if which == 0 else (lambda i, lo, hi, nt: (hi[i], 0, 0)))
    return pl.pallas_call(
        _experts_kernel,
        out_shape=jax.ShapeDtypeStruct((rows, d), F32),
        grid_spec=pltpu.PrefetchScalarGridSpec(
            num_scalar_prefetch=3,
            grid=(rows // tg,),
            in_specs=[pl.BlockSpec((tg, w), tile),
                      wspec((d, D_EXPERT), 0), wspec((d, D_EXPERT), 0), wspec((D_EXPERT, d), 0),
                      wspec((d, D_EXPERT), 1), wspec((d, D_EXPERT), 1), wspec((D_EXPERT, d), 1)],
            out_specs=pl.BlockSpec((tg, d), lambda i, lo, hi, nt: (i, 0))),
        compiler_params=_params("arbitrary"),
        name="moe_experts",
    )(tile_lo, tile_hi, n_tiles, xs, wg, wu, wdn, wg, wu, wdn)


def _final_kernel(tiles_per_mod, pos_ref, x1_ref, mod_ref, modf_ref, gf_ref, os_ref, y_ref, buf, sem):
    del tiles_per_mod
    d = D_MODEL
    tm = x1_ref.shape[0]
    i = pl.program_id(0)
    n = pl.num_programs(0)

    def issue(tile, slot):
        base = tile * tm

        def body(r, c):
            pltpu.make_async_copy(os_ref.at[pl.ds(pos_ref[base + r], 1), :],
                                  buf.at[slot, pl.ds(r, 1), :], sem.at[slot]).start()
            return c

        lax.fori_loop(0, tm, body, 0, unroll=8)

    @pl.when(i == 0)
    def _():
        issue(0, 0)

    @pl.when(i + 1 < n)
    def _():
        issue(i + 1, (i + 1) % 2)

    slot = i % 2
    pltpu.make_async_copy(os_ref.at[pl.ds(0, tm), :], buf.at[slot], sem.at[slot]).wait()
    ga2 = mod_ref[0, :, 5 * d:6 * d]
    x2 = x1_ref[...] + ga2 * buf[slot]
    shf = modf_ref[0, :, 0:d]
    scf = modf_ref[0, :, d:2 * d]
    y_ref[...] = _rms(x2, gf_ref[...]) * (1.0 + scf) + shf


def _final(pos, x1, mod, modf, gf, out_sorted, tm, tiles_per_mod):
    n, d = x1.shape
    r = mod.shape[1]
    return pl.pallas_call(
        functools.partial(_final_kernel, tiles_per_mod),
        out_shape=jax.ShapeDtypeStruct((n, d), F32),
        grid_spec=pltpu.PrefetchScalarGridSpec(
            num_scalar_prefetch=1,
            grid=(n // tm,),
            in_specs=[pl.BlockSpec((tm, d), lambda i, pos: (i, 0)),
                      pl.BlockSpec((1, r, 6 * d), lambda i, pos: (i // tiles_per_mod, 0, 0)),
                      pl.BlockSpec((1, r, 2 * d), lambda i, pos: (i // tiles_per_mod, 0, 0)),
                      pl.BlockSpec((1, d), lambda i, pos: (0, 0)),
                      pl.BlockSpec(memory_space=pl.ANY)],
            out_specs=pl.BlockSpec((tm, d), lambda i, pos: (i, 0)),
            scratch_shapes=[pltpu.VMEM((2, tm, d), F32), pltpu.SemaphoreType.DMA((2,))]),
        compiler_params=_params("arbitrary"),
        name="moe_combine_final",
    )(pos, x1, mod, modf, gf, out_sorted)


def kernel(x_prompt, x_sample, c_prompt, c_sample, cache_k, cache_v, cache_logf, state_conv, page_table, w_ada, b_ada, g_norm1, w_in, b_forget, conv_w, g_attn_out, g_conv_out, w_out, g_norm2, w_router_group, b_router_group, w_router_expert, b_router_expert, w_expert_gate, w_expert_up, w_expert_down, w_ada_final, b_ada_final, g_final):
    d, a, c, h = D_MODEL, ATTN_WIDTH, CONV_WIDTH, N_HEADS
    bsz, t, _ = x_prompt.shape
    dbs = x_sample.shape[0]
    n_prompt = bsz * t
    assert w_ada.shape[0] == 1 and x_sample.shape[1] == 1

    c_all = jnp.concatenate([c_prompt, c_sample], axis=0)
    mod = _ada(c_all, w_ada[0], b_ada[0])
    modf = _ada(c_all, w_ada_final, b_ada_final)
    mod_p, mod_s = mod[:bsz].reshape(bsz, 1, 6 * d), mod[bsz:].reshape(1, dbs, 6 * d)
    modf_p, modf_s = modf[:bsz].reshape(bsz, 1, 2 * d), modf[bsz:].reshape(1, dbs, 2 * d)

    wt = w_in[0].T.astype(BF16)
    wq = wt[0:a].T
    wkt = wt[a:2 * a]
    wvt = wt[2 * a:3 * a]
    wv = wvt.T
    wft = jnp.zeros((16, d), BF16).at[0:h].set(wt[3 * a:3 * a + h])
    wc = wt[3 * a + h:].T
    bf = b_forget[0].reshape(h, 1)
    g1 = g_norm1[0].reshape(1, d)
    g2 = g_norm2[0].reshape(1, d)
    gf = g_final.reshape(1, d)
    ga = g_attn_out[0].reshape(1, a)
    gc = g_conv_out[0].reshape(1, c)
    cw = conv_w[0]
    wout = w_out[0].astype(BF16)
    lane_group = jnp.arange(a) // HEAD_DIM
    bd = (lane_group[:, None] == lane_group[None, :]).astype(BF16)
    wr = jnp.zeros((d, LANES), F32)
    wr = wr.at[:, 0:N_EXPERTS].set(w_router_expert[0])
    wr = wr.at[:, ROUTER_GROUP_LANE0:ROUTER_GROUP_LANE0 + N_GROUPS].set(w_router_group[0])
    wrh = wr.astype(BF16)
    wrl = (wr - wrh.astype(F32)).astype(BF16)
    br = jnp.zeros((1, LANES), F32)
    br = br.at[0, 0:N_EXPERTS].set(b_router_expert[0])
    br = br.at[0, ROUTER_GROUP_LANE0:ROUTER_GROUP_LANE0 + N_GROUPS].set(b_router_group[0])
    wg = w_expert_gate[0].astype(BF16)
    wu = w_expert_up[0].astype(BF16)
    wdn = w_expert_down[0].astype(BF16)
    tri = lambda m: (jnp.arange(m)[:, None] >= jnp.arange(m)[None, :]).astype(BF16)

    q_p, kt_p, vt_p, ktb_p, vb_p, lft_p, cin_p, gb_p = _inproj(
        True, x_prompt, mod_p, g1, wq, wkt, wvt, wv, wft, bf, wc, TOKEN_TILE)
    cum = _cumsum(lft_p)
    cum_rows = cum.reshape(bsz, h // 2, 2, t)
    cum_cols = cum_rows.transpose(0, 1, 3, 2)
    attn_p = _attn_prompt(q_p, ktb_p, vb_p, cum_rows, cum_cols)
    x1_p, hpk_p, info_p, cnt_p = _merge(
        False, attn_p, cin_p, cin_p, cin_p, gb_p, x_prompt, mod_p, cw, ga, gc, bd, wout, g2, wrh, wrl, br,
        tri(TOKEN_TILE), jnp.zeros((1, LANES), F32), TOKEN_TILE)

    xs_ = x_sample.reshape(1, dbs, d)
    q_s, k_s, v_s, lft_s, cin_s, gb_s = _inproj(False, xs_, mod_s, g1, wq, wkt, wvt, wv, wft, bf, wc, dbs)
    head_of_lane = jnp.arange(a) // HEAD_DIM
    qbd = jnp.where(head_of_lane[None, None, :] == jnp.arange(h)[None, :, None],
                    q_s.reshape(dbs, 1, a), jnp.zeros((), BF16))
    n_pool = cache_k.shape[1]
    page_major = lambda z: z[0].transpose(0, 2, 3, 1).reshape(n_pool, a, PAGE_SIZE)
    attn_s = _attn_sample(
        page_table, qbd, k_s.reshape(dbs, 1, a), v_s.reshape(dbs, 1, a),
        lft_s[0].T.reshape(dbs, h, 1),
        page_major(cache_k), page_major(cache_v), cache_logf[0].transpose(0, 2, 1))
    st = state_conv[0]
    x1_s, hpk_s, info_s, cnt_all = _merge(
        True, attn_s.reshape(1, dbs, a), cin_s, st[:, 1][None], st[:, 0][None], gb_s, xs_, mod_s, cw, ga,
        gc, bd, wout, g2, wrh, wrl, br, tri(dbs), cnt_p, dbs)

    n_rows_max = n_prompt + dbs + N_BUCKETS * (MOE_TILE - 1)
    max_tiles = -(-n_rows_max // MOE_TILE)
    counts = cnt_all[0, 0:N_BUCKETS].astype(I32)
    padded = (counts + (MOE_TILE - 1)) // MOE_TILE * MOE_TILE
    ends = jnp.cumsum(padded)
    offsets = ends - padded
    n_tiles = (ends[-1] // MOE_TILE).astype(I32)
    tile_ids = jnp.minimum(jnp.arange(max_tiles, dtype=I32), n_tiles - 1)
    tile_bucket = jnp.sum(tile_ids[:, None] >= (ends // MOE_TILE)[None, :], axis=1)
    lo_tab, hi_tab = _pair_tables()
    tile_lo = jnp.asarray(lo_tab)[tile_bucket]
    tile_hi = jnp.asarray(hi_tab)[tile_bucket]
    position = lambda info: (offsets[info[..., 0].astype(I32)] + info[..., 1].astype(I32)).reshape(-1)
    pos_p, pos_s = position(info_p), position(info_s)

    xs = jnp.zeros((max_tiles * MOE_TILE, ROW_WORDS), I32)
    xs = _dispatch(pos_p, hpk_p.reshape(n_prompt, ROW_WORDS), xs, TOKEN_TILE)
    xs = _dispatch(pos_s, hpk_s.reshape(dbs, ROW_WORDS), xs, dbs)
    out_sorted = _experts(tile_lo, tile_hi, n_tiles.reshape(1), xs, wg, wu, wdn)
    y_p = _final(pos_p, x1_p.reshape(n_prompt, d), mod_p, modf_p, gf, out_sorted, TOKEN_TILE,
                 t // TOKEN_TILE)
    y_s = _final(pos_s, x1_s.reshape(dbs, d), mod_s, modf_s, gf, out_sorted, dbs, 1)

    kv_p = lambda zt: zt.reshape(bsz, h, HEAD_DIM, t).transpose(0, 3, 1, 2)[None]
    kv_s = lambda z: z.reshape(1, dbs, 1, h, HEAD_DIM)
    return (y_p.reshape(bsz, t, d), y_s.reshape(dbs, 1, d),
            kv_p(kt_p), kv_p(vt_p),
            lft_p.transpose(0, 2, 1)[None],
            cin_p[:, t - (CONV_K - 1):, :][None],
            kv_s(k_s), kv_s(v_s),
            lft_s[0].T.reshape(1, dbs, 1, h),
            jnp.stack([st[:, 1], cin_s[0]], axis=1)[None])
```

```python
import functools

import numpy as np
import jax
import jax.numpy as jnp
from jax import lax
from jax.experimental import pallas as pl
from jax.experimental.pallas import tpu as pltpu

F32 = jnp.float32
BF16 = jnp.bfloat16
I32 = jnp.int32

D_MODEL = 1024
N_HEADS = 8
HEAD_DIM = 64
ATTN_WIDTH = N_HEADS * HEAD_DIM
CONV_WIDTH = D_MODEL - ATTN_WIDTH
CONV_K = 3
N_GROUPS = 4
EXPERTS_PER_GROUP = 8
N_EXPERTS = N_GROUPS * EXPERTS_PER_GROUP
D_EXPERT = 256
PAGE_SIZE = 128
EPS = 1e-6

LANES = 128
ROUTER_GROUP_LANE0 = 32
VMEM_LIMIT = 56 * 1024 * 1024
TOKEN_TILE = 512
ATTN_TILE = 512
PAGES_PER_STEP = 16

PAIRS_PER_GROUP = EXPERTS_PER_GROUP * (EXPERTS_PER_GROUP - 1) // 2
N_BUCKETS = N_GROUPS * PAIRS_PER_GROUP
MOE_TILE = 256
ROW_WIDTH = D_MODEL + LANES
LOG2E = 1.4426950408889634


def _pair_tables():
    lo, hi = [], []
    for g in range(N_GROUPS):
        for a in range(EXPERTS_PER_GROUP):
            for b in range(a + 1, EXPERTS_PER_GROUP):
                lo.append(g * EXPERTS_PER_GROUP + a)
                hi.append(g * EXPERTS_PER_GROUP + b)
    return np.asarray(lo, np.int32), np.asarray(hi, np.int32)


def _params(*sem):
    return pltpu.CompilerParams(dimension_semantics=sem, vmem_limit_bytes=VMEM_LIMIT)


def _rms(x, g):
    return x * lax.rsqrt(jnp.mean(x * x, axis=-1, keepdims=True) + EPS) * g


def _split3(a):
    hi = a.astype(BF16)
    r = a - hi.astype(F32)
    mid = r.astype(BF16)
    lo = (r - mid.astype(F32)).astype(BF16)
    return hi, mid, lo


def _dot3(a, b_exact):
    hi, mid, lo = _split3(a)
    d = lambda t: jnp.dot(t, b_exact, preferred_element_type=F32)
    return d(hi) + d(mid) + d(lo)


def _dot_nt(a, b):
    return lax.dot_general(a, b, (((1,), (1,)), ((), ())), preferred_element_type=F32)


def _ada_kernel(c_ref, w_ref, b_ref, o_ref):
    c = c_ref[...]
    s = (c * jax.nn.sigmoid(c)).astype(BF16)
    o_ref[...] = jnp.dot(s, w_ref[...].astype(BF16), preferred_element_type=F32) + b_ref[...]


def _ada(c, w, b):
    m, d = c.shape
    n = w.shape[1]
    tn = 1024
    return pl.pallas_call(
        _ada_kernel,
        out_shape=jax.ShapeDtypeStruct((m, n), F32),
        grid=(n // tn,),
        in_specs=[pl.BlockSpec((m, d), lambda j: (0, 0)),
                  pl.BlockSpec((d, tn), lambda j: (0, j)),
                  pl.BlockSpec((1, tn), lambda j: (0, j))],
        out_specs=pl.BlockSpec((m, tn), lambda j: (0, j)),
        compiler_params=_params("arbitrary"),
        name="adaln_mod",
    )(c, w, b.reshape(1, n))


def _spread_heads(x, extra_ones):
    rows = x.shape[0]
    lane = lax.broadcasted_iota(jnp.int32, (rows, LANES), 1)
    low = lane < HEAD_DIM
    tiles = []
    for pair in range(N_HEADS // 2):
        tile = x[:, pair * LANES:(pair + 1) * LANES]
        for hh, vals in enumerate((tile, pltpu.roll(tile, HEAD_DIM, 1))):
            ones = functools.reduce(jnp.logical_or, [lane == o for o in extra_ones(2 * pair + hh)])
            tiles.append(jnp.where(low, vals, jnp.where(ones, 1.0, 0.0)).astype(BF16))
    return jnp.concatenate(tiles, axis=-1)


def _inproj_kernel(prompt, x_ref, mod_ref, g1_ref, wq_ref, wkt_ref, wvt_ref, wv_ref, wft_ref, bf_ref,
                   wc_ref, *outs):
    c, d = CONV_WIDTH, D_MODEL
    x = x_ref[0]
    sh1 = mod_ref[0, :, 0:d]
    sc1 = mod_ref[0, :, d:2 * d]
    h = _rms(x, g1_ref[...]) * (1.0 + sc1) + sh1
    hb = h.astype(BF16)
    q = jnp.dot(hb, wq_ref[...], preferred_element_type=F32)
    if prompt:
        q_ref, kt32_ref, vt32_ref, ktb_ref, vb_ref, lft_ref, cin_ref, gb_ref = outs
        kt = _dot_nt(wkt_ref[...], hb)
        vt = _dot_nt(wvt_ref[...], hb)
        kt32_ref[0] = kt
        vt32_ref[0] = vt
        ktb_ref[0] = kt.astype(BF16)
        q_ref[0] = _spread_heads(q * (HEAD_DIM ** -0.5 * LOG2E),
                                 lambda hd: [HEAD_DIM + part * N_HEADS + hd for part in range(3)])
        vb_ref[0] = _spread_heads(jnp.dot(hb, wv_ref[...], preferred_element_type=F32),
                                  lambda hd: [HEAD_DIM])
    else:
        q_ref, k32_ref, v32_ref, lft_ref, cin_ref, gb_ref = outs
        k32_ref[0] = _dot_nt(hb, wkt_ref[...])
        v32_ref[0] = jnp.dot(hb, wv_ref[...], preferred_element_type=F32)
        q_ref[0] = (q * (HEAD_DIM ** -0.5)).astype(BF16)
    z = _dot_nt(wft_ref[...], hb)[0:N_HEADS, :] + bf_ref[...]
    lft_ref[0] = jnp.minimum(z, 0.0) - jnp.log1p(jnp.exp(-jnp.abs(z)))
    cv = jnp.dot(hb, wc_ref[...], preferred_element_type=F32)
    cin_ref[0] = cv[:, 2 * c:3 * c] * cv[:, 0:c]
    gb_ref[0] = cv[:, c:2 * c]


def _inproj(prompt, x, mod, g1, wq, wkt, wvt, wv, wft, bf, wc, tm):
    g, t, d = x.shape
    r = mod.shape[1]
    a, c = ATTN_WIDTH, CONV_WIDTH
    row = lambda w, dt: jax.ShapeDtypeStruct((g, t, w), dt)
    col = lambda w, dt: jax.ShapeDtypeStruct((g, w, t), dt)
    rspec = lambda w: pl.BlockSpec((1, tm, w), lambda b, i: (b, i, 0))
    cspec = lambda w: pl.BlockSpec((1, w, tm), lambda b, i: (b, 0, i))
    const = lambda arr: pl.BlockSpec(arr.shape, lambda b, i: (0,) * arr.ndim)
    if prompt:
        out_shape = (row(2 * a, BF16), col(a, F32), col(a, F32), col(a, BF16), row(2 * a, BF16),
                     col(N_HEADS, F32), row(c, F32), row(c, F32))
        out_specs = (rspec(2 * a), cspec(a), cspec(a), cspec(a), rspec(2 * a), cspec(N_HEADS), rspec(c),
                     rspec(c))
    else:
        out_shape = (row(a, BF16), row(a, F32), row(a, F32), col(N_HEADS, F32), row(c, F32), row(c, F32))
        out_specs = (rspec(a), rspec(a), rspec(a), cspec(N_HEADS), rspec(c), rspec(c))
    return pl.pallas_call(
        functools.partial(_inproj_kernel, prompt),
        out_shape=out_shape,
        grid=(g, t // tm),
        in_specs=[rspec(d),
                  pl.BlockSpec((1, r, 6 * d), lambda b, i: (b, 0, 0)),
                  const(g1), const(wq), const(wkt), const(wvt), const(wv), const(wft), const(bf),
                  const(wc)],
        out_specs=out_specs,
        compiler_params=_params("arbitrary", "arbitrary"),
        name="norm_inproj",
    )(x, mod, g1, wq, wkt, wvt, wv, wft, bf, wc)


def _cumsum_kernel(lf_ref, u_ref, o_ref):
    t = lf_ref.shape[2]
    tc = u_ref.shape[0]
    carry = jnp.zeros((N_HEADS, 1), F32)
    pad = jnp.zeros((HEAD_DIM - 3 * N_HEADS, tc), F32)
    for i in range(t // tc):
        blk = lf_ref[0, :, i * tc:(i + 1) * tc]
        cs = _dot3(blk, u_ref[...]) + carry
        carry = cs[:, tc - 1:tc]
        parts = [p.astype(F32) for p in _split3(cs * (-LOG2E))]
        o_ref[0, :, i * tc:(i + 1) * tc] = jnp.concatenate(parts + [pad], axis=0).astype(BF16)


def _cumsum(lft):
    g, h, t = lft.shape
    tc = 512
    idx = jnp.arange(tc)
    upper = (idx[:, None] <= idx[None, :]).astype(BF16)
    return pl.pallas_call(
        _cumsum_kernel,
        out_shape=jax.ShapeDtypeStruct((g, HEAD_DIM, t), BF16),
        grid=(g,),
        in_specs=[pl.BlockSpec((1, h, t), lambda b: (b, 0, 0)),
                  pl.BlockSpec((tc, tc), lambda b: (0, 0))],
        out_specs=pl.BlockSpec((1, HEAD_DIM, t), lambda b: (b, 0, 0)),
        compiler_params=_params("arbitrary"),
        name="logf_cumsum",
    )(lft, upper)


def _attn_prompt_kernel(q_ref, kt_ref, kb_ref, v_ref, o_ref):
    t = q_ref.shape[1]
    ta = ATTN_TILE
    n = t // ta
    row = lax.broadcasted_iota(jnp.int32, (ta, ta), 0)
    col = lax.broadcasted_iota(jnp.int32, (ta, ta), 1)
    causal = col <= row
    kb = kb_ref[0]
    kt_aug = [jnp.concatenate([kt_ref[0, hh * HEAD_DIM:(hh + 1) * HEAD_DIM, :], kb], axis=0)
              for hh in range(2)]
    for i in range(n):
        outs = []
        for hh in range(2):
            ls = slice(hh * LANES, (hh + 1) * LANES)
            qa = q_ref[0, i * ta:(i + 1) * ta, ls]
            m = jnp.full((ta, 1), -jnp.inf, F32)
            acc = jnp.zeros((ta, LANES), F32)
            for j in range(i + 1):
                s = jnp.dot(qa, kt_aug[hh][:, j * ta:(j + 1) * ta], preferred_element_type=F32)
                if j == i:
                    s = jnp.where(causal, s, -jnp.inf)
                m_new = jnp.maximum(m, jnp.max(s, axis=-1, keepdims=True))
                p = jnp.exp2(s - m_new).astype(BF16)
                acc = jnp.exp2(m - m_new) * acc + jnp.dot(p, v_ref[0, j * ta:(j + 1) * ta, ls],
                                                          preferred_element_type=F32)
                m = m_new
            outs.append(acc[:, 0:HEAD_DIM] / acc[:, HEAD_DIM:HEAD_DIM + 1])
        o_ref[0, i * ta:(i + 1) * ta, :] = jnp.concatenate(outs, axis=-1)


def _attn_prompt(q_aug, ktb, kbias, v_aug):
    g, t, _ = q_aug.shape
    a = ATTN_WIDTH
    wide = pl.BlockSpec((1, t, 2 * LANES), lambda b, p: (b, 0, p))
    return pl.pallas_call(
        _attn_prompt_kernel,
        out_shape=jax.ShapeDtypeStruct((g, t, a), F32),
        grid=(g, a // LANES),
        in_specs=[wide, pl.BlockSpec((1, LANES, t), lambda b, p: (b, p, 0)),
                  pl.BlockSpec((1, HEAD_DIM, t), lambda b, p: (b, 0, 0)), wide],
        out_specs=pl.BlockSpec((1, t, LANES), lambda b, p: (b, 0, p)),
        compiler_params=_params("arbitrary", "arbitrary"),
        name="fox_prompt_attn",
    )(q_aug, ktb, kbias, v_aug)


def _scores_sample_kernel(pt_ref, qbd_ref, kn_ref, fn_ref, lmat_ref, *rest):
    np_ = PAGES_PER_STEP
    kt_refs, f_refs = rest[0:np_], rest[np_:2 * np_]
    p_ref, pn_ref, l_ref, need_ref, m_sc, snew_sc, carry_sc, pmax_sc = rest[2 * np_:]
    step = pl.program_id(1)
    qbd = qbd_ref[0]

    @pl.when(step == 0)
    def _():
        kn = kn_ref[0].astype(BF16).astype(F32)
        s_new = jnp.sum(qbd.astype(F32) * kn, axis=-1, keepdims=True)
        snew_sc[...] = s_new
        m_sc[...] = s_new
        carry_sc[...] = fn_ref[0]
        pmax_sc[...] = jnp.full(pmax_sc.shape, -jnp.inf, F32)

    f_all = jnp.concatenate([f_refs[i][0] for i in range(np_)], axis=0)
    suffix = _dot3(f_all, lmat_ref[...])
    totals = jnp.sum(f_all, axis=-1, keepdims=True)
    carry = carry_sc[...]
    lane = lax.broadcasted_iota(jnp.int32, pmax_sc.shape, 1)
    pmax = pmax_sc[...]
    for i in range(np_):
        sl = slice(i * N_HEADS, (i + 1) * N_HEADS)
        order = step * np_ + i
        s = jnp.dot(qbd, kt_refs[i][0].astype(BF16), preferred_element_type=F32) + carry + suffix[sl]
        p_ref[0, order] = s
        pmax = jnp.where(lane == order, jnp.max(s, axis=-1, keepdims=True), pmax)
        carry = carry + totals[sl]
    carry_sc[...] = carry
    pmax_sc[...] = pmax
    m_sc[...] = jnp.maximum(m_sc[...], jnp.max(pmax, axis=-1, keepdims=True))

    @pl.when(step == pl.num_programs(1) - 1)
    def _():
        m = m_sc[...]
        p = jnp.exp(p_ref[0] - m[None])
        p_ref[0] = p
        pn = jnp.exp(snew_sc[...] - m)
        pn_ref[0] = pn
        l_ref[0] = jnp.sum(jnp.sum(p, axis=0), axis=-1, keepdims=True) + pn
        top = jnp.exp(pmax_sc[...] - m).astype(BF16).astype(F32)
        need_ref[0] = jnp.max(jnp.where(top != 0.0, 1.0, 0.0), axis=0, keepdims=True)


def _pv_sample_kernel(eff_ref, need_ref, p_ref, pn_ref, l_ref, vn_ref, *rest):
    del eff_ref
    np_ = PAGES_PER_STEP
    vt_refs = rest[0:np_]
    o_ref, acc_sc = rest[np_:]
    b, step = pl.program_id(0), pl.program_id(1)

    @pl.when(step == 0)
    def _():
        acc_sc[...] = pn_ref[0].astype(BF16).astype(F32) * vn_ref[0].astype(BF16).astype(F32)

    base = (b * pl.num_programs(1) + step) * np_
    for i in range(np_):
        @pl.when(need_ref[base + i] != 0)
        def _(i=i):
            acc_sc[...] += _dot_nt(p_ref[0, i].astype(BF16), vt_refs[i][0].astype(BF16))

    @pl.when(step == pl.num_programs(1) - 1)
    def _():
        res = acc_sc[...] / l_ref[0]
        head = lax.broadcasted_iota(jnp.int32, res.shape, 0)
        lane_head = lax.broadcasted_iota(jnp.int32, res.shape, 1) // HEAD_DIM
        o_ref[0] = jnp.sum(jnp.where(head == lane_head, res, 0.0), axis=0, keepdims=True)


def _attn_sample(page_table, qbd, k_new, v_new, f_new, cache_kt, cache_vt, cache_ft):
    bsz, n_pages = page_table.shape
    a, h = ATTN_WIDTH, N_HEADS
    np_ = PAGES_PER_STEP
    n_steps = n_pages // np_
    idx = jnp.arange(PAGE_SIZE)
    lmat = (idx[:, None] > idx[None, :]).astype(BF16)
    newest_first = page_table[:, ::-1].reshape(-1)
    per_b = lambda shp: pl.BlockSpec(shp, lambda b, s, *_: (b,) + (0,) * (len(shp) - 1))

    def page_spec(shape, i):
        return pl.BlockSpec(shape, lambda b, s, pt, *_: (pt[(b * n_steps + s) * np_ + i], 0, 0))

    p, pn, l, need = pl.pallas_call(
        _scores_sample_kernel,
        out_shape=(jax.ShapeDtypeStruct((bsz, n_pages, h, PAGE_SIZE), F32),
                   jax.ShapeDtypeStruct((bsz, h, 1), F32), jax.ShapeDtypeStruct((bsz, h, 1), F32),
                   jax.ShapeDtypeStruct((bsz, 1, n_pages), F32)),
        grid_spec=pltpu.PrefetchScalarGridSpec(
            num_scalar_prefetch=1,
            grid=(bsz, n_steps),
            in_specs=([per_b((1, h, a)), per_b((1, 1, a)), per_b((1, h, 1)),
                       pl.BlockSpec((PAGE_SIZE, PAGE_SIZE), lambda b, s, pt: (0, 0))]
                      + [page_spec((1, a, PAGE_SIZE), i) for i in range(np_)]
                      + [page_spec((1, h, PAGE_SIZE), i) for i in range(np_)]),
            out_specs=(per_b((1, n_pages, h, PAGE_SIZE)), per_b((1, h, 1)), per_b((1, h, 1)),
                       per_b((1, 1, n_pages))),
            scratch_shapes=[pltpu.VMEM((h, 1), F32), pltpu.VMEM((h, 1), F32), pltpu.VMEM((h, 1), F32),
                            pltpu.VMEM((h, n_pages), F32)]),
        compiler_params=_params("arbitrary", "arbitrary"),
        name="fox_sample_scores",
    )(newest_first, qbd, k_new, f_new, lmat, *([cache_kt] * np_), *([cache_ft] * np_))

    need_k = need.reshape(bsz * n_steps, np_) > 0
    k_idx = jnp.arange(bsz * n_steps, dtype=I32)[:, None]
    last_needed = lax.cummax(jnp.where(need_k, k_idx, 0), axis=0)
    eff = jnp.take_along_axis(newest_first.reshape(bsz * n_steps, np_), last_needed, axis=0).reshape(-1)

    return pl.pallas_call(
        _pv_sample_kernel,
        out_shape=jax.ShapeDtypeStruct((bsz, 1, a), F32),
        grid_spec=pltpu.PrefetchScalarGridSpec(
            num_scalar_prefetch=2,
            grid=(bsz, n_steps),
            in_specs=([pl.BlockSpec((1, np_, h, PAGE_SIZE), lambda b, s, *_: (b, s, 0, 0)),
                       per_b((1, h, 1)), per_b((1, h, 1)), per_b((1, 1, a))]
                      + [page_spec((1, a, PAGE_SIZE), i) for i in range(np_)]),
            out_specs=per_b((1, 1, a)),
            scratch_shapes=[pltpu.VMEM((h, a), F32)]),
        compiler_params=_params("arbitrary", "arbitrary"),
        name="fox_sample_pv",
    )(eff, need_k.reshape(-1).astype(I32), p, pn, l, v_new, *([cache_vt] * np_))


def _group_rms(x, bd, g):
    ss = x * x
    hi = ss.astype(BF16)
    lo = (ss - hi.astype(F32)).astype(BF16)
    ms = (jnp.dot(hi, bd, preferred_element_type=F32) + jnp.dot(lo, bd, preferred_element_type=F32))
    return x * lax.rsqrt(ms * (1.0 / HEAD_DIM) + EPS) * g


def _route(logits):
    lane = lax.broadcasted_iota(jnp.int32, logits.shape, 1).astype(F32)
    big = jnp.float32(1e9)
    g0 = float(ROUTER_GROUP_LANE0)
    epg = float(EXPERTS_PER_GROUP)
    glog = jnp.where((lane >= g0) & (lane < g0 + N_GROUPS), logits, -jnp.inf)
    gmax = jnp.max(glog, axis=-1, keepdims=True)
    gidx = jnp.min(jnp.where(glog == gmax, lane, big), axis=-1, keepdims=True) - g0
    gp = 1.0 / jnp.sum(jnp.exp(glog - gmax), axis=-1, keepdims=True)
    lo = gidx * epg
    el = jnp.where((lane >= lo) & (lane < lo + epg), logits, -jnp.inf)
    m1 = jnp.max(el, axis=-1, keepdims=True)
    i1 = jnp.min(jnp.where(el == m1, lane, big), axis=-1, keepdims=True)
    el2 = jnp.where(lane == i1, -jnp.inf, el)
    m2 = jnp.max(el2, axis=-1, keepdims=True)
    i2 = jnp.min(jnp.where(el2 == m2, lane, big), axis=-1, keepdims=True)
    t = jnp.exp(m2 - m1)
    w1 = gp / (1.0 + t)
    w2 = w1 * t
    first_low = i1 < i2
    ea = jnp.minimum(i1, i2) - lo
    eb = jnp.maximum(i1, i2) - lo
    pair = ea * (2.0 * epg - 1.0 - ea) * 0.5 + (eb - ea - 1.0)
    bucket = gidx * float(PAIRS_PER_GROUP) + pair
    return bucket, jnp.where(first_low, w1, w2), jnp.where(first_low, w2, w1)


def _merge_kernel(sample, attn_ref, cin_ref, prev_ref, prev2_ref, gb_ref, x_ref, mod_ref, cw_ref,
                  ga_ref, gc_ref, bd_ref, wout_ref, g2_ref, wrh_ref, wrl_ref, br_ref, tri_ref, cnt_ref,
                  x1_ref, hrow_ref, info_ref, cnto_ref, cnt_sc):
    d, a = D_MODEL, ATTN_WIDTH
    cin = cin_ref[0]
    tm = cin.shape[0]
    first_step = (pl.program_id(0) == 0) & (pl.program_id(1) == 0)

    @pl.when(first_step)
    def _():
        cnt_sc[...] = cnt_ref[...]

    w0, w1, w2 = cw_ref[0:1, :], cw_ref[1:2, :], cw_ref[2:3, :]
    if sample:
        conv_y = w0 * prev2_ref[0] + w1 * prev_ref[0] + w2 * cin
    else:
        first = pl.program_id(1) == 0
        tail = jnp.where(first, 0.0, prev_ref[0])
        ext = jnp.concatenate([tail, cin], axis=0)
        conv_y = w0 * ext[6:6 + tm] + w1 * ext[7:7 + tm] + w2 * cin
    bd = bd_ref[...]
    an = _group_rms(attn_ref[0], bd, ga_ref[...]).astype(BF16)
    cn = _group_rms(gb_ref[0] * conv_y, bd, gc_ref[...]).astype(BF16)
    mix = (jnp.dot(an, wout_ref[0:a, :], preferred_element_type=F32)
           + jnp.dot(cn, wout_ref[a:, :], preferred_element_type=F32))
    ga1 = mod_ref[0, :, 2 * d:3 * d]
    sh2 = mod_ref[0, :, 3 * d:4 * d]
    sc2 = mod_ref[0, :, 4 * d:5 * d]
    x1 = x_ref[0] + ga1 * mix
    x1_ref[0] = x1
    h2 = _rms(x1, g2_ref[...]) * (1.0 + sc2) + sh2
    hb = h2.astype(BF16)
    hl = (h2 - hb.astype(F32)).astype(BF16)
    wrh = wrh_ref[...]
    logits = (jnp.dot(hb, wrh, preferred_element_type=F32)
              + jnp.dot(hl, wrh, preferred_element_type=F32)
              + jnp.dot(hb, wrl_ref[...], preferred_element_type=F32)) + br_ref[...]
    bucket, w_lo, w_hi = _route(logits)

    lane = lax.broadcasted_iota(jnp.int32, (tm, LANES), 1)
    lanef = lane.astype(F32)
    onehot = lanef == bucket
    incl = jnp.dot(tri_ref[...], jnp.where(onehot, 1.0, 0.0).astype(BF16), preferred_element_type=F32)
    cnt = cnt_sc[...]
    rank = jnp.sum(jnp.where(onehot, incl - 1.0 + cnt, 0.0), axis=-1, keepdims=True)
    cnt_new = cnt + incl[tm - 1:tm, :]
    cnt_sc[...] = cnt_new
    cnto_ref[...] = cnt_new
    info_ref[0] = jnp.where(lane == 0, bucket, jnp.where(lane == 1, rank, 0.0))

    hrow_ref[0, :, 0:d] = h2
    hrow_ref[0, :, d:] = jnp.where(lane == 0, w_lo, jnp.where(lane == 1, w_hi, 0.0))


def _merge(sample, attn, cin, prev, prev2, gb, x, mod, cw, ga, gc, bd, wout, g2, wrh, wrl, br, tri, cnt, tm):
    g, t, d = x.shape
    r = mod.shape[1]
    c = CONV_WIDTH
    rspec = lambda w: pl.BlockSpec((1, tm, w), lambda b, i: (b, i, 0))
    const = lambda arr: pl.BlockSpec(arr.shape, lambda b, i: (0,) * arr.ndim)
    if sample:
        prev_spec = rspec(c)
    else:
        prev_spec = pl.BlockSpec((1, 8, c), lambda b, i: (b, jnp.maximum(i * (tm // 8) - 1, 0), 0))
    return pl.pallas_call(
        functools.partial(_merge_kernel, sample),
        out_shape=(jax.ShapeDtypeStruct((g, t, d), F32), jax.ShapeDtypeStruct((g, t, ROW_WIDTH), F32),
                   jax.ShapeDtypeStruct((g, t, LANES), F32), jax.ShapeDtypeStruct((1, LANES), F32)),
        grid=(g, t // tm),
        in_specs=[rspec(ATTN_WIDTH), rspec(c), prev_spec, prev_spec, rspec(c), rspec(d),
                  pl.BlockSpec((1, r, 6 * d), lambda b, i: (b, 0, 0)),
                  const(cw), const(ga), const(gc), const(bd), const(wout), const(g2), const(wrh),
                  const(wrl), const(br), const(tri), const(cnt)],
        out_specs=(rspec(d), rspec(ROW_WIDTH), rspec(LANES), pl.BlockSpec((1, LANES), lambda b, i: (0, 0))),
        scratch_shapes=[pltpu.VMEM((1, LANES), F32)],
        compiler_params=_params("arbitrary", "arbitrary"),
        name="mix_out_router",
    )(attn, cin, prev, prev2, gb, x, mod, cw, ga, gc, bd, wout, g2, wrh, wrl, br, tri, cnt)


def _dispatch_kernel(pos_ref, h_ref, xs_in_ref, xs_ref, sem):
    del xs_in_ref
    tm = h_ref.shape[0]
    base = pl.program_id(0) * tm

    def issue(r, c):
        pltpu.make_async_copy(h_ref.at[pl.ds(r, 1), :], xs_ref.at[pl.ds(pos_ref[base + r], 1), :],
                              sem).start()
        return c

    lax.fori_loop(0, tm, issue, 0, unroll=8)
    pltpu.make_async_copy(h_ref, xs_ref.at[pl.ds(0, tm), :], sem).wait()


def _dispatch(pos, hpk, xs, tm):
    n, w = hpk.shape
    return pl.pallas_call(
        _dispatch_kernel,
        out_shape=jax.ShapeDtypeStruct(xs.shape, xs.dtype),
        grid_spec=pltpu.PrefetchScalarGridSpec(
            num_scalar_prefetch=1,
            grid=(n // tm,),
            in_specs=[pl.BlockSpec((tm, w), lambda i, pos: (i, 0)),
                      pl.BlockSpec(memory_space=pl.ANY)],
            out_specs=pl.BlockSpec(memory_space=pl.ANY),
            scratch_shapes=[pltpu.SemaphoreType.DMA]),
        input_output_aliases={2: 0},
        compiler_params=_params("arbitrary"),
        name="moe_dispatch",
    )(pos, hpk, xs)


def _experts_kernel(lo_ref, hi_ref, nt_ref, xs_ref, wgl_ref, wul_ref, wdl_ref, wgh_ref, wuh_ref, wdh_ref,
                    o_ref):
    del lo_ref, hi_ref
    live = pl.program_id(0) < nt_ref[0]

    @pl.when(jnp.logical_not(live))
    def _():
        o_ref[...] = jnp.zeros_like(o_ref)

    @pl.when(live)
    def _():
        d = D_MODEL
        x = xs_ref[:, 0:d].astype(BF16)
        w_lo = xs_ref[:, d:d + 1]
        w_hi = xs_ref[:, d + 1:d + 2]

        def hidden(wg_ref, wu_ref, w):
            gt = jnp.dot(x, wg_ref[0], preferred_element_type=F32)
            up = jnp.dot(x, wu_ref[0], preferred_element_type=F32)
            return (gt * jax.nn.sigmoid(gt) * up * w).astype(BF16)

        o_ref[...] = (jnp.dot(hidden(wgl_ref, wul_ref, w_lo), wdl_ref[0], preferred_element_type=F32)
                      + jnp.dot(hidden(wgh_ref, wuh_ref, w_hi), wdh_ref[0], preferred_element_type=F32))


def _experts(tile_lo, tile_hi, n_tiles, xs, wg, wu, wdn):
    rows, w = xs.shape
    d = D_MODEL
    tg = MOE_TILE
    tile = lambda i, lo, hi, nt: (jnp.minimum(i, nt[0] - 1), 0)
    wspec = lambda shp, which: pl.BlockSpec(
        (1,) + shp, (lambda i, lo, hi, nt: (lo[i], 0, 0)) if which == 0 else (lambda i, lo, hi, nt: (hi[i], 0, 0)))
    return pl.pallas_call(
        _experts_kernel,
        out_shape=jax.ShapeDtypeStruct((rows, d), F32),
        grid_spec=pltpu.PrefetchScalarGridSpec(
            num_scalar_prefetch=3,
            grid=(rows // tg,),
            in_specs=[pl.BlockSpec((tg, w), tile),
                      wspec((d, D_EXPERT), 0), wspec((d, D_EXPERT), 0), wspec((D_EXPERT, d), 0),
                      wspec((d, D_EXPERT), 1), wspec((d, D_EXPERT), 1), wspec((D_EXPERT, d), 1)],
            out_specs=pl.BlockSpec((tg, d), lambda i, lo, hi, nt: (i, 0))),
        compiler_params=_params("arbitrary"),
        name="moe_experts",
    )(tile_lo, tile_hi, n_tiles, xs, wg, wu, wdn, wg, wu, wdn)


def _final_kernel(tiles_per_mod, pos_ref, x1_ref, mod_ref, modf_ref, gf_ref, os_ref, y_ref, buf, sem):
    del tiles_per_mod
    d = D_MODEL
    tm = x1_ref.shape[0]
    i = pl.program_id(0)
    n = pl.num_programs(0)

    def issue(tile, slot):
        base = tile * tm

        def body(r, c):
            pltpu.make_async_copy(os_ref.at[pl.ds(pos_ref[base + r], 1), :],
                                  buf.at[slot, pl.ds(r, 1), :], sem.at[slot]).start()
            return c

        lax.fori_loop(0, tm, body, 0, unroll=8)

    @pl.when(i == 0)
    def _():
        issue(0, 0)

    @pl.when(i + 1 < n)
    def _():
        issue(i + 1, (i + 1) % 2)

    slot = i % 2
    pltpu.make_async_copy(os_ref.at[pl.ds(0, tm), :], buf.at[slot], sem.at[slot]).wait()
    ga2 = mod_ref[0, :, 5 * d:6 * d]
    x2 = x1_ref[...] + ga2 * buf[slot]
    shf = modf_ref[0, :, 0:d]
    scf = modf_ref[0, :, d:2 * d]
    y_ref[...] = _rms(x2, gf_ref[...]) * (1.0 + scf) + shf


def _final(pos, x1, mod, modf, gf, out_sorted, tm, tiles_per_mod):
    n, d = x1.shape
    r = mod.shape[1]
    return pl.pallas_call(
        functools.partial(_final_kernel, tiles_per_mod),
        out_shape=jax.ShapeDtypeStruct((n, d), F32),
        grid_spec=pltpu.PrefetchScalarGridSpec(
            num_scalar_prefetch=1,
            grid=(n // tm,),
            in_specs=[pl.BlockSpec((tm, d), lambda i, pos: (i, 0)),
                      pl.BlockSpec((1, r, 6 * d), lambda i, pos: (i // tiles_per_mod, 0, 0)),
                      pl.BlockSpec((1, r, 2 * d), lambda i, pos: (i // tiles_per_mod, 0, 0)),
                      pl.BlockSpec((1, d), lambda i, pos: (0, 0)),
                      pl.BlockSpec(memory_space=pl.ANY)],
            out_specs=pl.BlockSpec((tm, d), lambda i, pos: (i, 0)),
            scratch_shapes=[pltpu.VMEM((2, tm, d), F32), pltpu.SemaphoreType.DMA((2,))]),
        compiler_params=_params("arbitrary"),
        name="moe_combine_final",
    )(pos, x1, mod, modf, gf, out_sorted)


def kernel(x_prompt, x_sample, c_prompt, c_sample, cache_k, cache_v, cache_logf, state_conv, page_table, w_ada, b_ada, g_norm1, w_in, b_forget, conv_w, g_attn_out, g_conv_out, w_out, g_norm2, w_router_group, b_router_group, w_router_expert, b_router_expert, w_expert_gate, w_expert_up, w_expert_down, w_ada_final, b_ada_final, g_final):
    d, a, c, h = D_MODEL, ATTN_WIDTH, CONV_WIDTH, N_HEADS
    bsz, t, _ = x_prompt.shape
    dbs = x_sample.shape[0]
    n_prompt = bsz * t
    assert w_ada.shape[0] == 1 and x_sample.shape[1] == 1

    c_all = jnp.concatenate([c_prompt, c_sample], axis=0)
    mod = _ada(c_all, w_ada[0], b_ada[0])
    modf = _ada(c_all, w_ada_final, b_ada_final)
    mod_p, mod_s = mod[:bsz].reshape(bsz, 1, 6 * d), mod[bsz:].reshape(1, dbs, 6 * d)
    modf_p, modf_s = modf[:bsz].reshape(bsz, 1, 2 * d), modf[bsz:].reshape(1, dbs, 2 * d)

    wt = w_in[0].T.astype(BF16)
    wq = wt[0:a].T
    wkt = wt[a:2 * a]
    wvt = wt[2 * a:3 * a]
    wv = wvt.T
    wft = jnp.zeros((16, d), BF16).at[0:h].set(wt[3 * a:3 * a + h])
    wc = wt[3 * a + h:].T
    bf = b_forget[0].reshape(h, 1)
    g1 = g_norm1[0].reshape(1, d)
    g2 = g_norm2[0].reshape(1, d)
    gf = g_final.reshape(1, d)
    ga = g_attn_out[0].reshape(1, a)
    gc = g_conv_out[0].reshape(1, c)
    cw = conv_w[0]
    wout = w_out[0].astype(BF16)
    lane_group = jnp.arange(a) // HEAD_DIM
    bd = (lane_group[:, None] == lane_group[None, :]).astype(BF16)
    wr = jnp.zeros((d, LANES), F32)
    wr = wr.at[:, 0:N_EXPERTS].set(w_router_expert[0])
    wr = wr.at[:, ROUTER_GROUP_LANE0:ROUTER_GROUP_LANE0 + N_GROUPS].set(w_router_group[0])
    wrh = wr.astype(BF16)
    wrl = (wr - wrh.astype(F32)).astype(BF16)
    br = jnp.zeros((1, LANES), F32)
    br = br.at[0, 0:N_EXPERTS].set(b_router_expert[0])
    br = br.at[0, ROUTER_GROUP_LANE0:ROUTER_GROUP_LANE0 + N_GROUPS].set(b_router_group[0])
    wg = w_expert_gate[0].astype(BF16)
    wu = w_expert_up[0].astype(BF16)
    wdn = w_expert_down[0].astype(BF16)
    tri = lambda m: (jnp.arange(m)[:, None] >= jnp.arange(m)[None, :]).astype(BF16)

    q_p, kt_p, vt_p, ktb_p, vb_p, lft_p, cin_p, gb_p = _inproj(
        True, x_prompt, mod_p, g1, wq, wkt, wvt, wv, wft, bf, wc, TOKEN_TILE)
    attn_p = _attn_prompt(q_p, ktb_p, _cumsum(lft_p), vb_p)
    x1_p, hrow_p, info_p, cnt_p = _merge(
        False, attn_p, cin_p, cin_p, cin_p, gb_p, x_prompt, mod_p, cw, ga, gc, bd, wout, g2, wrh, wrl, br,
        tri(TOKEN_TILE), jnp.zeros((1, LANES), F32), TOKEN_TILE)

    xs_ = x_sample.reshape(1, dbs, d)
    q_s, k_s, v_s, lft_s, cin_s, gb_s = _inproj(False, xs_, mod_s, g1, wq, wkt, wvt, wv, wft, bf, wc, dbs)
    head_of_lane = jnp.arange(a) // HEAD_DIM
    qbd = jnp.where(head_of_lane[None, None, :] == jnp.arange(h)[None, :, None],
                    q_s.reshape(dbs, 1, a), jnp.zeros((), BF16))
    n_pool = cache_k.shape[1]
    page_major = lambda z: z[0].transpose(0, 2, 3, 1).reshape(n_pool, a, PAGE_SIZE)
    attn_s = _attn_sample(
        page_table, qbd, k_s.reshape(dbs, 1, a), v_s.reshape(dbs, 1, a),
        lft_s[0].T.reshape(dbs, h, 1),
        page_major(cache_k), page_major(cache_v), cache_logf[0].transpose(0, 2, 1))
    st = state_conv[0]
    x1_s, hrow_s, info_s, cnt_all = _merge(
        True, attn_s.reshape(1, dbs, a), cin_s, st[:, 1][None], st[:, 0][None], gb_s, xs_, mod_s, cw, ga,
        gc, bd, wout, g2, wrh, wrl, br, tri(dbs), cnt_p, dbs)

    n_rows_max = n_prompt + dbs + N_BUCKETS * (MOE_TILE - 1)
    max_tiles = -(-n_rows_max // MOE_TILE)
    counts = cnt_all[0, 0:N_BUCKETS].astype(I32)
    padded = (counts + (MOE_TILE - 1)) // MOE_TILE * MOE_TILE
    ends = jnp.cumsum(padded)
    offsets = ends - padded
    n_tiles = (ends[-1] // MOE_TILE).astype(I32)
    tile_ids = jnp.minimum(jnp.arange(max_tiles, dtype=I32), n_tiles - 1)
    tile_bucket = jnp.sum(tile_ids[:, None] >= (ends // MOE_TILE)[None, :], axis=1)
    lo_tab, hi_tab = _pair_tables()
    tile_lo = jnp.asarray(lo_tab)[tile_bucket]
    tile_hi = jnp.asarray(hi_tab)[tile_bucket]
    bucket_ids = jnp.arange(N_BUCKETS, dtype=F32)

    def position(info):
        bucket, rank = info[..., 0:1], info[..., 1].astype(I32)
        base = jnp.sum(jnp.where(bucket == bucket_ids, offsets, 0), axis=-1)
        return (base + rank).reshape(-1)

    pos_p, pos_s = position(info_p), position(info_s)

    xs = jnp.zeros((max_tiles * MOE_TILE, ROW_WIDTH), F32)
    xs = _dispatch(pos_p, hrow_p.reshape(n_prompt, ROW_WIDTH), xs, TOKEN_TILE)
    xs = _dispatch(pos_s, hrow_s.reshape(dbs, ROW_WIDTH), xs, dbs)
    out_sorted = _experts(tile_lo, tile_hi, n_tiles.reshape(1), xs, wg, wu, wdn)
    y_p = _final(pos_p, x1_p.reshape(n_prompt, d), mod_p, modf_p, gf, out_sorted, TOKEN_TILE,
                 t // TOKEN_TILE)
    y_s = _final(pos_s, x1_s.reshape(dbs, d), mod_s, modf_s, gf, out_sorted, dbs, 1)

    kv_p = lambda zt: zt.reshape(bsz, h, HEAD_DIM, t).transpose(0, 3, 1, 2)[None]
    kv_s = lambda z: z.reshape(1, dbs, 1, h, HEAD_DIM)
    return (y_p.reshape(bsz, t, d), y_s.reshape(dbs, 1, d),
            kv_p(kt_p), kv_p(vt_p),
            lft_p.transpose(0, 2, 1)[None],
            cin_p[:, t - (CONV_K - 1):, :][None],
            kv_s(k_s), kv_s(v_s),
            lft_s[0].T.reshape(1, dbs, 1, h),
            jnp.stack([st[:, 1], cin_s[0]], axis=1)[None])
```

```python
import functools

import numpy as np
import jax
import jax.numpy as jnp
from jax import lax
from jax.experimental import pallas as pl
from jax.experimental.pallas import tpu as pltpu

F32 = jnp.float32
BF16 = jnp.bfloat16
I32 = jnp.int32

D_MODEL = 1024
N_HEADS = 8
HEAD_DIM = 64
ATTN_WIDTH = N_HEADS * HEAD_DIM
CONV_WIDTH = D_MODEL - ATTN_WIDTH
CONV_K = 3
N_GROUPS = 4
EXPERTS_PER_GROUP = 8
N_EXPERTS = N_GROUPS * EXPERTS_PER_GROUP
D_EXPERT = 256
PAGE_SIZE = 128
EPS = 1e-6

LANES = 128
ROUTER_GROUP_LANE0 = 32
VMEM_LIMIT = 56 * 1024 * 1024
TOKEN_TILE = 512
ATTN_TILE = 512
PAGES_PER_STEP = 16
PV_SLOTS = 4

PAIRS_PER_GROUP = EXPERTS_PER_GROUP * (EXPERTS_PER_GROUP - 1) // 2
N_BUCKETS = N_GROUPS * PAIRS_PER_GROUP
MOE_TILE = 256
ROW_WIDTH = D_MODEL + LANES
LOG2E = 1.4426950408889634


def _pair_tables():
    lo, hi = [], []
    for g in range(N_GROUPS):
        for a in range(EXPERTS_PER_GROUP):
            for b in range(a + 1, EXPERTS_PER_GROUP):
                lo.append(g * EXPERTS_PER_GROUP + a)
                hi.append(g * EXPERTS_PER_GROUP + b)
    return np.asarray(lo, np.int32), np.asarray(hi, np.int32)


def _params(*sem):
    return pltpu.CompilerParams(dimension_semantics=sem, vmem_limit_bytes=VMEM_LIMIT)


def _rms(x, g):
    return x * lax.rsqrt(jnp.mean(x * x, axis=-1, keepdims=True) + EPS) * g


def _split3(a):
    hi = a.astype(BF16)
    r = a - hi.astype(F32)
    mid = r.astype(BF16)
    lo = (r - mid.astype(F32)).astype(BF16)
    return hi, mid, lo


def _dot3(a, b_exact):
    hi, mid, lo = _split3(a)
    d = lambda t: jnp.dot(t, b_exact, preferred_element_type=F32)
    return d(hi) + d(mid) + d(lo)


def _dot_nt(a, b):
    return lax.dot_general(a, b, (((1,), (1,)), ((), ())), preferred_element_type=F32)


def _ada_kernel(c_ref, w_ref, b_ref, o_ref):
    c = c_ref[...]
    s = (c * jax.nn.sigmoid(c)).astype(BF16)
    o_ref[...] = jnp.dot(s, w_ref[...].astype(BF16), preferred_element_type=F32) + b_ref[...]


def _ada(c, w, b):
    m, d = c.shape
    n = w.shape[1]
    tn = 1024
    return pl.pallas_call(
        _ada_kernel,
        out_shape=jax.ShapeDtypeStruct((m, n), F32),
        grid=(n // tn,),
        in_specs=[pl.BlockSpec((m, d), lambda j: (0, 0)),
                  pl.BlockSpec((d, tn), lambda j: (0, j)),
                  pl.BlockSpec((1, tn), lambda j: (0, j))],
        out_specs=pl.BlockSpec((m, tn), lambda j: (0, j)),
        compiler_params=_params("arbitrary"),
        name="adaln_mod",
    )(c, w, b.reshape(1, n))


def _spread_heads(x, extra_ones):
    rows = x.shape[0]
    lane = lax.broadcasted_iota(jnp.int32, (rows, LANES), 1)
    low = lane < HEAD_DIM
    tiles = []
    for pair in range(N_HEADS // 2):
        tile = x[:, pair * LANES:(pair + 1) * LANES]
        for hh, vals in enumerate((tile, pltpu.roll(tile, HEAD_DIM, 1))):
            ones = functools.reduce(jnp.logical_or, [lane == o for o in extra_ones(2 * pair + hh)])
            tiles.append(jnp.where(low, vals, jnp.where(ones, 1.0, 0.0)).astype(BF16))
    return jnp.concatenate(tiles, axis=-1)


def _inproj_kernel(prompt, x_ref, mod_ref, g1_ref, wq_ref, wkt_ref, wvt_ref, wv_ref, wft_ref, bf_ref,
                   wc_ref, *outs):
    c, d = CONV_WIDTH, D_MODEL
    x = x_ref[0]
    sh1 = mod_ref[0, :, 0:d]
    sc1 = mod_ref[0, :, d:2 * d]
    h = _rms(x, g1_ref[...]) * (1.0 + sc1) + sh1
    hb = h.astype(BF16)
    q = jnp.dot(hb, wq_ref[...], preferred_element_type=F32)
    if prompt:
        q_ref, kt32_ref, vt32_ref, ktb_ref, vb_ref, lft_ref, cin_ref, gb_ref = outs
        kt = _dot_nt(wkt_ref[...], hb)
        vt = _dot_nt(wvt_ref[...], hb)
        kt32_ref[0] = kt
        vt32_ref[0] = vt
        ktb_ref[0] = kt.astype(BF16)
        q_ref[0] = _spread_heads(q * (HEAD_DIM ** -0.5 * LOG2E),
                                 lambda hd: [HEAD_DIM + part * N_HEADS + hd for part in range(3)])
        vb_ref[0] = _spread_heads(jnp.dot(hb, wv_ref[...], preferred_element_type=F32),
                                  lambda hd: [HEAD_DIM])
    else:
        q_ref, k32_ref, v32_ref, lft_ref, cin_ref, gb_ref = outs
        k32_ref[0] = _dot_nt(hb, wkt_ref[...])
        v32_ref[0] = jnp.dot(hb, wv_ref[...], preferred_element_type=F32)
        q_ref[0] = (q * (HEAD_DIM ** -0.5)).astype(BF16)
    z = _dot_nt(wft_ref[...], hb)[0:N_HEADS, :] + bf_ref[...]
    lft_ref[0] = jnp.minimum(z, 0.0) - jnp.log1p(jnp.exp(-jnp.abs(z)))
    cv = jnp.dot(hb, wc_ref[...], preferred_element_type=F32)
    cin_ref[0] = cv[:, 2 * c:3 * c] * cv[:, 0:c]
    gb_ref[0] = cv[:, c:2 * c]


def _inproj(prompt, x, mod, g1, wq, wkt, wvt, wv, wft, bf, wc, tm):
    g, t, d = x.shape
    r = mod.shape[1]
    a, c = ATTN_WIDTH, CONV_WIDTH
    row = lambda w, dt: jax.ShapeDtypeStruct((g, t, w), dt)
    col = lambda w, dt: jax.ShapeDtypeStruct((g, w, t), dt)
    rspec = lambda w: pl.BlockSpec((1, tm, w), lambda b, i: (b, i, 0))
    cspec = lambda w: pl.BlockSpec((1, w, tm), lambda b, i: (b, 0, i))
    const = lambda arr: pl.BlockSpec(arr.shape, lambda b, i: (0,) * arr.ndim)
    if prompt:
        out_shape = (row(2 * a, BF16), col(a, F32), col(a, F32), col(a, BF16), row(2 * a, BF16),
                     col(N_HEADS, F32), row(c, F32), row(c, F32))
        out_specs = (rspec(2 * a), cspec(a), cspec(a), cspec(a), rspec(2 * a), cspec(N_HEADS), rspec(c),
                     rspec(c))
    else:
        out_shape = (row(a, BF16), row(a, F32), row(a, F32), col(N_HEADS, F32), row(c, F32), row(c, F32))
        out_specs = (rspec(a), rspec(a), rspec(a), cspec(N_HEADS), rspec(c), rspec(c))
    return pl.pallas_call(
        functools.partial(_inproj_kernel, prompt),
        out_shape=out_shape,
        grid=(g, t // tm),
        in_specs=[rspec(d),
                  pl.BlockSpec((1, r, 6 * d), lambda b, i: (b, 0, 0)),
                  const(g1), const(wq), const(wkt), const(wvt), const(wv), const(wft), const(bf),
                  const(wc)],
        out_specs=out_specs,
        compiler_params=_params("arbitrary", "arbitrary"),
        name="norm_inproj",
    )(x, mod, g1, wq, wkt, wvt, wv, wft, bf, wc)


def _cumsum_kernel(lf_ref, u_ref, o_ref):
    t = lf_ref.shape[2]
    tc = u_ref.shape[0]
    carry = jnp.zeros((N_HEADS, 1), F32)
    pad = jnp.zeros((HEAD_DIM - 3 * N_HEADS, tc), F32)
    for i in range(t // tc):
        blk = lf_ref[0, :, i * tc:(i + 1) * tc]
        cs = _dot3(blk, u_ref[...]) + carry
        carry = cs[:, tc - 1:tc]
        parts = [p.astype(F32) for p in _split3(cs * (-LOG2E))]
        o_ref[0, :, i * tc:(i + 1) * tc] = jnp.concatenate(parts + [pad], axis=0).astype(BF16)


def _cumsum(lft):
    g, h, t = lft.shape
    tc = 512
    idx = jnp.arange(tc)
    upper = (idx[:, None] <= idx[None, :]).astype(BF16)
    return pl.pallas_call(
        _cumsum_kernel,
        out_shape=jax.ShapeDtypeStruct((g, HEAD_DIM, t), BF16),
        grid=(g,),
        in_specs=[pl.BlockSpec((1, h, t), lambda b: (b, 0, 0)),
                  pl.BlockSpec((tc, tc), lambda b: (0, 0))],
        out_specs=pl.BlockSpec((1, HEAD_DIM, t), lambda b: (b, 0, 0)),
        compiler_params=_params("arbitrary"),
        name="logf_cumsum",
    )(lft, upper)


def _attn_prompt_kernel(q_ref, kt_ref, kb_ref, v_ref, o_ref):
    t = q_ref.shape[1]
    ta = ATTN_TILE
    n = t // ta
    row = lax.broadcasted_iota(jnp.int32, (ta, ta), 0)
    col = lax.broadcasted_iota(jnp.int32, (ta, ta), 1)
    causal = col <= row
    kb = kb_ref[0]
    kt_aug = [jnp.concatenate([kt_ref[0, hh * HEAD_DIM:(hh + 1) * HEAD_DIM, :], kb], axis=0)
              for hh in range(2)]
    for i in range(n):
        outs = []
        for hh in range(2):
            ls = slice(hh * LANES, (hh + 1) * LANES)
            qa = q_ref[0, i * ta:(i + 1) * ta, ls]
            m = jnp.full((ta, 1), -jnp.inf, F32)
            acc = jnp.zeros((ta, LANES), F32)
            for j in range(i + 1):
                s = jnp.dot(qa, kt_aug[hh][:, j * ta:(j + 1) * ta], preferred_element_type=F32)
                if j == i:
                    s = jnp.where(causal, s, -jnp.inf)
                m_new = jnp.maximum(m, jnp.max(s, axis=-1, keepdims=True))
                p = jnp.exp2(s - m_new).astype(BF16)
                acc = jnp.exp2(m - m_new) * acc + jnp.dot(p, v_ref[0, j * ta:(j + 1) * ta, ls],
                                                          preferred_element_type=F32)
                m = m_new
            outs.append(acc[:, 0:HEAD_DIM] / acc[:, HEAD_DIM:HEAD_DIM + 1])
        o_ref[0, i * ta:(i + 1) * ta, :] = jnp.concatenate(outs, axis=-1)


def _attn_prompt(q_aug, ktb, kbias, v_aug):
    g, t, _ = q_aug.shape
    a = ATTN_WIDTH
    wide = pl.BlockSpec((1, t, 2 * LANES), lambda b, p: (b, 0, p))
    return pl.pallas_call(
        _attn_prompt_kernel,
        out_shape=jax.ShapeDtypeStruct((g, t, a), F32),
        grid=(g, a // LANES),
        in_specs=[wide, pl.BlockSpec((1, LANES, t), lambda b, p: (b, p, 0)),
                  pl.BlockSpec((1, HEAD_DIM, t), lambda b, p: (b, 0, 0)), wide],
        out_specs=pl.BlockSpec((1, t, LANES), lambda b, p: (b, 0, p)),
        compiler_params=_params("arbitrary", "arbitrary"),
        name="fox_prompt_attn",
    )(q_aug, ktb, kbias, v_aug)


def _scores_sample_kernel(pt_ref, qbd_ref, kn_ref, fn_ref, lmat_ref, *rest):
    np_ = PAGES_PER_STEP
    kt_refs, f_refs = rest[0:np_], rest[np_:2 * np_]
    p_ref, pn_ref, l_ref, need_ref, m_sc, snew_sc, carry_sc, pmax_sc = rest[2 * np_:]
    step = pl.program_id(1)
    qbd = qbd_ref[0]

    @pl.when(step == 0)
    def _():
        kn = kn_ref[0].astype(BF16).astype(F32)
        s_new = jnp.sum(qbd.astype(F32) * kn, axis=-1, keepdims=True)
        snew_sc[...] = s_new
        m_sc[...] = s_new
        carry_sc[...] = fn_ref[0]
        pmax_sc[...] = jnp.full(pmax_sc.shape, -jnp.inf, F32)

    f_all = jnp.concatenate([f_refs[i][0] for i in range(np_)], axis=0)
    suffix = _dot3(f_all, lmat_ref[...])
    totals = jnp.sum(f_all, axis=-1, keepdims=True)
    carry = carry_sc[...]
    lane = lax.broadcasted_iota(jnp.int32, pmax_sc.shape, 1)
    pmax = pmax_sc[...]
    for i in range(np_):
        sl = slice(i * N_HEADS, (i + 1) * N_HEADS)
        order = step * np_ + i
        s = jnp.dot(qbd, kt_refs[i][0].astype(BF16), preferred_element_type=F32) + carry + suffix[sl]
        p_ref[0, order] = s
        pmax = jnp.where(lane == order, jnp.max(s, axis=-1, keepdims=True), pmax)
        carry = carry + totals[sl]
    carry_sc[...] = carry
    pmax_sc[...] = pmax
    m_sc[...] = jnp.maximum(m_sc[...], jnp.max(pmax, axis=-1, keepdims=True))

    @pl.when(step == pl.num_programs(1) - 1)
    def _():
        m = m_sc[...]
        p = jnp.exp(p_ref[0] - m[None])
        p_ref[0] = p
        pn = jnp.exp(snew_sc[...] - m)
        pn_ref[0] = pn
        l_ref[0] = jnp.sum(jnp.sum(p, axis=0), axis=-1, keepdims=True) + pn
        top = jnp.exp(pmax_sc[...] - m).astype(BF16).astype(F32)
        need_ref[0] = jnp.max(jnp.where(top != 0.0, 1.0, 0.0), axis=0, keepdims=True)


def _pv_sample_kernel(pages_ref, count_ref, p_ref, pn_ref, l_ref, vn_ref, vt_hbm, o_ref, vbuf, sem):
    b = pl.program_id(0)
    n = count_ref[b]
    base = b * p_ref.shape[1]

    def fetch(order, slot):
        return pltpu.make_async_copy(vt_hbm.at[pages_ref[base + order]], vbuf.at[slot], sem.at[slot])

    for slot in range(PV_SLOTS):
        @pl.when(slot < n)
        def _(slot=slot):
            fetch(slot, slot).start()

    def body(order, acc):
        slot = order % PV_SLOTS
        fetch(order, slot).wait()
        acc = acc + _dot_nt(p_ref[0, order].astype(BF16), vbuf[slot].astype(BF16))

        @pl.when(order + PV_SLOTS < n)
        def _():
            fetch(order + PV_SLOTS, slot).start()

        return acc

    acc = lax.fori_loop(0, n, body,
                        pn_ref[0].astype(BF16).astype(F32) * vn_ref[0].astype(BF16).astype(F32))
    res = acc / l_ref[0]
    head = lax.broadcasted_iota(jnp.int32, res.shape, 0)
    lane_head = lax.broadcasted_iota(jnp.int32, res.shape, 1) // HEAD_DIM
    o_ref[0] = jnp.sum(jnp.where(head == lane_head, res, 0.0), axis=0, keepdims=True)


def _attn_sample(page_table, qbd, k_new, v_new, f_new, cache_kt, cache_vt, cache_ft):
    bsz, n_pages = page_table.shape
    a, h = ATTN_WIDTH, N_HEADS
    np_ = PAGES_PER_STEP
    n_steps = n_pages // np_
    idx = jnp.arange(PAGE_SIZE)
    lmat = (idx[:, None] > idx[None, :]).astype(BF16)
    newest_first = page_table[:, ::-1].reshape(-1)
    per_b = lambda shp: pl.BlockSpec(shp, lambda b, s, *_: (b,) + (0,) * (len(shp) - 1))

    def page_spec(shape, i):
        return pl.BlockSpec(shape, lambda b, s, pt, *_: (pt[(b * n_steps + s) * np_ + i], 0, 0))

    p, pn, l, need = pl.pallas_call(
        _scores_sample_kernel,
        out_shape=(jax.ShapeDtypeStruct((bsz, n_pages, h, PAGE_SIZE), F32),
                   jax.ShapeDtypeStruct((bsz, h, 1), F32), jax.ShapeDtypeStruct((bsz, h, 1), F32),
                   jax.ShapeDtypeStruct((bsz, 1, n_pages), F32)),
        grid_spec=pltpu.PrefetchScalarGridSpec(
            num_scalar_prefetch=1,
            grid=(bsz, n_steps),
            in_specs=([per_b((1, h, a)), per_b((1, 1, a)), per_b((1, h, 1)),
                       pl.BlockSpec((PAGE_SIZE, PAGE_SIZE), lambda b, s, pt: (0, 0))]
                      + [page_spec((1, a, PAGE_SIZE), i) for i in range(np_)]
                      + [page_spec((1, h, PAGE_SIZE), i) for i in range(np_)]),
            out_specs=(per_b((1, n_pages, h, PAGE_SIZE)), per_b((1, h, 1)), per_b((1, h, 1)),
                       per_b((1, 1, n_pages))),
            scratch_shapes=[pltpu.VMEM((h, 1), F32), pltpu.VMEM((h, 1), F32), pltpu.VMEM((h, 1), F32),
                            pltpu.VMEM((h, n_pages), F32)]),
        compiler_params=_params("arbitrary", "arbitrary"),
        name="fox_sample_scores",
    )(newest_first, qbd, k_new, f_new, lmat, *([cache_kt] * np_), *([cache_ft] * np_))

    order = jnp.arange(1, n_pages + 1, dtype=I32)
    count = jnp.max(jnp.where(need[:, 0, :] > 0, order, 0), axis=-1)

    per_seq = lambda shp: pl.BlockSpec(shp, lambda b, *_: (b,) + (0,) * (len(shp) - 1))
    return pl.pallas_call(
        _pv_sample_kernel,
        out_shape=jax.ShapeDtypeStruct((bsz, 1, a), F32),
        grid_spec=pltpu.PrefetchScalarGridSpec(
            num_scalar_prefetch=2,
            grid=(bsz,),
            in_specs=[per_seq((1, n_pages, h, PAGE_SIZE)), per_seq((1, h, 1)), per_seq((1, h, 1)),
                      per_seq((1, 1, a)), pl.BlockSpec(memory_space=pl.ANY)],
            out_specs=per_seq((1, 1, a)),
            scratch_shapes=[pltpu.VMEM((PV_SLOTS, a, PAGE_SIZE), F32),
                            pltpu.SemaphoreType.DMA((PV_SLOTS,))]),
        compiler_params=_params("arbitrary"),
        name="fox_sample_pv",
    )(newest_first, count, p, pn, l, v_new, cache_vt)


def _group_rms(x, bd, g):
    ss = x * x
    hi = ss.astype(BF16)
    lo = (ss - hi.astype(F32)).astype(BF16)
    ms = (jnp.dot(hi, bd, preferred_element_type=F32) + jnp.dot(lo, bd, preferred_element_type=F32))
    return x * lax.rsqrt(ms * (1.0 / HEAD_DIM) + EPS) * g


def _route(logits):
    lane = lax.broadcasted_iota(jnp.int32, logits.shape, 1).astype(F32)
    big = jnp.float32(1e9)
    g0 = float(ROUTER_GROUP_LANE0)
    epg = float(EXPERTS_PER_GROUP)
    glog = jnp.where((lane >= g0) & (lane < g0 + N_GROUPS), logits, -jnp.inf)
    gmax = jnp.max(glog, axis=-1, keepdims=True)
    gidx = jnp.min(jnp.where(glog == gmax, lane, big), axis=-1, keepdims=True) - g0
    gp = 1.0 / jnp.sum(jnp.exp(glog - gmax), axis=-1, keepdims=True)
    lo = gidx * epg
    el = jnp.where((lane >= lo) & (lane < lo + epg), logits, -jnp.inf)
    m1 = jnp.max(el, axis=-1, keepdims=True)
    i1 = jnp.min(jnp.where(el == m1, lane, big), axis=-1, keepdims=True)
    el2 = jnp.where(lane == i1, -jnp.inf, el)
    m2 = jnp.max(el2, axis=-1, keepdims=True)
    i2 = jnp.min(jnp.where(el2 == m2, lane, big), axis=-1, keepdims=True)
    t = jnp.exp(m2 - m1)
    w1 = gp / (1.0 + t)
    w2 = w1 * t
    first_low = i1 < i2
    ea = jnp.minimum(i1, i2) - lo
    eb = jnp.maximum(i1, i2) - lo
    pair = ea * (2.0 * epg - 1.0 - ea) * 0.5 + (eb - ea - 1.0)
    bucket = gidx * float(PAIRS_PER_GROUP) + pair
    return bucket, jnp.where(first_low, w1, w2), jnp.where(first_low, w2, w1)


def _merge_kernel(sample, attn_ref, cin_ref, prev_ref, prev2_ref, gb_ref, x_ref, mod_ref, cw_ref,
                  ga_ref, gc_ref, bd_ref, wout_ref, g2_ref, wrh_ref, wrl_ref, br_ref, tri_ref, cnt_ref,
                  x1_ref, hrow_ref, info_ref, cnto_ref, cnt_sc):
    d, a = D_MODEL, ATTN_WIDTH
    cin = cin_ref[0]
    tm = cin.shape[0]
    first_step = (pl.program_id(0) == 0) & (pl.program_id(1) == 0)

    @pl.when(first_step)
    def _():
        cnt_sc[...] = cnt_ref[...]

    w0, w1, w2 = cw_ref[0:1, :], cw_ref[1:2, :], cw_ref[2:3, :]
    if sample:
        conv_y = w0 * prev2_ref[0] + w1 * prev_ref[0] + w2 * cin
    else:
        first = pl.program_id(1) == 0
        tail = jnp.where(first, 0.0, prev_ref[0])
        ext = jnp.concatenate([tail, cin], axis=0)
        conv_y = w0 * ext[6:6 + tm] + w1 * ext[7:7 + tm] + w2 * cin
    bd = bd_ref[...]
    an = _group_rms(attn_ref[0], bd, ga_ref[...]).astype(BF16)
    cn = _group_rms(gb_ref[0] * conv_y, bd, gc_ref[...]).astype(BF16)
    mix = (jnp.dot(an, wout_ref[0:a, :], preferred_element_type=F32)
           + jnp.dot(cn, wout_ref[a:, :], preferred_element_type=F32))
    ga1 = mod_ref[0, :, 2 * d:3 * d]
    sh2 = mod_ref[0, :, 3 * d:4 * d]
    sc2 = mod_ref[0, :, 4 * d:5 * d]
    x1 = x_ref[0] + ga1 * mix
    x1_ref[0] = x1
    h2 = _rms(x1, g2_ref[...]) * (1.0 + sc2) + sh2
    hb = h2.astype(BF16)
    hl = (h2 - hb.astype(F32)).astype(BF16)
    wrh = wrh_ref[...]
    logits = (jnp.dot(hb, wrh, preferred_element_type=F32)
              + jnp.dot(hl, wrh, preferred_element_type=F32)
              + jnp.dot(hb, wrl_ref[...], preferred_element_type=F32)) + br_ref[...]
    bucket, w_lo, w_hi = _route(logits)

    lane = lax.broadcasted_iota(jnp.int32, (tm, LANES), 1)
    lanef = lane.astype(F32)
    onehot = lanef == bucket
    incl = jnp.dot(tri_ref[...], jnp.where(onehot, 1.0, 0.0).astype(BF16), preferred_element_type=F32)
    cnt = cnt_sc[...]
    rank = jnp.sum(jnp.where(onehot, incl - 1.0 + cnt, 0.0), axis=-1, keepdims=True)
    cnt_new = cnt + incl[tm - 1:tm, :]
    cnt_sc[...] = cnt_new
    cnto_ref[...] = cnt_new
    info_ref[0] = jnp.where(lane == 0, bucket, jnp.where(lane == 1, rank, 0.0))

    hrow_ref[0, :, 0:d] = h2
    hrow_ref[0, :, d:] = jnp.where(lane == 0, w_lo, jnp.where(lane == 1, w_hi, 0.0))


def _merge(sample, attn, cin, prev, prev2, gb, x, mod, cw, ga, gc, bd, wout, g2, wrh, wrl, br, tri, cnt, tm):
    g, t, d = x.shape
    r = mod.shape[1]
    c = CONV_WIDTH
    rspec = lambda w: pl.BlockSpec((1, tm, w), lambda b, i: (b, i, 0))
    const = lambda arr: pl.BlockSpec(arr.shape, lambda b, i: (0,) * arr.ndim)
    if sample:
        prev_spec = rspec(c)
    else:
        prev_spec = pl.BlockSpec((1, 8, c), lambda b, i: (b, jnp.maximum(i * (tm // 8) - 1, 0), 0))
    return pl.pallas_call(
        functools.partial(_merge_kernel, sample),
        out_shape=(jax.ShapeDtypeStruct((g, t, d), F32), jax.ShapeDtypeStruct((g, t, ROW_WIDTH), F32),
                   jax.ShapeDtypeStruct((g, t, LANES), F32), jax.ShapeDtypeStruct((1, LANES), F32)),
        grid=(g, t // tm),
        in_specs=[rspec(ATTN_WIDTH), rspec(c), prev_spec, prev_spec, rspec(c), rspec(d),
                  pl.BlockSpec((1, r, 6 * d), lambda b, i: (b, 0, 0)),
                  const(cw), const(ga), const(gc), const(bd), const(wout), const(g2), const(wrh),
                  const(wrl), const(br), const(tri), const(cnt)],
        out_specs=(rspec(d), rspec(ROW_WIDTH), rspec(LANES), pl.BlockSpec((1, LANES), lambda b, i: (0, 0))),
        scratch_shapes=[pltpu.VMEM((1, LANES), F32)],
        compiler_params=_params("arbitrary", "arbitrary"),
        name="mix_out_router",
    )(attn, cin, prev, prev2, gb, x, mod, cw, ga, gc, bd, wout, g2, wrh, wrl, br, tri, cnt)


def _dispatch_kernel(n_steps, pos_ref, h_ref, xs_in_ref, xs_ref, buf, sem):
    del xs_in_ref
    tm = h_ref.shape[0]
    i = pl.program_id(0)
    slot = i % 2

    def drain(s):
        pltpu.make_async_copy(buf.at[s], xs_ref.at[pl.ds(0, tm), :], sem.at[s]).wait()

    @pl.when(i >= 2)
    def _():
        drain(slot)

    buf[slot] = h_ref[...]
    base = i * tm
    for r in range(tm):
        pltpu.make_async_copy(buf.at[slot, pl.ds(r, 1), :], xs_ref.at[pl.ds(pos_ref[base + r], 1), :],
                              sem.at[slot]).start(priority=r % 2)

    @pl.when(i == n_steps - 1)
    def _():
        drain(slot)
        if n_steps > 1:
            drain(1 - slot)


def _dispatch(pos, hrow, xs, tm):
    n, w = hrow.shape
    n_steps = n // tm
    return pl.pallas_call(
        functools.partial(_dispatch_kernel, n_steps),
        out_shape=jax.ShapeDtypeStruct(xs.shape, xs.dtype),
        grid_spec=pltpu.PrefetchScalarGridSpec(
            num_scalar_prefetch=1,
            grid=(n_steps,),
            in_specs=[pl.BlockSpec((tm, w), lambda i, pos: (i, 0)),
                      pl.BlockSpec(memory_space=pl.ANY)],
            out_specs=pl.BlockSpec(memory_space=pl.ANY),
            scratch_shapes=[pltpu.VMEM((2, tm, w), F32), pltpu.SemaphoreType.DMA((2,))]),
        input_output_aliases={2: 0},
        compiler_params=_params("arbitrary"),
        name="moe_dispatch",
    )(pos, hrow, xs)


def _experts_kernel(lo_ref, hi_ref, nt_ref, xs_ref, wgl_ref, wul_ref, wdl_ref, wgh_ref, wuh_ref, wdh_ref,
                    o_ref):
    del lo_ref, hi_ref
    live = pl.program_id(0) < nt_ref[0]

    @pl.when(jnp.logical_not(live))
    def _():
        o_ref[...] = jnp.zeros_like(o_ref)

    @pl.when(live)
    def _():
        d = D_MODEL
        x = xs_ref[:, 0:d].astype(BF16)
        w_lo = xs_ref[:, d:d + 1]
        w_hi = xs_ref[:, d + 1:d + 2]

        def hidden(wg_ref, wu_ref, w):
            gt = jnp.dot(x, wg_ref[0], preferred_element_type=F32)
            up = jnp.dot(x, wu_ref[0], preferred_element_type=F32)
            return (gt * jax.nn.sigmoid(gt) * up * w).astype(BF16)

        o_ref[...] = (jnp.dot(hidden(wgl_ref, wul_ref, w_lo), wdl_ref[0], preferred_element_type=F32)
                      + jnp.dot(hidden(wgh_ref, wuh_ref, w_hi), wdh_ref[0], preferred_element_type=F32))


def _experts(tile_lo, tile_hi, n_tiles, xs, wg, wu, wdn):
    rows, w = xs.shape
    d = D_MODEL
    tg = MOE_TILE
    tile = lambda i, lo, hi, nt: (jnp.minimum(i, nt[0] - 1), 0)
    wspec = lambda shp, which: pl.BlockSpec(
        (1,) + shp, (lambda i, lo, hi, nt: (lo[i], 0, 0)) if which == 0 else (lambda i, lo, hi, nt: (hi[i], 0, 0)))
    return pl.pallas_call(
        _experts_kernel,
        out_shape=jax.ShapeDtypeStruct((rows, d), F32),
        grid_spec=pltpu.PrefetchScalarGridSpec(
            num_scalar_prefetch=3,
            grid=(rows // tg,),
            in_specs=[pl.BlockSpec((tg, w), tile),
                      wspec((d, D_EXPERT), 0), wspec((d, D_EXPERT), 0), wspec((D_EXPERT, d), 0),
                      wspec((d, D_EXPERT), 1), wspec((d, D_EXPERT), 1), wspec((D_EXPERT, d), 1)],
            out_specs=pl.BlockSpec((tg, d), lambda i, lo, hi, nt: (i, 0))),
        compiler_params=_params("arbitrary"),
        name="moe_experts",
    )(tile_lo, tile_hi, n_tiles, xs, wg, wu, wdn, wg, wu, wdn)


def _final_kernel(n_tiles, pos_ref, x1_ref, mod_ref, modf_ref, gf_ref, os_ref, y_ref, buf, sem):
    d = D_MODEL
    tm = x1_ref.shape[0]
    i = pl.program_id(0)

    @pl.when(i < n_tiles)
    def _():
        slot = i % 2
        base = i * tm
        for r in range(tm):
            pltpu.make_async_copy(os_ref.at[pl.ds(pos_ref[base + r], 1), :],
                                  buf.at[slot, pl.ds(r, 1), :], sem.at[slot]).start(priority=r % 2)

    @pl.when(i > 0)
    def _():
        slot = (i - 1) % 2
        pltpu.make_async_copy(os_ref.at[pl.ds(0, tm), :], buf.at[slot], sem.at[slot]).wait()
        ga2 = mod_ref[0, :, 5 * d:6 * d]
        x2 = x1_ref[...] + ga2 * buf[slot]
        shf = modf_ref[0, :, 0:d]
        scf = modf_ref[0, :, d:2 * d]
        y_ref[...] = _rms(x2, gf_ref[...]) * (1.0 + scf) + shf


def _final(pos, x1, mod, modf, gf, out_sorted, tm, tiles_per_mod):
    n, d = x1.shape
    r = mod.shape[1]
    n_tiles = n // tm
    done = lambda i: jnp.maximum(i - 1, 0)
    return pl.pallas_call(
        functools.partial(_final_kernel, n_tiles),
        out_shape=jax.ShapeDtypeStruct((n, d), F32),
        grid_spec=pltpu.PrefetchScalarGridSpec(
            num_scalar_prefetch=1,
            grid=(n_tiles + 1,),
            in_specs=[pl.BlockSpec((tm, d), lambda i, pos: (done(i), 0)),
                      pl.BlockSpec((1, r, 6 * d), lambda i, pos: (done(i) // tiles_per_mod, 0, 0)),
                      pl.BlockSpec((1, r, 2 * d), lambda i, pos: (done(i) // tiles_per_mod, 0, 0)),
                      pl.BlockSpec((1, d), lambda i, pos: (0, 0)),
                      pl.BlockSpec(memory_space=pl.ANY)],
            out_specs=pl.BlockSpec((tm, d), lambda i, pos: (done(i), 0)),
            scratch_shapes=[pltpu.VMEM((2, tm, d), F32), pltpu.SemaphoreType.DMA((2,))]),
        compiler_params=_params("arbitrary"),
        name="moe_combine_final",
    )(pos, x1, mod, modf, gf, out_sorted)


def kernel(x_prompt, x_sample, c_prompt, c_sample, cache_k, cache_v, cache_logf, state_conv, page_table, w_ada, b_ada, g_norm1, w_in, b_forget, conv_w, g_attn_out, g_conv_out, w_out, g_norm2, w_router_group, b_router_group, w_router_expert, b_router_expert, w_expert_gate, w_expert_up, w_expert_down, w_ada_final, b_ada_final, g_final):
    d, a, c, h = D_MODEL, ATTN_WIDTH, CONV_WIDTH, N_HEADS
    bsz, t, _ = x_prompt.shape
    dbs = x_sample.shape[0]
    n_prompt = bsz * t
    assert w_ada.shape[0] == 1 and x_sample.shape[1] == 1

    c_all = jnp.concatenate([c_prompt, c_sample], axis=0)
    mod = _ada(c_all, w_ada[0], b_ada[0])
    modf = _ada(c_all, w_ada_final, b_ada_final)
    mod_p, mod_s = mod[:bsz].reshape(bsz, 1, 6 * d), mod[bsz:].reshape(1, dbs, 6 * d)
    modf_p, modf_s = modf[:bsz].reshape(bsz, 1, 2 * d), modf[bsz:].reshape(1, dbs, 2 * d)

    wt = w_in[0].T.astype(BF16)
    wq = wt[0:a].T
    wkt = wt[a:2 * a]
    wvt = wt[2 * a:3 * a]
    wv = wvt.T
    wft = jnp.zeros((16, d), BF16).at[0:h].set(wt[3 * a:3 * a + h])
    wc = wt[3 * a + h:].T
    bf = b_forget[0].reshape(h, 1)
    g1 = g_norm1[0].reshape(1, d)
    g2 = g_norm2[0].reshape(1, d)
    gf = g_final.reshape(1, d)
    ga = g_attn_out[0].reshape(1, a)
    gc = g_conv_out[0].reshape(1, c)
    cw = conv_w[0]
    wout = w_out[0].astype(BF16)
    lane_group = jnp.arange(a) // HEAD_DIM
    bd = (lane_group[:, None] == lane_group[None, :]).astype(BF16)
    wr = jnp.zeros((d, LANES), F32)
    wr = wr.at[:, 0:N_EXPERTS].set(w_router_expert[0])
    wr = wr.at[:, ROUTER_GROUP_LANE0:ROUTER_GROUP_LANE0 + N_GROUPS].set(w_router_group[0])
    wrh = wr.astype(BF16)
    wrl = (wr - wrh.astype(F32)).astype(BF16)
    br = jnp.zeros((1, LANES), F32)
    br = br.at[0, 0:N_EXPERTS].set(b_router_expert[0])
    br = br.at[0, ROUTER_GROUP_LANE0:ROUTER_GROUP_LANE0 + N_GROUPS].set(b_router_group[0])
    wg = w_expert_gate[0].astype(BF16)
    wu = w_expert_up[0].astype(BF16)
    wdn = w_expert_down[0].astype(BF16)
    tri = lambda m: (jnp.arange(m)[:, None] >= jnp.arange(m)[None, :]).astype(BF16)

    q_p, kt_p, vt_p, ktb_p, vb_p, lft_p, cin_p, gb_p = _inproj(
        True, x_prompt, mod_p, g1, wq, wkt, wvt, wv, wft, bf, wc, TOKEN_TILE)
    attn_p = _attn_prompt(q_p, ktb_p, _cumsum(lft_p), vb_p)
    x1_p, hrow_p, info_p, cnt_p = _merge(
        False, attn_p, cin_p, cin_p, cin_p, gb_p, x_prompt, mod_p, cw, ga, gc, bd, wout, g2, wrh, wrl, br,
        tri(TOKEN_TILE), jnp.zeros((1, LANES), F32), TOKEN_TILE)

    xs_ = x_sample.reshape(1, dbs, d)
    q_s, k_s, v_s, lft_s, cin_s, gb_s = _inproj(False, xs_, mod_s, g1, wq, wkt, wvt, wv, wft, bf, wc, dbs)
    head_of_lane = jnp.arange(a) // HEAD_DIM
    qbd = jnp.where(head_of_lane[None, None, :] == jnp.arange(h)[None, :, None],
                    q_s.reshape(dbs, 1, a), jnp.zeros((), BF16))
    n_pool = cache_k.shape[1]
    page_major = lambda z: z[0].transpose(0, 2, 3, 1).reshape(n_pool, a, PAGE_SIZE)
    attn_s = _attn_sample(
        page_table, qbd, k_s.reshape(dbs, 1, a), v_s.reshape(dbs, 1, a),
        lft_s[0].T.reshape(dbs, h, 1),
        page_major(cache_k), page_major(cache_v), cache_logf[0].transpose(0, 2, 1))
    st = state_conv[0]
    x1_s, hrow_s, info_s, cnt_all = _merge(
        True, attn_s.reshape(1, dbs, a), cin_s, st[:, 1][None], st[:, 0][None], gb_s, xs_, mod_s, cw, ga,
        gc, bd, wout, g2, wrh, wrl, br, tri(dbs), cnt_p, dbs)

    n_rows_max = n_prompt + dbs + N_BUCKETS * (MOE_TILE - 1)
    max_tiles = -(-n_rows_max // MOE_TILE)
    counts = cnt_all[0, 0:N_BUCKETS].astype(I32)
    padded = (counts + (MOE_TILE - 1)) // MOE_TILE * MOE_TILE
    ends = jnp.cumsum(padded)
    offsets = ends - padded
    n_tiles = (ends[-1] // MOE_TILE).astype(I32)
    tile_ids = jnp.minimum(jnp.arange(max_tiles, dtype=I32), n_tiles - 1)
    tile_bucket = jnp.sum(tile_ids[:, None] >= (ends // MOE_TILE)[None, :], axis=1)
    lo_tab, hi_tab = _pair_tables()
    tile_lo = jnp.asarray(lo_tab)[tile_bucket]
    tile_hi = jnp.asarray(hi_tab)[tile_bucket]
    bucket_ids = jnp.arange(N_BUCKETS, dtype=F32)

    def position(info):
        bucket, rank = info[..., 0:1], info[..., 1].astype(I32)
        base = jnp.sum(jnp.where(bucket == bucket_ids, offsets, 0), axis=-1)
        return (base + rank).reshape(-1)

    pos_p, pos_s = position(info_p), position(info_s)

    xs = jnp.zeros((max_tiles * MOE_TILE, ROW_WIDTH), F32)
    xs = _dispatch(pos_p, hrow_p.reshape(n_prompt, ROW_WIDTH), xs, TOKEN_TILE)
    xs = _dispatch(pos_s, hrow_s.reshape(dbs, ROW_WIDTH), xs, dbs)
    out_sorted = _experts(tile_lo, tile_hi, n_tiles.reshape(1), xs, wg, wu, wdn)
    y_p = _final(pos_p, x1_p.reshape(n_prompt, d), mod_p, modf_p, gf, out_sorted, TOKEN_TILE,
                 t // TOKEN_TILE)
    y_s = _final(pos_s, x1_s.reshape(dbs, d), mod_s, modf_s, gf, out_sorted, dbs, 1)

    kv_p = lambda zt: zt.reshape(bsz, h, HEAD_DIM, t).transpose(0, 3, 1, 2)[None]
    kv_s = lambda z: z.reshape(1, dbs, 1, h, HEAD_DIM)
    return (y_p.reshape(bsz, t, d), y_s.reshape(dbs, 1, d),
            kv_p(kt_p), kv_p(vt_p),
            lft_p.transpose(0, 2, 1)[None],
            cin_p[:, t - (CONV_K - 1):, :][None],
            kv_s(k_s), kv_s(v_s),
            lft_s[0].T.reshape(1, dbs, 1, h),
            jnp.stack([st[:, 1], cin_s[0]], axis=1)[None])
```

```python
import functools

import numpy as np
import jax
import jax.numpy as jnp
from jax import lax
from jax.experimental import pallas as pl
from jax.experimental.pallas import tpu as pltpu

F32 = jnp.float32
BF16 = jnp.bfloat16
I32 = jnp.int32

D_MODEL = 1024
N_HEADS = 8
HEAD_DIM = 64
ATTN_WIDTH = N_HEADS * HEAD_DIM
CONV_WIDTH = D_MODEL - ATTN_WIDTH
CONV_K = 3
N_GROUPS = 4
EXPERTS_PER_GROUP = 8
N_EXPERTS = N_GROUPS * EXPERTS_PER_GROUP
D_EXPERT = 256
PAGE_SIZE = 128
EPS = 1e-6

LANES = 128
ROUTER_GROUP_LANE0 = 32
VMEM_LIMIT = 56 * 1024 * 1024
TOKEN_TILE = 512
ATTN_TILE = 512
PAGES_PER_STEP = 32
PV_SLOTS = 4

PAIRS_PER_GROUP = EXPERTS_PER_GROUP * (EXPERTS_PER_GROUP - 1) // 2
N_BUCKETS = N_GROUPS * PAIRS_PER_GROUP
MOE_TILE = 256
ROW_WIDTH = D_MODEL + LANES
LOG2E = 1.4426950408889634


def _pair_tables():
    lo, hi = [], []
    for g in range(N_GROUPS):
        for a in range(EXPERTS_PER_GROUP):
            for b in range(a + 1, EXPERTS_PER_GROUP):
                lo.append(g * EXPERTS_PER_GROUP + a)
                hi.append(g * EXPERTS_PER_GROUP + b)
    return np.asarray(lo, np.int32), np.asarray(hi, np.int32)


def _params(*sem):
    return pltpu.CompilerParams(dimension_semantics=sem, vmem_limit_bytes=VMEM_LIMIT)


def _rms(x, g):
    return x * lax.rsqrt(jnp.mean(x * x, axis=-1, keepdims=True) + EPS) * g


def _split3(a):
    hi = a.astype(BF16)
    r = a - hi.astype(F32)
    mid = r.astype(BF16)
    lo = (r - mid.astype(F32)).astype(BF16)
    return hi, mid, lo


def _dot3(a, b_exact):
    hi, mid, lo = _split3(a)
    d = lambda t: jnp.dot(t, b_exact, preferred_element_type=F32)
    return d(hi) + d(mid) + d(lo)


def _dot_nt(a, b):
    return lax.dot_general(a, b, (((1,), (1,)), ((), ())), preferred_element_type=F32)


def _ada_kernel(c_ref, w_ref, b_ref, o_ref):
    c = c_ref[...]
    s = (c * jax.nn.sigmoid(c)).astype(BF16)
    o_ref[...] = jnp.dot(s, w_ref[...].astype(BF16), preferred_element_type=F32) + b_ref[...]


def _ada(c, w, b):
    m, d = c.shape
    n = w.shape[1]
    tn = 1024
    return pl.pallas_call(
        _ada_kernel,
        out_shape=jax.ShapeDtypeStruct((m, n), F32),
        grid=(n // tn,),
        in_specs=[pl.BlockSpec((m, d), lambda j: (0, 0)),
                  pl.BlockSpec((d, tn), lambda j: (0, j)),
                  pl.BlockSpec((1, tn), lambda j: (0, j))],
        out_specs=pl.BlockSpec((m, tn), lambda j: (0, j)),
        compiler_params=_params("arbitrary"),
        name="adaln_mod",
    )(c, w, b.reshape(1, n))


def _spread_heads(x, extra_ones):
    rows = x.shape[0]
    lane = lax.broadcasted_iota(jnp.int32, (rows, LANES), 1)
    low = lane < HEAD_DIM
    tiles = []
    for pair in range(N_HEADS // 2):
        tile = x[:, pair * LANES:(pair + 1) * LANES]
        for hh, vals in enumerate((tile, pltpu.roll(tile, HEAD_DIM, 1))):
            ones = functools.reduce(jnp.logical_or, [lane == o for o in extra_ones(2 * pair + hh)])
            tiles.append(jnp.where(low, vals, jnp.where(ones, 1.0, 0.0)).astype(BF16))
    return jnp.concatenate(tiles, axis=-1)


def _inproj_kernel(prompt, x_ref, mod_ref, g1_ref, wq_ref, wkt_ref, wvt_ref, wv_ref, wft_ref, bf_ref,
                   wc_ref, *outs):
    c, d = CONV_WIDTH, D_MODEL
    x = x_ref[0]
    sh1 = mod_ref[0, :, 0:d]
    sc1 = mod_ref[0, :, d:2 * d]
    h = _rms(x, g1_ref[...]) * (1.0 + sc1) + sh1
    hb = h.astype(BF16)
    q = jnp.dot(hb, wq_ref[...], preferred_element_type=F32)
    if prompt:
        q_ref, kt32_ref, vt32_ref, ktb_ref, vb_ref, lft_ref, cin_ref, gb_ref = outs
        kt = _dot_nt(wkt_ref[...], hb)
        vt = _dot_nt(wvt_ref[...], hb)
        kt32_ref[0] = kt
        vt32_ref[0] = vt
        ktb_ref[0] = kt.astype(BF16)
        q_ref[0] = _spread_heads(q * (HEAD_DIM ** -0.5 * LOG2E),
                                 lambda hd: [HEAD_DIM + part * N_HEADS + hd for part in range(3)])
        vb_ref[0] = _spread_heads(jnp.dot(hb, wv_ref[...], preferred_element_type=F32),
                                  lambda hd: [HEAD_DIM])
    else:
        q_ref, k32_ref, v32_ref, lft_ref, cin_ref, gb_ref = outs
        k32_ref[0] = _dot_nt(hb, wkt_ref[...])
        v32_ref[0] = jnp.dot(hb, wv_ref[...], preferred_element_type=F32)
        q_ref[0] = (q * (HEAD_DIM ** -0.5)).astype(BF16)
    z = _dot_nt(wft_ref[...], hb)[0:N_HEADS, :] + bf_ref[...]
    lft_ref[0] = jnp.minimum(z, 0.0) - jnp.log1p(jnp.exp(-jnp.abs(z)))
    cv = jnp.dot(hb, wc_ref[...], preferred_element_type=F32)
    cin_ref[0] = cv[:, 2 * c:3 * c] * cv[:, 0:c]
    gb_ref[0] = cv[:, c:2 * c]


def _inproj(prompt, x, mod, g1, wq, wkt, wvt, wv, wft, bf, wc, tm):
    g, t, d = x.shape
    r = mod.shape[1]
    a, c = ATTN_WIDTH, CONV_WIDTH
    row = lambda w, dt: jax.ShapeDtypeStruct((g, t, w), dt)
    col = lambda w, dt: jax.ShapeDtypeStruct((g, w, t), dt)
    rspec = lambda w: pl.BlockSpec((1, tm, w), lambda b, i: (b, i, 0))
    cspec = lambda w: pl.BlockSpec((1, w, tm), lambda b, i: (b, 0, i))
    const = lambda arr: pl.BlockSpec(arr.shape, lambda b, i: (0,) * arr.ndim)
    if prompt:
        out_shape = (row(2 * a, BF16), col(a, F32), col(a, F32), col(a, BF16), row(2 * a, BF16),
                     col(N_HEADS, F32), row(c, F32), row(c, F32))
        out_specs = (rspec(2 * a), cspec(a), cspec(a), cspec(a), rspec(2 * a), cspec(N_HEADS), rspec(c),
                     rspec(c))
    else:
        out_shape = (row(a, BF16), row(a, F32), row(a, F32), col(N_HEADS, F32), row(c, F32), row(c, F32))
        out_specs = (rspec(a), rspec(a), rspec(a), cspec(N_HEADS), rspec(c), rspec(c))
    return pl.pallas_call(
        functools.partial(_inproj_kernel, prompt),
        out_shape=out_shape,
        grid=(g, t // tm),
        in_specs=[rspec(d),
                  pl.BlockSpec((1, r, 6 * d), lambda b, i: (b, 0, 0)),
                  const(g1), const(wq), const(wkt), const(wvt), const(wv), const(wft), const(bf),
                  const(wc)],
        out_specs=out_specs,
        compiler_params=_params("arbitrary", "arbitrary"),
        name="norm_inproj",
    )(x, mod, g1, wq, wkt, wvt, wv, wft, bf, wc)


def _cumsum_kernel(lf_ref, u_ref, o_ref):
    t = lf_ref.shape[2]
    tc = u_ref.shape[0]
    carry = jnp.zeros((N_HEADS, 1), F32)
    pad = jnp.zeros((HEAD_DIM - 3 * N_HEADS, tc), F32)
    for i in range(t // tc):
        blk = lf_ref[0, :, i * tc:(i + 1) * tc]
        cs = _dot3(blk, u_ref[...]) + carry
        carry = cs[:, tc - 1:tc]
        parts = [p.astype(F32) for p in _split3(cs * (-LOG2E))]
        o_ref[0, :, i * tc:(i + 1) * tc] = jnp.concatenate(parts + [pad], axis=0).astype(BF16)


def _cumsum(lft):
    g, h, t = lft.shape
    tc = 512
    idx = jnp.arange(tc)
    upper = (idx[:, None] <= idx[None, :]).astype(BF16)
    return pl.pallas_call(
        _cumsum_kernel,
        out_shape=jax.ShapeDtypeStruct((g, HEAD_DIM, t), BF16),
        grid=(g,),
        in_specs=[pl.BlockSpec((1, h, t), lambda b: (b, 0, 0)),
                  pl.BlockSpec((tc, tc), lambda b: (0, 0))],
        out_specs=pl.BlockSpec((1, HEAD_DIM, t), lambda b: (b, 0, 0)),
        compiler_params=_params("arbitrary"),
        name="logf_cumsum",
    )(lft, upper)


def _attn_prompt_kernel(q_ref, kt_ref, kb_ref, v_ref, o_ref):
    t = q_ref.shape[1]
    ta = ATTN_TILE
    n = t // ta
    row = lax.broadcasted_iota(jnp.int32, (ta, ta), 0)
    col = lax.broadcasted_iota(jnp.int32, (ta, ta), 1)
    causal = col <= row
    kb = kb_ref[0]
    kt_aug = [jnp.concatenate([kt_ref[0, hh * HEAD_DIM:(hh + 1) * HEAD_DIM, :], kb], axis=0)
              for hh in range(2)]
    for i in range(n):
        outs = []
        for hh in range(2):
            ls = slice(hh * LANES, (hh + 1) * LANES)
            qa = q_ref[0, i * ta:(i + 1) * ta, ls]
            m = jnp.full((ta, 1), -jnp.inf, F32)
            acc = jnp.zeros((ta, LANES), F32)
            for j in range(i + 1):
                s = jnp.dot(qa, kt_aug[hh][:, j * ta:(j + 1) * ta], preferred_element_type=F32)
                if j == i:
                    s = jnp.where(causal, s, -jnp.inf)
                m_new = jnp.maximum(m, jnp.max(s, axis=-1, keepdims=True))
                p = jnp.exp2(s - m_new).astype(BF16)
                acc = jnp.exp2(m - m_new) * acc + jnp.dot(p, v_ref[0, j * ta:(j + 1) * ta, ls],
                                                          preferred_element_type=F32)
                m = m_new
            outs.append(acc[:, 0:HEAD_DIM] / acc[:, HEAD_DIM:HEAD_DIM + 1])
        o_ref[0, i * ta:(i + 1) * ta, :] = jnp.concatenate(outs, axis=-1)


def _attn_prompt(q_aug, ktb, kbias, v_aug):
    g, t, _ = q_aug.shape
    a = ATTN_WIDTH
    wide = pl.BlockSpec((1, t, 2 * LANES), lambda b, p: (b, 0, p))
    return pl.pallas_call(
        _attn_prompt_kernel,
        out_shape=jax.ShapeDtypeStruct((g, t, a), F32),
        grid=(g, a // LANES),
        in_specs=[wide, pl.BlockSpec((1, LANES, t), lambda b, p: (b, p, 0)),
                  pl.BlockSpec((1, HEAD_DIM, t), lambda b, p: (b, 0, 0)), wide],
        out_specs=pl.BlockSpec((1, t, LANES), lambda b, p: (b, 0, p)),
        compiler_params=_params("arbitrary", "arbitrary"),
        name="fox_prompt_attn",
    )(q_aug, ktb, kbias, v_aug)


def _scores_sample_kernel(pt_ref, qbd_ref, kn_ref, fn_ref, lmat_ref, *rest):
    np_ = PAGES_PER_STEP
    kt_refs, f_refs = rest[0:np_], rest[np_:2 * np_]
    p_ref, pn_ref, l_ref, need_ref, m_sc, snew_sc, carry_sc, pmax_sc = rest[2 * np_:]
    step = pl.program_id(1)
    qbd = qbd_ref[0]

    @pl.when(step == 0)
    def _():
        kn = kn_ref[0].astype(BF16).astype(F32)
        s_new = jnp.sum(qbd.astype(F32) * kn, axis=-1, keepdims=True)
        snew_sc[...] = s_new
        m_sc[...] = s_new
        carry_sc[...] = fn_ref[0]
        pmax_sc[...] = jnp.full(pmax_sc.shape, -jnp.inf, F32)

    f_all = jnp.concatenate([f_refs[i][0] for i in range(np_)], axis=0)
    suffix = _dot3(f_all, lmat_ref[...])
    totals = jnp.sum(f_all, axis=-1, keepdims=True)
    carry = carry_sc[...]
    lane = lax.broadcasted_iota(jnp.int32, pmax_sc.shape, 1)
    pmax = pmax_sc[...]
    for i in range(np_):
        sl = slice(i * N_HEADS, (i + 1) * N_HEADS)
        order = step * np_ + i
        s = jnp.dot(qbd, kt_refs[i][0].astype(BF16), preferred_element_type=F32) + carry + suffix[sl]
        p_ref[0, order] = s
        pmax = jnp.where(lane == order, jnp.max(s, axis=-1, keepdims=True), pmax)
        carry = carry + totals[sl]
    carry_sc[...] = carry
    pmax_sc[...] = pmax
    m_sc[...] = jnp.maximum(m_sc[...], jnp.max(pmax, axis=-1, keepdims=True))

    @pl.when(step == pl.num_programs(1) - 1)
    def _():
        m = m_sc[...]
        p = jnp.exp(p_ref[0] - m[None])
        p_ref[0] = p
        pn = jnp.exp(snew_sc[...] - m)
        pn_ref[0] = pn
        l_ref[0] = jnp.sum(jnp.sum(p, axis=0), axis=-1, keepdims=True) + pn
        top = jnp.exp(pmax_sc[...] - m).astype(BF16).astype(F32)
        need_ref[0] = jnp.max(jnp.where(top != 0.0, 1.0, 0.0), axis=0, keepdims=True)


def _pv_sample_kernel(pages_ref, count_ref, p_ref, pn_ref, l_ref, vn_ref, vt_hbm, o_ref, vbuf, sem):
    b = pl.program_id(0)
    n = count_ref[b]
    base = b * p_ref.shape[1]

    def fetch(order, slot):
        return pltpu.make_async_copy(vt_hbm.at[pages_ref[base + order]], vbuf.at[slot], sem.at[slot])

    for slot in range(PV_SLOTS):
        @pl.when(slot < n)
        def _(slot=slot):
            fetch(slot, slot).start()

    def body(order, acc):
        slot = order % PV_SLOTS
        fetch(order, slot).wait()
        acc = acc + _dot_nt(p_ref[0, order].astype(BF16), vbuf[slot].astype(BF16))

        @pl.when(order + PV_SLOTS < n)
        def _():
            fetch(order + PV_SLOTS, slot).start()

        return acc

    acc = lax.fori_loop(0, n, body,
                        pn_ref[0].astype(BF16).astype(F32) * vn_ref[0].astype(BF16).astype(F32))
    res = acc / l_ref[0]
    head = lax.broadcasted_iota(jnp.int32, res.shape, 0)
    lane_head = lax.broadcasted_iota(jnp.int32, res.shape, 1) // HEAD_DIM
    o_ref[0] = jnp.sum(jnp.where(head == lane_head, res, 0.0), axis=0, keepdims=True)


def _attn_sample(page_table, qbd, k_new, v_new, f_new, cache_kt, cache_vt, cache_ft):
    bsz, n_pages = page_table.shape
    a, h = ATTN_WIDTH, N_HEADS
    np_ = PAGES_PER_STEP
    n_steps = n_pages // np_
    idx = jnp.arange(PAGE_SIZE)
    lmat = (idx[:, None] > idx[None, :]).astype(BF16)
    newest_first = page_table[:, ::-1].reshape(-1)
    per_b = lambda shp: pl.BlockSpec(shp, lambda b, s, *_: (b,) + (0,) * (len(shp) - 1))

    def page_spec(shape, i):
        return pl.BlockSpec(shape, lambda b, s, pt, *_: (pt[(b * n_steps + s) * np_ + i], 0, 0))

    p, pn, l, need = pl.pallas_call(
        _scores_sample_kernel,
        out_shape=(jax.ShapeDtypeStruct((bsz, n_pages, h, PAGE_SIZE), F32),
                   jax.ShapeDtypeStruct((bsz, h, 1), F32), jax.ShapeDtypeStruct((bsz, h, 1), F32),
                   jax.ShapeDtypeStruct((bsz, 1, n_pages), F32)),
        grid_spec=pltpu.PrefetchScalarGridSpec(
            num_scalar_prefetch=1,
            grid=(bsz, n_steps),
            in_specs=([per_b((1, h, a)), per_b((1, 1, a)), per_b((1, h, 1)),
                       pl.BlockSpec((PAGE_SIZE, PAGE_SIZE), lambda b, s, pt: (0, 0))]
                      + [page_spec((1, a, PAGE_SIZE), i) for i in range(np_)]
                      + [page_spec((1, h, PAGE_SIZE), i) for i in range(np_)]),
            out_specs=(per_b((1, n_pages, h, PAGE_SIZE)), per_b((1, h, 1)), per_b((1, h, 1)),
                       per_b((1, 1, n_pages))),
            scratch_shapes=[pltpu.VMEM((h, 1), F32), pltpu.VMEM((h, 1), F32), pltpu.VMEM((h, 1), F32),
                            pltpu.VMEM((h, n_pages), F32)]),
        compiler_params=_params("arbitrary", "arbitrary"),
        name="fox_sample_scores",
    )(newest_first, qbd, k_new, f_new, lmat, *([cache_kt] * np_), *([cache_ft] * np_))

    order = jnp.arange(1, n_pages + 1, dtype=I32)
    count = jnp.max(jnp.where(need[:, 0, :] > 0, order, 0), axis=-1)

    per_seq = lambda shp: pl.BlockSpec(shp, lambda b, *_: (b,) + (0,) * (len(shp) - 1))
    return pl.pallas_call(
        _pv_sample_kernel,
        out_shape=jax.ShapeDtypeStruct((bsz, 1, a), F32),
        grid_spec=pltpu.PrefetchScalarGridSpec(
            num_scalar_prefetch=2,
            grid=(bsz,),
            in_specs=[per_seq((1, n_pages, h, PAGE_SIZE)), per_seq((1, h, 1)), per_seq((1, h, 1)),
                      per_seq((1, 1, a)), pl.BlockSpec(memory_space=pl.ANY)],
            out_specs=per_seq((1, 1, a)),
            scratch_shapes=[pltpu.VMEM((PV_SLOTS, a, PAGE_SIZE), F32),
                            pltpu.SemaphoreType.DMA((PV_SLOTS,))]),
        compiler_params=_params("arbitrary"),
        name="fox_sample_pv",
    )(newest_first, count, p, pn, l, v_new, cache_vt)


def _group_rms(x, bd, g):
    ms = jnp.dot((x * x).astype(BF16), bd, preferred_element_type=F32)
    return x * lax.rsqrt(ms * (1.0 / HEAD_DIM) + EPS) * g


def _route(logits):
    lane = lax.broadcasted_iota(jnp.int32, logits.shape, 1).astype(F32)
    big = jnp.float32(1e9)
    g0 = float(ROUTER_GROUP_LANE0)
    epg = float(EXPERTS_PER_GROUP)
    glog = jnp.where((lane >= g0) & (lane < g0 + N_GROUPS), logits, -jnp.inf)
    gmax = jnp.max(glog, axis=-1, keepdims=True)
    gidx = jnp.min(jnp.where(glog == gmax, lane, big), axis=-1, keepdims=True) - g0
    gp = 1.0 / jnp.sum(jnp.exp(glog - gmax), axis=-1, keepdims=True)
    lo = gidx * epg
    el = jnp.where((lane >= lo) & (lane < lo + epg), logits, -jnp.inf)
    m1 = jnp.max(el, axis=-1, keepdims=True)
    i1 = jnp.min(jnp.where(el == m1, lane, big), axis=-1, keepdims=True)
    el2 = jnp.where(lane == i1, -jnp.inf, el)
    m2 = jnp.max(el2, axis=-1, keepdims=True)
    i2 = jnp.min(jnp.where(el2 == m2, lane, big), axis=-1, keepdims=True)
    t = jnp.exp(m2 - m1)
    w1 = gp / (1.0 + t)
    w2 = w1 * t
    first_low = i1 < i2
    ea = jnp.minimum(i1, i2) - lo
    eb = jnp.maximum(i1, i2) - lo
    pair = ea * (2.0 * epg - 1.0 - ea) * 0.5 + (eb - ea - 1.0)
    bucket = gidx * float(PAIRS_PER_GROUP) + pair
    return bucket, jnp.where(first_low, w1, w2), jnp.where(first_low, w2, w1)


def _merge_kernel(sample, attn_ref, cin_ref, prev_ref, prev2_ref, gb_ref, x_ref, mod_ref, cw_ref,
                  ga_ref, gc_ref, bd_ref, wout_ref, g2_ref, wr_ref, br_ref, tri_ref, cnt_ref,
                  x1_ref, hrow_ref, info_ref, cnto_ref, cnt_sc):
    d, a = D_MODEL, ATTN_WIDTH
    cin = cin_ref[0]
    tm = cin.shape[0]
    first_step = (pl.program_id(0) == 0) & (pl.program_id(1) == 0)

    @pl.when(first_step)
    def _():
        cnt_sc[...] = cnt_ref[...]

    w0, w1, w2 = cw_ref[0:1, :], cw_ref[1:2, :], cw_ref[2:3, :]
    if sample:
        conv_y = w0 * prev2_ref[0] + w1 * prev_ref[0] + w2 * cin
    else:
        first = pl.program_id(1) == 0
        tail = jnp.where(first, 0.0, prev_ref[0])
        ext = jnp.concatenate([tail, cin], axis=0)
        conv_y = w0 * ext[6:6 + tm] + w1 * ext[7:7 + tm] + w2 * cin
    bd = bd_ref[...]
    an = _group_rms(attn_ref[0], bd, ga_ref[...]).astype(BF16)
    cn = _group_rms(gb_ref[0] * conv_y, bd, gc_ref[...]).astype(BF16)
    mix = (jnp.dot(an, wout_ref[0:a, :], preferred_element_type=F32)
           + jnp.dot(cn, wout_ref[a:, :], preferred_element_type=F32))
    ga1 = mod_ref[0, :, 2 * d:3 * d]
    sh2 = mod_ref[0, :, 3 * d:4 * d]
    sc2 = mod_ref[0, :, 4 * d:5 * d]
    x1 = x_ref[0] + ga1 * mix
    x1_ref[0] = x1
    h2 = _rms(x1, g2_ref[...]) * (1.0 + sc2) + sh2
    hb = h2.astype(BF16)
    hl = (h2 - hb.astype(F32)).astype(BF16)
    both = jnp.dot(hb, wr_ref[...], preferred_element_type=F32)
    logits = (both[:, 0:LANES] + both[:, LANES:]
              + jnp.dot(hl, wr_ref[:, 0:LANES], preferred_element_type=F32)) + br_ref[...]
    bucket, w_lo, w_hi = _route(logits)

    lane = lax.broadcasted_iota(jnp.int32, (tm, LANES), 1)
    lanef = lane.astype(F32)
    onehot = lanef == bucket
    incl = jnp.dot(tri_ref[...], jnp.where(onehot, 1.0, 0.0).astype(BF16), preferred_element_type=F32)
    cnt = cnt_sc[...]
    rank = jnp.sum(jnp.where(onehot, incl - 1.0 + cnt, 0.0), axis=-1, keepdims=True)
    cnt_new = cnt + incl[tm - 1:tm, :]
    cnt_sc[...] = cnt_new
    cnto_ref[...] = cnt_new
    info_ref[0] = jnp.where(lane == 0, bucket, jnp.where(lane == 1, rank, 0.0))

    hrow_ref[0, :, 0:d] = h2
    hrow_ref[0, :, d:] = jnp.where(lane == 0, w_lo, jnp.where(lane == 1, w_hi, 0.0))


def _merge(sample, attn, cin, prev, prev2, gb, x, mod, cw, ga, gc, bd, wout, g2, wr, br, tri, cnt, tm):
    g, t, d = x.shape
    r = mod.shape[1]
    c = CONV_WIDTH
    rspec = lambda w: pl.BlockSpec((1, tm, w), lambda b, i: (b, i, 0))
    const = lambda arr: pl.BlockSpec(arr.shape, lambda b, i: (0,) * arr.ndim)
    if sample:
        prev_spec = rspec(c)
    else:
        prev_spec = pl.BlockSpec((1, 8, c), lambda b, i: (b, jnp.maximum(i * (tm // 8) - 1, 0), 0))
    return pl.pallas_call(
        functools.partial(_merge_kernel, sample),
        out_shape=(jax.ShapeDtypeStruct((g, t, d), F32), jax.ShapeDtypeStruct((g, t, ROW_WIDTH), F32),
                   jax.ShapeDtypeStruct((g, t, LANES), F32), jax.ShapeDtypeStruct((1, LANES), F32)),
        grid=(g, t // tm),
        in_specs=[rspec(ATTN_WIDTH), rspec(c), prev_spec, prev_spec, rspec(c), rspec(d),
                  pl.BlockSpec((1, r, 6 * d), lambda b, i: (b, 0, 0)),
                  const(cw), const(ga), const(gc), const(bd), const(wout), const(g2), const(wr),
                  const(br), const(tri), const(cnt)],
        out_specs=(rspec(d), rspec(ROW_WIDTH), rspec(LANES), pl.BlockSpec((1, LANES), lambda b, i: (0, 0))),
        scratch_shapes=[pltpu.VMEM((1, LANES), F32)],
        compiler_params=_params("arbitrary", "arbitrary"),
        name="mix_out_router",
    )(attn, cin, prev, prev2, gb, x, mod, cw, ga, gc, bd, wout, g2, wr, br, tri, cnt)


def _dispatch_kernel(n_steps, pos_ref, h_ref, xs_in_ref, xs_ref, buf, sem):
    del xs_in_ref
    tm = h_ref.shape[0]
    i = pl.program_id(0)
    slot = i % 2

    def drain(s):
        pltpu.make_async_copy(buf.at[s], xs_ref.at[pl.ds(0, tm), :], sem.at[s]).wait()

    @pl.when(i >= 2)
    def _():
        drain(slot)

    buf[slot] = h_ref[...]
    base = i * tm
    for r in range(tm):
        pltpu.make_async_copy(buf.at[slot, pl.ds(r, 1), :], xs_ref.at[pl.ds(pos_ref[base + r], 1), :],
                              sem.at[slot]).start(priority=r % 2)

    @pl.when(i == n_steps - 1)
    def _():
        drain(slot)
        if n_steps > 1:
            drain(1 - slot)


def _dispatch(pos, hrow, xs, tm):
    n, w = hrow.shape
    n_steps = n // tm
    return pl.pallas_call(
        functools.partial(_dispatch_kernel, n_steps),
        out_shape=jax.ShapeDtypeStruct(xs.shape, xs.dtype),
        grid_spec=pltpu.PrefetchScalarGridSpec(
            num_scalar_prefetch=1,
            grid=(n_steps,),
            in_specs=[pl.BlockSpec((tm, w), lambda i, pos: (i, 0)),
                      pl.BlockSpec(memory_space=pl.ANY)],
            out_specs=pl.BlockSpec(memory_space=pl.ANY),
            scratch_shapes=[pltpu.VMEM((2, tm, w), F32), pltpu.SemaphoreType.DMA((2,))]),
        input_output_aliases={2: 0},
        compiler_params=_params("arbitrary"),
        name="moe_dispatch",
    )(pos, hrow, xs)


def _experts_kernel(lo_ref, hi_ref, nt_ref, xs_ref, wgl_ref, wul_ref, wdl_ref, wgh_ref, wuh_ref, wdh_ref,
                    o_ref):
    del lo_ref, hi_ref
    live = pl.program_id(0) < nt_ref[0]

    @pl.when(jnp.logical_not(live))
    def _():
        o_ref[...] = jnp.zeros_like(o_ref)

    @pl.when(live)
    def _():
        d = D_MODEL
        x = xs_ref[:, 0:d].astype(BF16)
        w_lo = xs_ref[:, d:d + 1]
        w_hi = xs_ref[:, d + 1:d + 2]

        def hidden(wg_ref, wu_ref, w):
            gt = jnp.dot(x, wg_ref[0], preferred_element_type=F32)
            up = jnp.dot(x, wu_ref[0], preferred_element_type=F32)
            return (gt * jax.nn.sigmoid(gt) * up * w).astype(BF16)

        o_ref[...] = (jnp.dot(hidden(wgl_ref, wul_ref, w_lo), wdl_ref[0], preferred_element_type=F32)
                      + jnp.dot(hidden(wgh_ref, wuh_ref, w_hi), wdh_ref[0], preferred_element_type=F32))


def _experts(tile_lo, tile_hi, n_tiles, xs, wg, wu, wdn):
    rows, w = xs.shape
    d = D_MODEL
    tg = MOE_TILE
    tile = lambda i, lo, hi, nt: (jnp.minimum(i, nt[0] - 1), 0)
    wspec = lambda shp, which: pl.BlockSpec(
        (1,) + shp, (lambda i, lo, hi, nt: (lo[i], 0, 0)) if which == 0 else (lambda i, lo, hi, nt: (hi[i], 0, 0)))
    return pl.pallas_call(
        _experts_kernel,
        out_shape=jax.ShapeDtypeStruct((rows, d), F32),
        grid_spec=pltpu.PrefetchScalarGridSpec(
            num_scalar_prefetch=3,
            grid=(rows // tg,),
            in_specs=[pl.BlockSpec((tg, w), tile),
                      wspec((d, D_EXPERT), 0), wspec((d, D_EXPERT), 0), wspec((D_EXPERT, d), 0),
                      wspec((d, D_EXPERT), 1), wspec((d, D_EXPERT), 1), wspec((D_EXPERT, d), 1)],
            out_specs=pl.BlockSpec((tg, d), lambda i, lo, hi, nt: (i, 0))),
        compiler_params=_params("arbitrary"),
        name="moe_experts",
    )(tile_lo, tile_hi, n_tiles, xs, wg, wu, wdn, wg, wu, wdn)


def _final_kernel(n_tiles, pos_ref, x1_ref, mod_ref, modf_ref, gf_ref, os_ref, y_ref, buf, sem):
    d = D_MODEL
    tm = x1_ref.shape[0]
    i = pl.program_id(0)

    @pl.when(i < n_tiles)
    def _():
        slot = i % 2
        base = i * tm
        for r in range(tm):
            pltpu.make_async_copy(os_ref.at[pl.ds(pos_ref[base + r], 1), :],
                                  buf.at[slot, pl.ds(r, 1), :], sem.at[slot]).start(priority=r % 2)

    @pl.when(i > 0)
    def _():
        slot = (i - 1) % 2
        pltpu.make_async_copy(os_ref.at[pl.ds(0, tm), :], buf.at[slot], sem.at[slot]).wait()
        ga2 = mod_ref[0, :, 5 * d:6 * d]
        x2 = x1_ref[...] + ga2 * buf[slot]
        shf = modf_ref[0, :, 0:d]
        scf = modf_ref[0, :, d:2 * d]
        y_ref[...] = _rms(x2, gf_ref[...]) * (1.0 + scf) + shf


def _final(pos, x1, mod, modf, gf, out_sorted, tm, tiles_per_mod):
    n, d = x1.shape
    r = mod.shape[1]
    n_tiles = n // tm
    done = lambda i: jnp.maximum(i - 1, 0)
    return pl.pallas_call(
        functools.partial(_final_kernel, n_tiles),
        out_shape=jax.ShapeDtypeStruct((n, d), F32),
        grid_spec=pltpu.PrefetchScalarGridSpec(
            num_scalar_prefetch=1,
            grid=(n_tiles + 1,),
            in_specs=[pl.BlockSpec((tm, d), lambda i, pos: (done(i), 0)),
                      pl.BlockSpec((1, r, 6 * d), lambda i, pos: (done(i) // tiles_per_mod, 0, 0)),
                      pl.BlockSpec((1, r, 2 * d), lambda i, pos: (done(i) // tiles_per_mod, 0, 0)),
                      pl.BlockSpec((1, d), lambda i, pos: (0, 0)),
                      pl.BlockSpec(memory_space=pl.ANY)],
            out_specs=pl.BlockSpec((tm, d), lambda i, pos: (done(i), 0)),
            scratch_shapes=[pltpu.VMEM((2, tm, d), F32), pltpu.SemaphoreType.DMA((2,))]),
        compiler_params=_params("arbitrary"),
        name="moe_combine_final",
    )(pos, x1, mod, modf, gf, out_sorted)


def kernel(x_prompt, x_sample, c_prompt, c_sample, cache_k, cache_v, cache_logf, state_conv, page_table, w_ada, b_ada, g_norm1, w_in, b_forget, conv_w, g_attn_out, g_conv_out, w_out, g_norm2, w_router_group, b_router_group, w_router_expert, b_router_expert, w_expert_gate, w_expert_up, w_expert_down, w_ada_final, b_ada_final, g_final):
    d, a, c, h = D_MODEL, ATTN_WIDTH, CONV_WIDTH, N_HEADS
    bsz, t, _ = x_prompt.shape
    dbs = x_sample.shape[0]
    n_prompt = bsz * t
    assert w_ada.shape[0] == 1 and x_sample.shape[1] == 1

    c_all = jnp.concatenate([c_prompt, c_sample], axis=0)
    mod = _ada(c_all, w_ada[0], b_ada[0])
    modf = _ada(c_all, w_ada_final, b_ada_final)
    mod_p, mod_s = mod[:bsz].reshape(bsz, 1, 6 * d), mod[bsz:].reshape(1, dbs, 6 * d)
    modf_p, modf_s = modf[:bsz].reshape(bsz, 1, 2 * d), modf[bsz:].reshape(1, dbs, 2 * d)

    wt = w_in[0].T.astype(BF16)
    wq = wt[0:a].T
    wkt = wt[a:2 * a]
    wvt = wt[2 * a:3 * a]
    wv = wvt.T
    wft = jnp.zeros((16, d), BF16).at[0:h].set(wt[3 * a:3 * a + h])
    wc = wt[3 * a + h:].T
    bf = b_forget[0].reshape(h, 1)
    g1 = g_norm1[0].reshape(1, d)
    g2 = g_norm2[0].reshape(1, d)
    gf = g_final.reshape(1, d)
    ga = g_attn_out[0].reshape(1, a)
    gc = g_conv_out[0].reshape(1, c)
    cw = conv_w[0]
    wout = w_out[0].astype(BF16)
    lane_group = jnp.arange(a) // HEAD_DIM
    bd = (lane_group[:, None] == lane_group[None, :]).astype(BF16)
    wr = jnp.zeros((d, LANES), F32)
    wr = wr.at[:, 0:N_EXPERTS].set(w_router_expert[0])
    wr = wr.at[:, ROUTER_GROUP_LANE0:ROUTER_GROUP_LANE0 + N_GROUPS].set(w_router_group[0])
    wrh = wr.astype(BF16)
    wr = jnp.concatenate([wrh, (wr - wrh.astype(F32)).astype(BF16)], axis=1)
    br = jnp.zeros((1, LANES), F32)
    br = br.at[0, 0:N_EXPERTS].set(b_router_expert[0])
    br = br.at[0, ROUTER_GROUP_LANE0:ROUTER_GROUP_LANE0 + N_GROUPS].set(b_router_group[0])
    wg = w_expert_gate[0].astype(BF16)
    wu = w_expert_up[0].astype(BF16)
    wdn = w_expert_down[0].astype(BF16)
    tri = lambda m: (jnp.arange(m)[:, None] >= jnp.arange(m)[None, :]).astype(BF16)

    q_p, kt_p, vt_p, ktb_p, vb_p, lft_p, cin_p, gb_p = _inproj(
        True, x_prompt, mod_p, g1, wq, wkt, wvt, wv, wft, bf, wc, TOKEN_TILE)
    attn_p = _attn_prompt(q_p, ktb_p, _cumsum(lft_p), vb_p)
    x1_p, hrow_p, info_p, cnt_p = _merge(
        False, attn_p, cin_p, cin_p, cin_p, gb_p, x_prompt, mod_p, cw, ga, gc, bd, wout, g2, wr, br,
        tri(TOKEN_TILE), jnp.zeros((1, LANES), F32), TOKEN_TILE)

    xs_ = x_sample.reshape(1, dbs, d)
    q_s, k_s, v_s, lft_s, cin_s, gb_s = _inproj(False, xs_, mod_s, g1, wq, wkt, wvt, wv, wft, bf, wc, dbs)
    head_of_lane = jnp.arange(a) // HEAD_DIM
    qbd = jnp.where(head_of_lane[None, None, :] == jnp.arange(h)[None, :, None],
                    q_s.reshape(dbs, 1, a), jnp.zeros((), BF16))
    n_pool = cache_k.shape[1]
    page_major = lambda z: z[0].transpose(0, 2, 3, 1).reshape(n_pool, a, PAGE_SIZE)
    attn_s = _attn_sample(
        page_table, qbd, k_s.reshape(dbs, 1, a), v_s.reshape(dbs, 1, a),
        lft_s[0].T.reshape(dbs, h, 1),
        page_major(cache_k), page_major(cache_v), cache_logf[0].transpose(0, 2, 1))
    st = state_conv[0]
    x1_s, hrow_s, info_s, cnt_all = _merge(
        True, attn_s.reshape(1, dbs, a), cin_s, st[:, 1][None], st[:, 0][None], gb_s, xs_, mod_s, cw, ga,
        gc, bd, wout, g2, wr, br, tri(dbs), cnt_p, dbs)

    n_rows_max = n_prompt + dbs + N_BUCKETS * (MOE_TILE - 1)
    max_tiles = -(-n_rows_max // MOE_TILE)
    counts = cnt_all[0, 0:N_BUCKETS].astype(I32)
    padded = (counts + (MOE_TILE - 1)) // MOE_TILE * MOE_TILE
    ends = jnp.cumsum(padded)
    offsets = ends - padded
    n_tiles = (ends[-1] // MOE_TILE).astype(I32)
    tile_ids = jnp.minimum(jnp.arange(max_tiles, dtype=I32), n_tiles - 1)
    tile_bucket = jnp.sum(tile_ids[:, None] >= (ends // MOE_TILE)[None, :], axis=1)
    lo_tab, hi_tab = _pair_tables()
    tile_lo = jnp.asarray(lo_tab)[tile_bucket]
    tile_hi = jnp.asarray(hi_tab)[tile_bucket]
    bucket_ids = jnp.arange(N_BUCKETS, dtype=F32)

    def position(info):
        bucket, rank = info[..., 0:1], info[..., 1].astype(I32)
        base = jnp.sum(jnp.where(bucket == bucket_ids, offsets, 0), axis=-1)
        return (base + rank).reshape(-1)

    pos_p, pos_s = position(info_p), position(info_s)

    xs = jnp.zeros((max_tiles * MOE_TILE, ROW_WIDTH), F32)
    xs = _dispatch(pos_p, hrow_p.reshape(n_prompt, ROW_WIDTH), xs, TOKEN_TILE)
    xs = _dispatch(pos_s, hrow_s.reshape(dbs, ROW_WIDTH), xs, dbs)
    out_sorted = _experts(tile_lo, tile_hi, n_tiles.reshape(1), xs, wg, wu, wdn)
    y_p = _final(pos_p, x1_p.reshape(n_prompt, d), mod_p, modf_p, gf, out_sorted, TOKEN_TILE,
                 t // TOKEN_TILE)
    y_s = _final(pos_s, x1_s.reshape(dbs, d), mod_s, modf_s, gf, out_sorted, dbs, 1)

    kv_p = lambda zt: zt.reshape(bsz, h, HEAD_DIM, t).transpose(0, 3, 1, 2)[None]
    kv_s = lambda z: z.reshape(1, dbs, 1, h, HEAD_DIM)
    return (y_p.reshape(bsz, t, d), y_s.reshape(dbs, 1, d),
            kv_p(kt_p), kv_p(vt_p),
            lft_p.transpose(0, 2, 1)[None],
            cin_p[:, t - (CONV_K - 1):, :][None],
            kv_s(k_s), kv_s(v_s),
            lft_s[0].T.reshape(1, dbs, 1, h),
            jnp.stack([st[:, 1], cin_s[0]], axis=1)[None])
```

```python
import functools

import numpy as np
import jax
import jax.numpy as jnp
from jax import lax
from jax.experimental import pallas as pl
from jax.experimental.pallas import tpu as pltpu

F32 = jnp.float32
BF16 = jnp.bfloat16
I32 = jnp.int32

D_MODEL = 1024
N_HEADS = 8
HEAD_DIM = 64
ATTN_WIDTH = N_HEADS * HEAD_DIM
CONV_WIDTH = D_MODEL - ATTN_WIDTH
CONV_K = 3
N_GROUPS = 4
EXPERTS_PER_GROUP = 8
N_EXPERTS = N_GROUPS * EXPERTS_PER_GROUP
D_EXPERT = 256
PAGE_SIZE = 128
EPS = 1e-6

LANES = 128
ROUTER_GROUP_LANE0 = 32
VMEM_LIMIT = 56 * 1024 * 1024
TOKEN_TILE = 512
ATTN_TILE = 512
PAGES_PER_STEP = 64
PV_SLOTS = 8

PAIRS_PER_GROUP = EXPERTS_PER_GROUP * (EXPERTS_PER_GROUP - 1) // 2
N_BUCKETS = N_GROUPS * PAIRS_PER_GROUP
MOE_TILE = 256
ROW_WIDTH = D_MODEL + LANES
LOG2E = 1.4426950408889634


def _pair_tables():
    lo, hi = [], []
    for g in range(N_GROUPS):
        for a in range(EXPERTS_PER_GROUP):
            for b in range(a + 1, EXPERTS_PER_GROUP):
                lo.append(g * EXPERTS_PER_GROUP + a)
                hi.append(g * EXPERTS_PER_GROUP + b)
    return np.asarray(lo, np.int32), np.asarray(hi, np.int32)


def _params(*sem):
    return pltpu.CompilerParams(dimension_semantics=sem, vmem_limit_bytes=VMEM_LIMIT)


def _rms(x, g):
    return x * lax.rsqrt(jnp.mean(x * x, axis=-1, keepdims=True) + EPS) * g


def _split3(a):
    hi = a.astype(BF16)
    r = a - hi.astype(F32)
    mid = r.astype(BF16)
    lo = (r - mid.astype(F32)).astype(BF16)
    return hi, mid, lo


def _dot3(a, b_exact):
    hi, mid, lo = _split3(a)
    d = lambda t: jnp.dot(t, b_exact, preferred_element_type=F32)
    return d(hi) + d(mid) + d(lo)


def _dot_nt(a, b):
    return lax.dot_general(a, b, (((1,), (1,)), ((), ())), preferred_element_type=F32)


def _ada_kernel(c_ref, w_ref, b_ref, o_ref):
    c = c_ref[...]
    s = (c * jax.nn.sigmoid(c)).astype(BF16)
    o_ref[...] = jnp.dot(s, w_ref[...].astype(BF16), preferred_element_type=F32) + b_ref[...]


def _ada(c, w, b):
    m, d = c.shape
    n = w.shape[1]
    tn = 1024
    return pl.pallas_call(
        _ada_kernel,
        out_shape=jax.ShapeDtypeStruct((m, n), F32),
        grid=(n // tn,),
        in_specs=[pl.BlockSpec((m, d), lambda j: (0, 0)),
                  pl.BlockSpec((d, tn), lambda j: (0, j)),
                  pl.BlockSpec((1, tn), lambda j: (0, j))],
        out_specs=pl.BlockSpec((m, tn), lambda j: (0, j)),
        compiler_params=_params("arbitrary"),
        name="adaln_mod",
    )(c, w, b.reshape(1, n))


def _spread_heads(x, extra_ones):
    rows = x.shape[0]
    lane = lax.broadcasted_iota(jnp.int32, (rows, LANES), 1)
    low = lane < HEAD_DIM
    tiles = []
    for pair in range(N_HEADS // 2):
        tile = x[:, pair * LANES:(pair + 1) * LANES]
        for hh, vals in enumerate((tile, pltpu.roll(tile, HEAD_DIM, 1))):
            ones = functools.reduce(jnp.logical_or, [lane == o for o in extra_ones(2 * pair + hh)])
            tiles.append(jnp.where(low, vals, jnp.where(ones, 1.0, 0.0)).astype(BF16))
    return jnp.concatenate(tiles, axis=-1)


def _inproj_kernel(prompt, x_ref, mod_ref, g1_ref, wq_ref, wkt_ref, wv_ref, wft_ref, bf_ref,
                   wc_ref, *outs):
    c, d = CONV_WIDTH, D_MODEL
    x = x_ref[0]
    sh1 = mod_ref[0, :, 0:d]
    sc1 = mod_ref[0, :, d:2 * d]
    h = _rms(x, g1_ref[...]) * (1.0 + sc1) + sh1
    hb = h.astype(BF16)
    q = jnp.dot(hb, wq_ref[...], preferred_element_type=F32)
    if prompt:
        q_ref, kt32_ref, vt32_ref, ktb_ref, vb_ref, lft_ref, cin_ref, gb_ref = outs
        kt = _dot_nt(wkt_ref[...], hb)
        v = jnp.dot(hb, wv_ref[...], preferred_element_type=F32)
        kt32_ref[0] = kt
        vt32_ref[0] = v.T
        ktb_ref[0] = kt.astype(BF16)
        q_ref[0] = _spread_heads(q * (HEAD_DIM ** -0.5 * LOG2E),
                                 lambda hd: [HEAD_DIM + part * N_HEADS + hd for part in range(3)])
        vb_ref[0] = _spread_heads(v, lambda hd: [HEAD_DIM])
    else:
        q_ref, k32_ref, v32_ref, lft_ref, cin_ref, gb_ref = outs
        k32_ref[0] = _dot_nt(hb, wkt_ref[...])
        v32_ref[0] = jnp.dot(hb, wv_ref[...], preferred_element_type=F32)
        q_ref[0] = (q * (HEAD_DIM ** -0.5)).astype(BF16)
    z = _dot_nt(wft_ref[...], hb)[0:N_HEADS, :] + bf_ref[...]
    lft_ref[0] = jnp.minimum(z, 0.0) - jnp.log1p(jnp.exp(-jnp.abs(z)))
    cv = jnp.dot(hb, wc_ref[...], preferred_element_type=F32)
    cin_ref[0] = cv[:, 2 * c:3 * c] * cv[:, 0:c]
    gb_ref[0] = cv[:, c:2 * c]


def _inproj(prompt, x, mod, g1, wq, wkt, wv, wft, bf, wc, tm):
    g, t, d = x.shape
    r = mod.shape[1]
    a, c = ATTN_WIDTH, CONV_WIDTH
    row = lambda w, dt: jax.ShapeDtypeStruct((g, t, w), dt)
    col = lambda w, dt: jax.ShapeDtypeStruct((g, w, t), dt)
    rspec = lambda w: pl.BlockSpec((1, tm, w), lambda b, i: (b, i, 0))
    cspec = lambda w: pl.BlockSpec((1, w, tm), lambda b, i: (b, 0, i))
    const = lambda arr: pl.BlockSpec(arr.shape, lambda b, i: (0,) * arr.ndim)
    if prompt:
        out_shape = (row(2 * a, BF16), col(a, F32), col(a, F32), col(a, BF16), row(2 * a, BF16),
                     col(N_HEADS, F32), row(c, F32), row(c, F32))
        out_specs = (rspec(2 * a), cspec(a), cspec(a), cspec(a), rspec(2 * a), cspec(N_HEADS), rspec(c),
                     rspec(c))
    else:
        out_shape = (row(a, BF16), row(a, F32), row(a, F32), col(N_HEADS, F32), row(c, F32), row(c, F32))
        out_specs = (rspec(a), rspec(a), rspec(a), cspec(N_HEADS), rspec(c), rspec(c))
    return pl.pallas_call(
        functools.partial(_inproj_kernel, prompt),
        out_shape=out_shape,
        grid=(g, t // tm),
        in_specs=[rspec(d),
                  pl.BlockSpec((1, r, 6 * d), lambda b, i: (b, 0, 0)),
                  const(g1), const(wq), const(wkt), const(wv), const(wft), const(bf),
                  const(wc)],
        out_specs=out_specs,
        compiler_params=_params("arbitrary", "arbitrary"),
        name="norm_inproj",
    )(x, mod, g1, wq, wkt, wv, wft, bf, wc)


def _cumsum_kernel(lf_ref, u_ref, o_ref):
    t = lf_ref.shape[2]
    tc = u_ref.shape[0]
    carry = jnp.zeros((N_HEADS, 1), F32)
    pad = jnp.zeros((HEAD_DIM - 3 * N_HEADS, tc), F32)
    for i in range(t // tc):
        blk = lf_ref[0, :, i * tc:(i + 1) * tc]
        cs = _dot3(blk, u_ref[...]) + carry
        carry = cs[:, tc - 1:tc]
        parts = [p.astype(F32) for p in _split3(cs * (-LOG2E))]
        o_ref[0, :, i * tc:(i + 1) * tc] = jnp.concatenate(parts + [pad], axis=0).astype(BF16)


def _cumsum(lft):
    g, h, t = lft.shape
    tc = 512
    idx = jnp.arange(tc)
    upper = (idx[:, None] <= idx[None, :]).astype(BF16)
    return pl.pallas_call(
        _cumsum_kernel,
        out_shape=jax.ShapeDtypeStruct((g, HEAD_DIM, t), BF16),
        grid=(g,),
        in_specs=[pl.BlockSpec((1, h, t), lambda b: (b, 0, 0)),
                  pl.BlockSpec((tc, tc), lambda b: (0, 0))],
        out_specs=pl.BlockSpec((1, HEAD_DIM, t), lambda b: (b, 0, 0)),
        compiler_params=_params("arbitrary"),
        name="logf_cumsum",
    )(lft, upper)


def _attn_prompt_kernel(q_ref, kt_ref, kb_ref, v_ref, o_ref):
    t = q_ref.shape[1]
    ta = ATTN_TILE
    n = t // ta
    row = lax.broadcasted_iota(jnp.int32, (ta, ta), 0)
    col = lax.broadcasted_iota(jnp.int32, (ta, ta), 1)
    causal = col <= row
    kb = kb_ref[0]
    kt_aug = [jnp.concatenate([kt_ref[0, hh * HEAD_DIM:(hh + 1) * HEAD_DIM, :], kb], axis=0)
              for hh in range(2)]
    for i in range(n):
        outs = []
        for hh in range(2):
            ls = slice(hh * LANES, (hh + 1) * LANES)
            qa = q_ref[0, i * ta:(i + 1) * ta, ls]
            m = jnp.full((ta, 1), -jnp.inf, F32)
            acc = jnp.zeros((ta, LANES), F32)
            for k0, k1 in ([(0, i * ta)] if i else []) + [(i * ta, (i + 1) * ta)]:
                s = jnp.dot(qa, kt_aug[hh][:, k0:k1], preferred_element_type=F32)
                if k0 == i * ta:
                    s = jnp.where(causal, s, -jnp.inf)
                m_new = jnp.maximum(m, jnp.max(s, axis=-1, keepdims=True))
                p = jnp.exp2(s - m_new).astype(BF16)
                acc = jnp.exp2(m - m_new) * acc + jnp.dot(p, v_ref[0, k0:k1, ls],
                                                          preferred_element_type=F32)
                m = m_new
            outs.append(acc[:, 0:HEAD_DIM] / acc[:, HEAD_DIM:HEAD_DIM + 1])
        o_ref[0, i * ta:(i + 1) * ta, :] = jnp.concatenate(outs, axis=-1)


def _attn_prompt(q_aug, ktb, kbias, v_aug):
    g, t, _ = q_aug.shape
    a = ATTN_WIDTH
    wide = pl.BlockSpec((1, t, 2 * LANES), lambda b, p: (b, 0, p))
    return pl.pallas_call(
        _attn_prompt_kernel,
        out_shape=jax.ShapeDtypeStruct((g, t, a), F32),
        grid=(g, a // LANES),
        in_specs=[wide, pl.BlockSpec((1, LANES, t), lambda b, p: (b, p, 0)),
                  pl.BlockSpec((1, HEAD_DIM, t), lambda b, p: (b, 0, 0)), wide],
        out_specs=pl.BlockSpec((1, t, LANES), lambda b, p: (b, 0, p)),
        compiler_params=_params("arbitrary", "arbitrary"),
        name="fox_prompt_attn",
    )(q_aug, ktb, kbias, v_aug)


def _scores_sample_kernel(pt_ref, qbd_ref, kn_ref, fn_ref, lmat_ref, *rest):
    np_ = PAGES_PER_STEP
    kt_refs, f_refs = rest[0:np_], rest[np_:2 * np_]
    p_ref, pn_ref, l_ref, need_ref, m_sc, snew_sc, carry_sc, pmax_sc = rest[2 * np_:]
    step = pl.program_id(1)
    qbd = qbd_ref[0]

    @pl.when(step == 0)
    def _():
        kn = kn_ref[0].astype(BF16).astype(F32)
        s_new = jnp.sum(qbd.astype(F32) * kn, axis=-1, keepdims=True)
        snew_sc[...] = s_new
        m_sc[...] = s_new
        carry_sc[...] = fn_ref[0]
        pmax_sc[...] = jnp.full(pmax_sc.shape, -jnp.inf, F32)

    f_all = jnp.concatenate([f_refs[i][0] for i in range(np_)], axis=0)
    suffix = _dot3(f_all, lmat_ref[...])
    totals = jnp.sum(f_all, axis=-1, keepdims=True)
    carry = carry_sc[...]
    lane = lax.broadcasted_iota(jnp.int32, pmax_sc.shape, 1)
    pmax = pmax_sc[...]
    for i in range(np_):
        sl = slice(i * N_HEADS, (i + 1) * N_HEADS)
        order = step * np_ + i
        s = jnp.dot(qbd, kt_refs[i][0].astype(BF16), preferred_element_type=F32) + carry + suffix[sl]
        p_ref[0, order] = s
        pmax = jnp.where(lane == order, jnp.max(s, axis=-1, keepdims=True), pmax)
        carry = carry + totals[sl]
    carry_sc[...] = carry
    pmax_sc[...] = pmax
    m_sc[...] = jnp.maximum(m_sc[...], jnp.max(pmax, axis=-1, keepdims=True))

    @pl.when(step == pl.num_programs(1) - 1)
    def _():
        m = m_sc[...]
        p = jnp.exp(p_ref[0] - m[None])
        p_ref[0] = p
        pn = jnp.exp(snew_sc[...] - m)
        pn_ref[0] = pn
        l_ref[0] = jnp.sum(jnp.sum(p, axis=0), axis=-1, keepdims=True) + pn
        top = jnp.exp(pmax_sc[...] - m).astype(BF16).astype(F32)
        need_ref[0] = jnp.max(jnp.where(top != 0.0, 1.0, 0.0), axis=0, keepdims=True)


def _pv_sample_kernel(pages_ref, count_ref, p_ref, pn_ref, l_ref, vn_ref, vt_hbm, o_ref, vbuf, sem):
    b = pl.program_id(0)
    n = count_ref[b]
    base = b * p_ref.shape[1]

    def fetch(order, slot):
        return pltpu.make_async_copy(vt_hbm.at[pages_ref[base + order]], vbuf.at[slot], sem.at[slot])

    for slot in range(PV_SLOTS):
        @pl.when(slot < n)
        def _(slot=slot):
            fetch(slot, slot).start()

    def body(order, acc):
        slot = order % PV_SLOTS
        fetch(order, slot).wait()
        acc = acc + _dot_nt(p_ref[0, order].astype(BF16), vbuf[slot].astype(BF16))

        @pl.when(order + PV_SLOTS < n)
        def _():
            fetch(order + PV_SLOTS, slot).start()

        return acc

    acc = lax.fori_loop(0, n, body,
                        pn_ref[0].astype(BF16).astype(F32) * vn_ref[0].astype(BF16).astype(F32))
    res = acc / l_ref[0]
    head = lax.broadcasted_iota(jnp.int32, res.shape, 0)
    lane_head = lax.broadcasted_iota(jnp.int32, res.shape, 1) // HEAD_DIM
    o_ref[0] = jnp.sum(jnp.where(head == lane_head, res, 0.0), axis=0, keepdims=True)


def _attn_sample(page_table, qbd, k_new, v_new, f_new, cache_kt, cache_vt, cache_ft):
    bsz, n_pages = page_table.shape
    a, h = ATTN_WIDTH, N_HEADS
    np_ = PAGES_PER_STEP
    n_steps = n_pages // np_
    idx = jnp.arange(PAGE_SIZE)
    lmat = (idx[:, None] > idx[None, :]).astype(BF16)
    newest_first = page_table[:, ::-1].reshape(-1)
    per_b = lambda shp: pl.BlockSpec(shp, lambda b, s, *_: (b,) + (0,) * (len(shp) - 1))

    def page_spec(shape, i):
        return pl.BlockSpec(shape, lambda b, s, pt, *_: (pt[(b * n_steps + s) * np_ + i], 0, 0))

    p, pn, l, need = pl.pallas_call(
        _scores_sample_kernel,
        out_shape=(jax.ShapeDtypeStruct((bsz, n_pages, h, PAGE_SIZE), F32),
                   jax.ShapeDtypeStruct((bsz, h, 1), F32), jax.ShapeDtypeStruct((bsz, h, 1), F32),
                   jax.ShapeDtypeStruct((bsz, 1, n_pages), F32)),
        grid_spec=pltpu.PrefetchScalarGridSpec(
            num_scalar_prefetch=1,
            grid=(bsz, n_steps),
            in_specs=([per_b((1, h, a)), per_b((1, 1, a)), per_b((1, h, 1)),
                       pl.BlockSpec((PAGE_SIZE, PAGE_SIZE), lambda b, s, pt: (0, 0))]
                      + [page_spec((1, a, PAGE_SIZE), i) for i in range(np_)]
                      + [page_spec((1, h, PAGE_SIZE), i) for i in range(np_)]),
            out_specs=(per_b((1, n_pages, h, PAGE_SIZE)), per_b((1, h, 1)), per_b((1, h, 1)),
                       per_b((1, 1, n_pages))),
            scratch_shapes=[pltpu.VMEM((h, 1), F32), pltpu.VMEM((h, 1), F32), pltpu.VMEM((h, 1), F32),
                            pltpu.VMEM((h, n_pages), F32)]),
        compiler_params=_params("arbitrary", "arbitrary"),
        name="fox_sample_scores",
    )(newest_first, qbd, k_new, f_new, lmat, *([cache_kt] * np_), *([cache_ft] * np_))

    order = jnp.arange(1, n_pages + 1, dtype=I32)
    count = jnp.max(jnp.where(need[:, 0, :] > 0, order, 0), axis=-1)

    per_seq = lambda shp: pl.BlockSpec(shp, lambda b, *_: (b,) + (0,) * (len(shp) - 1))
    return pl.pallas_call(
        _pv_sample_kernel,
        out_shape=jax.ShapeDtypeStruct((bsz, 1, a), F32),
        grid_spec=pltpu.PrefetchScalarGridSpec(
            num_scalar_prefetch=2,
            grid=(bsz,),
            in_specs=[per_seq((1, n_pages, h, PAGE_SIZE)), per_seq((1, h, 1)), per_seq((1, h, 1)),
                      per_seq((1, 1, a)), pl.BlockSpec(memory_space=pl.ANY)],
            out_specs=per_seq((1, 1, a)),
            scratch_shapes=[pltpu.VMEM((PV_SLOTS, a, PAGE_SIZE), F32),
                            pltpu.SemaphoreType.DMA((PV_SLOTS,))]),
        compiler_params=_params("arbitrary"),
        name="fox_sample_pv",
    )(newest_first, count, p, pn, l, v_new, cache_vt)


def _group_rms(x, bd, g):
    ms = jnp.dot((x * x).astype(BF16), bd, preferred_element_type=F32)
    return x * lax.rsqrt(ms * (1.0 / HEAD_DIM) + EPS) * g


def _route(logits):
    lane = lax.broadcasted_iota(jnp.int32, logits.shape, 1).astype(F32)
    big = jnp.float32(1e9)
    g0 = float(ROUTER_GROUP_LANE0)
    epg = float(EXPERTS_PER_GROUP)
    glog = jnp.where((lane >= g0) & (lane < g0 + N_GROUPS), logits, -jnp.inf)
    gmax = jnp.max(glog, axis=-1, keepdims=True)
    gidx = jnp.min(jnp.where(glog == gmax, lane, big), axis=-1, keepdims=True) - g0
    gp = 1.0 / jnp.sum(jnp.exp(glog - gmax), axis=-1, keepdims=True)
    lo = gidx * epg
    el = jnp.where((lane >= lo) & (lane < lo + epg), logits, -jnp.inf)
    m1 = jnp.max(el, axis=-1, keepdims=True)
    i1 = jnp.min(jnp.where(el == m1, lane, big), axis=-1, keepdims=True)
    el2 = jnp.where(lane == i1, -jnp.inf, el)
    m2 = jnp.max(el2, axis=-1, keepdims=True)
    i2 = jnp.min(jnp.where(el2 == m2, lane, big), axis=-1, keepdims=True)
    t = jnp.exp(m2 - m1)
    w1 = gp / (1.0 + t)
    w2 = w1 * t
    first_low = i1 < i2
    ea = jnp.minimum(i1, i2) - lo
    eb = jnp.maximum(i1, i2) - lo
    pair = ea * (2.0 * epg - 1.0 - ea) * 0.5 + (eb - ea - 1.0)
    bucket = gidx * float(PAIRS_PER_GROUP) + pair
    return bucket, jnp.where(first_low, w1, w2), jnp.where(first_low, w2, w1)


def _merge_kernel(sample, attn_ref, cin_ref, prev_ref, prev2_ref, gb_ref, x_ref, mod_ref, cw_ref,
                  ga_ref, gc_ref, bd_ref, wout_ref, g2_ref, wr_ref, br_ref, tri_ref, cnt_ref,
                  x1_ref, hrow_ref, info_ref, cnto_ref, cnt_sc):
    d, a = D_MODEL, ATTN_WIDTH
    cin = cin_ref[0]
    tm = cin.shape[0]
    first_step = (pl.program_id(0) == 0) & (pl.program_id(1) == 0)

    @pl.when(first_step)
    def _():
        cnt_sc[...] = cnt_ref[...]

    w0, w1, w2 = cw_ref[0:1, :], cw_ref[1:2, :], cw_ref[2:3, :]
    if sample:
        conv_y = w0 * prev2_ref[0] + w1 * prev_ref[0] + w2 * cin
    else:
        first = pl.program_id(1) == 0
        tail = jnp.where(first, 0.0, prev_ref[0])
        ext = jnp.concatenate([tail, cin], axis=0)
        conv_y = w0 * ext[6:6 + tm] + w1 * ext[7:7 + tm] + w2 * cin
    bd = bd_ref[...]
    an = _group_rms(attn_ref[0], bd, ga_ref[...]).astype(BF16)
    cn = _group_rms(gb_ref[0] * conv_y, bd, gc_ref[...]).astype(BF16)
    mix = (jnp.dot(an, wout_ref[0:a, :], preferred_element_type=F32)
           + jnp.dot(cn, wout_ref[a:, :], preferred_element_type=F32))
    ga1 = mod_ref[0, :, 2 * d:3 * d]
    sh2 = mod_ref[0, :, 3 * d:4 * d]
    sc2 = mod_ref[0, :, 4 * d:5 * d]
    x1 = x_ref[0] + ga1 * mix
    x1_ref[0] = x1
    h2 = _rms(x1, g2_ref[...]) * (1.0 + sc2) + sh2
    hb = h2.astype(BF16)
    hl = (h2 - hb.astype(F32)).astype(BF16)
    both = jnp.dot(hb, wr_ref[...], preferred_element_type=F32)
    logits = (both[:, 0:LANES] + both[:, LANES:]
              + jnp.dot(hl, wr_ref[:, 0:LANES], preferred_element_type=F32)) + br_ref[...]
    bucket, w_lo, w_hi = _route(logits)

    lane = lax.broadcasted_iota(jnp.int32, (tm, LANES), 1)
    lanef = lane.astype(F32)
    onehot = lanef == bucket
    incl = jnp.dot(tri_ref[...], jnp.where(onehot, 1.0, 0.0).astype(BF16), preferred_element_type=F32)
    cnt = cnt_sc[...]
    rank = jnp.sum(jnp.where(onehot, incl - 1.0 + cnt, 0.0), axis=-1, keepdims=True)
    cnt_new = cnt + incl[tm - 1:tm, :]
    cnt_sc[...] = cnt_new
    cnto_ref[...] = cnt_new
    info_ref[0] = jnp.where(lane == 0, bucket, jnp.where(lane == 1, rank, 0.0))

    hrow_ref[0, :, 0:d] = h2
    hrow_ref[0, :, d:] = jnp.where(lane == 0, w_lo, jnp.where(lane == 1, w_hi, 0.0))


def _merge(sample, attn, cin, prev, prev2, gb, x, mod, cw, ga, gc, bd, wout, g2, wr, br, tri, cnt, tm):
    g, t, d = x.shape
    r = mod.shape[1]
    c = CONV_WIDTH
    rspec = lambda w: pl.BlockSpec((1, tm, w), lambda b, i: (b, i, 0))
    const = lambda arr: pl.BlockSpec(arr.shape, lambda b, i: (0,) * arr.ndim)
    if sample:
        prev_spec = rspec(c)
    else:
        prev_spec = pl.BlockSpec((1, 8, c), lambda b, i: (b, jnp.maximum(i * (tm // 8) - 1, 0), 0))
    return pl.pallas_call(
        functools.partial(_merge_kernel, sample),
        out_shape=(jax.ShapeDtypeStruct((g, t, d), F32), jax.ShapeDtypeStruct((g, t, ROW_WIDTH), F32),
                   jax.ShapeDtypeStruct((g, t, LANES), F32), jax.ShapeDtypeStruct((1, LANES), F32)),
        grid=(g, t // tm),
        in_specs=[rspec(ATTN_WIDTH), rspec(c), prev_spec, prev_spec, rspec(c), rspec(d),
                  pl.BlockSpec((1, r, 6 * d), lambda b, i: (b, 0, 0)),
                  const(cw), const(ga), const(gc), const(bd), const(wout), const(g2), const(wr),
                  const(br), const(tri), const(cnt)],
        out_specs=(rspec(d), rspec(ROW_WIDTH), rspec(LANES), pl.BlockSpec((1, LANES), lambda b, i: (0, 0))),
        scratch_shapes=[pltpu.VMEM((1, LANES), F32)],
        compiler_params=_params("arbitrary", "arbitrary"),
        name="mix_out_router",
    )(attn, cin, prev, prev2, gb, x, mod, cw, ga, gc, bd, wout, g2, wr, br, tri, cnt)


def _dispatch_kernel(n_steps, pos_ref, h_ref, xs_in_ref, xs_ref, buf, sem):
    del xs_in_ref
    tm = h_ref.shape[0]
    i = pl.program_id(0)
    slot = i % 2

    def drain(s):
        pltpu.make_async_copy(buf.at[s], xs_ref.at[pl.ds(0, tm), :], sem.at[s]).wait()

    @pl.when(i >= 2)
    def _():
        drain(slot)

    buf[slot] = h_ref[...]
    base = i * tm
    for r in range(tm):
        pltpu.make_async_copy(buf.at[slot, pl.ds(r, 1), :], xs_ref.at[pl.ds(pos_ref[base + r], 1), :],
                              sem.at[slot]).start(priority=r % 2)

    @pl.when(i == n_steps - 1)
    def _():
        drain(slot)
        if n_steps > 1:
            drain(1 - slot)


def _dispatch(pos, hrow, xs, tm):
    n, w = hrow.shape
    n_steps = n // tm
    return pl.pallas_call(
        functools.partial(_dispatch_kernel, n_steps),
        out_shape=jax.ShapeDtypeStruct(xs.shape, xs.dtype),
        grid_spec=pltpu.PrefetchScalarGridSpec(
            num_scalar_prefetch=1,
            grid=(n_steps,),
            in_specs=[pl.BlockSpec((tm, w), lambda i, pos: (i, 0)),
                      pl.BlockSpec(memory_space=pl.ANY)],
            out_specs=pl.BlockSpec(memory_space=pl.ANY),
            scratch_shapes=[pltpu.VMEM((2, tm, w), F32), pltpu.SemaphoreType.DMA((2,))]),
        input_output_aliases={2: 0},
        compiler_params=_params("arbitrary"),
        name="moe_dispatch",
    )(pos, hrow, xs)


def _experts_kernel(lo_ref, hi_ref, nt_ref, xs_ref, wgl_ref, wul_ref, wdl_ref, wgh_ref, wuh_ref, wdh_ref,
                    o_ref):
    del lo_ref, hi_ref
    live = pl.program_id(0) < nt_ref[0]

    @pl.when(jnp.logical_not(live))
    def _():
        o_ref[...] = jnp.zeros_like(o_ref)

    @pl.when(live)
    def _():
        d = D_MODEL
        x = xs_ref[:, 0:d].astype(BF16)
        w_lo = xs_ref[:, d:d + 1]
        w_hi = xs_ref[:, d + 1:d + 2]

        def hidden(wg_ref, wu_ref, w):
            gt = jnp.dot(x, wg_ref[0], preferred_element_type=F32)
            up = jnp.dot(x, wu_ref[0], preferred_element_type=F32)
            return (gt * jax.nn.sigmoid(gt) * up * w).astype(BF16)

        o_ref[...] = (jnp.dot(hidden(wgl_ref, wul_ref, w_lo), wdl_ref[0], preferred_element_type=F32)
                      + jnp.dot(hidden(wgh_ref, wuh_ref, w_hi), wdh_ref[0], preferred_element_type=F32))


def _experts(tile_lo, tile_hi, n_tiles, xs, wg, wu, wdn):
    rows, w = xs.shape
    d = D_MODEL
    tg = MOE_TILE
    tile = lambda i, lo, hi, nt: (jnp.minimum(i, nt[0] - 1), 0)
    wspec = lambda shp, which: pl.BlockSpec(
        (1,) + shp, (lambda i, lo, hi, nt: (lo[i], 0, 0)) if which == 0 else (lambda i, lo, hi, nt: (hi[i], 0, 0)))
    return pl.pallas_call(
        _experts_kernel,
        out_shape=jax.ShapeDtypeStruct((rows, d), F32),
        grid_spec=pltpu.PrefetchScalarGridSpec(
            num_scalar_prefetch=3,
            grid=(rows // tg,),
            in_specs=[pl.BlockSpec((tg, w), tile),
                      wspec((d, D_EXPERT), 0), wspec((d, D_EXPERT), 0), wspec((D_EXPERT, d), 0),
                      wspec((d, D_EXPERT), 1), wspec((d, D_EXPERT), 1), wspec((D_EXPERT, d), 1)],
            out_specs=pl.BlockSpec((tg, d), lambda i, lo, hi, nt: (i, 0))),
        compiler_params=_params("arbitrary"),
        name="moe_experts",
    )(tile_lo, tile_hi, n_tiles, xs, wg, wu, wdn, wg, wu, wdn)


def _final_kernel(n_tiles, pos_ref, x1_ref, mod_ref, modf_ref, gf_ref, os_ref, y_ref, buf, sem):
    d = D_MODEL
    tm = x1_ref.shape[0]
    i = pl.program_id(0)

    @pl.when(i < n_tiles)
    def _():
        slot = i % 2
        base = i * tm
        for r in range(tm):
            pltpu.make_async_copy(os_ref.at[pl.ds(pos_ref[base + r], 1), :],
                                  buf.at[slot, pl.ds(r, 1), :], sem.at[slot]).start(priority=r % 2)

    @pl.when(i > 0)
    def _():
        slot = (i - 1) % 2
        pltpu.make_async_copy(os_ref.at[pl.ds(0, tm), :], buf.at[slot], sem.at[slot]).wait()
        ga2 = mod_ref[0, :, 5 * d:6 * d]
        x2 = x1_ref[...] + ga2 * buf[slot]
        shf = modf_ref[0, :, 0:d]
        scf = modf_ref[0, :, d:2 * d]
        y_ref[...] = _rms(x2, gf_ref[...]) * (1.0 + scf) + shf


def _final(pos, x1, mod, modf, gf, out_sorted, tm, tiles_per_mod):
    n, d = x1.shape
    r = mod.shape[1]
    n_tiles = n // tm
    done = lambda i: jnp.maximum(i - 1, 0)
    return pl.pallas_call(
        functools.partial(_final_kernel, n_tiles),
        out_shape=jax.ShapeDtypeStruct((n, d), F32),
        grid_spec=pltpu.PrefetchScalarGridSpec(
            num_scalar_prefetch=1,
            grid=(n_tiles + 1,),
            in_specs=[pl.BlockSpec((tm, d), lambda i, pos: (done(i), 0)),
                      pl.BlockSpec((1, r, 6 * d), lambda i, pos: (done(i) // tiles_per_mod, 0, 0)),
                      pl.BlockSpec((1, r, 2 * d), lambda i, pos: (done(i) // tiles_per_mod, 0, 0)),
                      pl.BlockSpec((1, d), lambda i, pos: (0, 0)),
                      pl.BlockSpec(memory_space=pl.ANY)],
            out_specs=pl.BlockSpec((tm, d), lambda i, pos: (done(i), 0)),
            scratch_shapes=[pltpu.VMEM((2, tm, d), F32), pltpu.SemaphoreType.DMA((2,))]),
        compiler_params=_params("arbitrary"),
        name="moe_combine_final",
    )(pos, x1, mod, modf, gf, out_sorted)


def kernel(x_prompt, x_sample, c_prompt, c_sample, cache_k, cache_v, cache_logf, state_conv, page_table, w_ada, b_ada, g_norm1, w_in, b_forget, conv_w, g_attn_out, g_conv_out, w_out, g_norm2, w_router_group, b_router_group, w_router_expert, b_router_expert, w_expert_gate, w_expert_up, w_expert_down, w_ada_final, b_ada_final, g_final):
    d, a, c, h = D_MODEL, ATTN_WIDTH, CONV_WIDTH, N_HEADS
    bsz, t, _ = x_prompt.shape
    dbs = x_sample.shape[0]
    n_prompt = bsz * t
    assert w_ada.shape[0] == 1 and x_sample.shape[1] == 1

    c_all = jnp.concatenate([c_prompt, c_sample], axis=0)
    mod = _ada(c_all, w_ada[0], b_ada[0])
    modf = _ada(c_all, w_ada_final, b_ada_final)
    mod_p, mod_s = mod[:bsz].reshape(bsz, 1, 6 * d), mod[bsz:].reshape(1, dbs, 6 * d)
    modf_p, modf_s = modf[:bsz].reshape(bsz, 1, 2 * d), modf[bsz:].reshape(1, dbs, 2 * d)

    wt = w_in[0].T.astype(BF16)
    wq = wt[0:a].T
    wkt = wt[a:2 * a]
    wv = wt[2 * a:3 * a].T
    wft = jnp.zeros((16, d), BF16).at[0:h].set(wt[3 * a:3 * a + h])
    wc = wt[3 * a + h:].T
    bf = b_forget[0].reshape(h, 1)
    g1 = g_norm1[0].reshape(1, d)
    g2 = g_norm2[0].reshape(1, d)
    gf = g_final.reshape(1, d)
    ga = g_attn_out[0].reshape(1, a)
    gc = g_conv_out[0].reshape(1, c)
    cw = conv_w[0]
    wout = w_out[0].astype(BF16)
    lane_group = jnp.arange(a) // HEAD_DIM
    bd = (lane_group[:, None] == lane_group[None, :]).astype(BF16)
    wr = jnp.zeros((d, LANES), F32)
    wr = wr.at[:, 0:N_EXPERTS].set(w_router_expert[0])
    wr = wr.at[:, ROUTER_GROUP_LANE0:ROUTER_GROUP_LANE0 + N_GROUPS].set(w_router_group[0])
    wrh = wr.astype(BF16)
    wr = jnp.concatenate([wrh, (wr - wrh.astype(F32)).astype(BF16)], axis=1)
    br = jnp.zeros((1, LANES), F32)
    br = br.at[0, 0:N_EXPERTS].set(b_router_expert[0])
    br = br.at[0, ROUTER_GROUP_LANE0:ROUTER_GROUP_LANE0 + N_GROUPS].set(b_router_group[0])
    wg = w_expert_gate[0].astype(BF16)
    wu = w_expert_up[0].astype(BF16)
    wdn = w_expert_down[0].astype(BF16)
    tri = lambda m: (jnp.arange(m)[:, None] >= jnp.arange(m)[None, :]).astype(BF16)

    q_p, kt_p, vt_p, ktb_p, vb_p, lft_p, cin_p, gb_p = _inproj(
        True, x_prompt, mod_p, g1, wq, wkt, wv, wft, bf, wc, TOKEN_TILE)
    attn_p = _attn_prompt(q_p, ktb_p, _cumsum(lft_p), vb_p)
    x1_p, hrow_p, info_p, cnt_p = _merge(
        False, attn_p, cin_p, cin_p, cin_p, gb_p, x_prompt, mod_p, cw, ga, gc, bd, wout, g2, wr, br,
        tri(TOKEN_TILE), jnp.zeros((1, LANES), F32), TOKEN_TILE)

    xs_ = x_sample.reshape(1, dbs, d)
    q_s, k_s, v_s, lft_s, cin_s, gb_s = _inproj(False, xs_, mod_s, g1, wq, wkt, wv, wft, bf, wc, dbs)
    head_of_lane = jnp.arange(a) // HEAD_DIM
    qbd = jnp.where(head_of_lane[None, None, :] == jnp.arange(h)[None, :, None],
                    q_s.reshape(dbs, 1, a), jnp.zeros((), BF16))
    n_pool = cache_k.shape[1]
    page_major = lambda z: z[0].transpose(0, 2, 3, 1).reshape(n_pool, a, PAGE_SIZE)
    attn_s = _attn_sample(
        page_table, qbd, k_s.reshape(dbs, 1, a), v_s.reshape(dbs, 1, a),
        lft_s[0].T.reshape(dbs, h, 1),
        page_major(cache_k), page_major(cache_v), cache_logf[0].transpose(0, 2, 1))
    st = state_conv[0]
    x1_s, hrow_s, info_s, cnt_all = _merge(
        True, attn_s.reshape(1, dbs, a), cin_s, st[:, 1][None], st[:, 0][None], gb_s, xs_, mod_s, cw, ga,
        gc, bd, wout, g2, wr, br, tri(dbs), cnt_p, dbs)

    n_rows_max = n_prompt + dbs + N_BUCKETS * (MOE_TILE - 1)
    max_tiles = -(-n_rows_max // MOE_TILE)
    counts = cnt_all[0, 0:N_BUCKETS].astype(I32)
    padded = (counts + (MOE_TILE - 1)) // MOE_TILE * MOE_TILE
    ends = jnp.cumsum(padded)
    offsets = ends - padded
    n_tiles = (ends[-1] // MOE_TILE).astype(I32)
    tile_ids = jnp.minimum(jnp.arange(max_tiles, dtype=I32), n_tiles - 1)
    tile_bucket = jnp.sum(tile_ids[:, None] >= (ends // MOE_TILE)[None, :], axis=1)
    lo_tab, hi_tab = _pair_tables()
    tile_lo = jnp.asarray(lo_tab)[tile_bucket]
    tile_hi = jnp.asarray(hi_tab)[tile_bucket]
    bucket_ids = jnp.arange(N_BUCKETS, dtype=F32)

    def position(info):
        bucket, rank = info[..., 0:1], info[..., 1].astype(I32)
        base = jnp.sum(jnp.where(bucket == bucket_ids, offsets, 0), axis=-1)
        return (base + rank).reshape(-1)

    pos_p, pos_s = position(info_p), position(info_s)

    xs = jnp.zeros((max_tiles * MOE_TILE, ROW_WIDTH), F32)
    xs = _dispatch(pos_p, hrow_p.reshape(n_prompt, ROW_WIDTH), xs, TOKEN_TILE)
    xs = _dispatch(pos_s, hrow_s.reshape(dbs, ROW_WIDTH), xs, dbs)
    out_sorted = _experts(tile_lo, tile_hi, n_tiles.reshape(1), xs, wg, wu, wdn)
    y_p = _final(pos_p, x1_p.reshape(n_prompt, d), mod_p, modf_p, gf, out_sorted, TOKEN_TILE,
                 t // TOKEN_TILE)
    y_s = _final(pos_s, x1_s.reshape(dbs, d), mod_s, modf_s, gf, out_sorted, dbs, 1)

    kv_p = lambda zt: zt.reshape(bsz, h, HEAD_DIM, t).transpose(0, 3, 1, 2)[None]
    kv_s = lambda z: z.reshape(1, dbs, 1, h, HEAD_DIM)
    return (y_p.reshape(bsz, t, d), y_s.reshape(dbs, 1, d),
            kv_p(kt_p), kv_p(vt_p),
            lft_p.transpose(0, 2, 1)[None],
            cin_p[:, t - (CONV_K - 1):, :][None],
            kv_s(k_s), kv_s(v_s),
            lft_s[0].T.reshape(1, dbs, 1, h),
            jnp.stack([st[:, 1], cin_s[0]], axis=1)[None])
```

```python
import functools

import numpy as np
import jax
import jax.numpy as jnp
from jax import lax
from jax.experimental import pallas as pl
from jax.experimental.pallas import tpu as pltpu

F32 = jnp.float32
BF16 = jnp.bfloat16
I32 = jnp.int32

D_MODEL = 1024
N_HEADS = 8
HEAD_DIM = 64
ATTN_WIDTH = N_HEADS * HEAD_DIM
CONV_WIDTH = D_MODEL - ATTN_WIDTH
CONV_K = 3
N_GROUPS = 4
EXPERTS_PER_GROUP = 8
N_EXPERTS = N_GROUPS * EXPERTS_PER_GROUP
D_EXPERT = 256
PAGE_SIZE = 128
EPS = 1e-6

LANES = 128
ROUTER_GROUP_LANE0 = 32
VMEM_LIMIT = 56 * 1024 * 1024
TOKEN_TILE = 512
ATTN_TILE = 512
PAGES_PER_STEP = 64
PV_SLOTS = 8

PAIRS_PER_GROUP = EXPERTS_PER_GROUP * (EXPERTS_PER_GROUP - 1) // 2
N_BUCKETS = N_GROUPS * PAIRS_PER_GROUP
MOE_TILE = 256
ROW_WIDTH = D_MODEL + LANES
LOG2E = 1.4426950408889634


def _pair_tables():
    lo, hi = [], []
    for g in range(N_GROUPS):
        for a in range(EXPERTS_PER_GROUP):
            for b in range(a + 1, EXPERTS_PER_GROUP):
                lo.append(g * EXPERTS_PER_GROUP + a)
                hi.append(g * EXPERTS_PER_GROUP + b)
    return np.asarray(lo, np.int32), np.asarray(hi, np.int32)


def _params(*sem):
    return pltpu.CompilerParams(dimension_semantics=sem, vmem_limit_bytes=VMEM_LIMIT)


def _rms(x, g):
    return x * lax.rsqrt(jnp.mean(x * x, axis=-1, keepdims=True) + EPS) * g


def _split3(a):
    hi = a.astype(BF16)
    r = a - hi.astype(F32)
    mid = r.astype(BF16)
    lo = (r - mid.astype(F32)).astype(BF16)
    return hi, mid, lo


def _dot3(a, b_exact):
    hi, mid, lo = _split3(a)
    d = lambda t: jnp.dot(t, b_exact, preferred_element_type=F32)
    return d(hi) + d(mid) + d(lo)


def _dot_nt(a, b):
    return lax.dot_general(a, b, (((1,), (1,)), ((), ())), preferred_element_type=F32)


def _ada_kernel(c_ref, w_ref, b_ref, o_ref):
    c = c_ref[...]
    s = (c * jax.nn.sigmoid(c)).astype(BF16)
    o_ref[...] = jnp.dot(s, w_ref[...].astype(BF16), preferred_element_type=F32) + b_ref[...]


def _ada(c, w, b):
    m, d = c.shape
    n = w.shape[1]
    tn = 1024
    return pl.pallas_call(
        _ada_kernel,
        out_shape=jax.ShapeDtypeStruct((m, n), F32),
        grid=(n // tn,),
        in_specs=[pl.BlockSpec((m, d), lambda j: (0, 0)),
                  pl.BlockSpec((d, tn), lambda j: (0, j)),
                  pl.BlockSpec((1, tn), lambda j: (0, j))],
        out_specs=pl.BlockSpec((m, tn), lambda j: (0, j)),
        compiler_params=_params("arbitrary"),
        name="adaln_mod",
    )(c, w, b.reshape(1, n))


def _spread_heads(x, extra_ones):
    rows = x.shape[0]
    lane = lax.broadcasted_iota(jnp.int32, (rows, LANES), 1)
    low = lane < HEAD_DIM
    tiles = []
    for pair in range(N_HEADS // 2):
        tile = x[:, pair * LANES:(pair + 1) * LANES]
        for hh, vals in enumerate((tile, pltpu.roll(tile, HEAD_DIM, 1))):
            ones = functools.reduce(jnp.logical_or, [lane == o for o in extra_ones(2 * pair + hh)])
            tiles.append(jnp.where(low, vals, jnp.where(ones, 1.0, 0.0)).astype(BF16))
    return jnp.concatenate(tiles, axis=-1)


def _inproj_kernel(prompt, x_ref, mod_ref, g1_ref, wq_ref, wkt_ref, wv_ref, wft_ref, bf_ref,
                   wc_ref, *outs):
    c, d = CONV_WIDTH, D_MODEL
    x = x_ref[0]
    sh1 = mod_ref[0, :, 0:d]
    sc1 = mod_ref[0, :, d:2 * d]
    h = _rms(x, g1_ref[...]) * (1.0 + sc1) + sh1
    hb = h.astype(BF16)
    q = jnp.dot(hb, wq_ref[...], preferred_element_type=F32)
    if prompt:
        q_ref, kt32_ref, vt32_ref, ktb_ref, vb_ref, lft_ref, cin_ref, gb_ref = outs
        kt = _dot_nt(wkt_ref[...], hb)
        v = jnp.dot(hb, wv_ref[...], preferred_element_type=F32)
        kt32_ref[0] = kt
        vt32_ref[0] = v.T
        ktb_ref[0] = kt.astype(BF16)
        q_ref[0] = _spread_heads(q * (HEAD_DIM ** -0.5 * LOG2E),
                                 lambda hd: [HEAD_DIM + part * N_HEADS + hd for part in range(3)])
        vb_ref[0] = _spread_heads(v, lambda hd: [HEAD_DIM])
    else:
        q_ref, k32_ref, v32_ref, lft_ref, cin_ref, gb_ref = outs
        k32_ref[0] = _dot_nt(hb, wkt_ref[...])
        v32_ref[0] = jnp.dot(hb, wv_ref[...], preferred_element_type=F32)
        q_ref[0] = (q * (HEAD_DIM ** -0.5)).astype(BF16)
    z = _dot_nt(wft_ref[...], hb)[0:N_HEADS, :] + bf_ref[...]
    lft_ref[0] = jnp.minimum(z, 0.0) - jnp.log1p(jnp.exp(-jnp.abs(z)))
    cv = jnp.dot(hb, wc_ref[...], preferred_element_type=F32)
    cin_ref[0] = cv[:, 2 * c:3 * c] * cv[:, 0:c]
    gb_ref[0] = cv[:, c:2 * c]


def _inproj(prompt, x, mod, g1, wq, wkt, wv, wft, bf, wc, tm):
    g, t, d = x.shape
    r = mod.shape[1]
    a, c = ATTN_WIDTH, CONV_WIDTH
    row = lambda w, dt: jax.ShapeDtypeStruct((g, t, w), dt)
    col = lambda w, dt: jax.ShapeDtypeStruct((g, w, t), dt)
    rspec = lambda w: pl.BlockSpec((1, tm, w), lambda b, i: (b, i, 0))
    cspec = lambda w: pl.BlockSpec((1, w, tm), lambda b, i: (b, 0, i))
    const = lambda arr: pl.BlockSpec(arr.shape, lambda b, i: (0,) * arr.ndim)
    if prompt:
        out_shape = (row(2 * a, BF16), col(a, F32), col(a, F32), col(a, BF16), row(2 * a, BF16),
                     col(N_HEADS, F32), row(c, F32), row(c, F32))
        out_specs = (rspec(2 * a), cspec(a), cspec(a), cspec(a), rspec(2 * a), cspec(N_HEADS), rspec(c),
                     rspec(c))
    else:
        out_shape = (row(a, BF16), row(a, F32), row(a, F32), col(N_HEADS, F32), row(c, F32), row(c, F32))
        out_specs = (rspec(a), rspec(a), rspec(a), cspec(N_HEADS), rspec(c), rspec(c))
    return pl.pallas_call(
        functools.partial(_inproj_kernel, prompt),
        out_shape=out_shape,
        grid=(g, t // tm),
        in_specs=[rspec(d),
                  pl.BlockSpec((1, r, 6 * d), lambda b, i: (b, 0, 0)),
                  const(g1), const(wq), const(wkt), const(wv), const(wft), const(bf),
                  const(wc)],
        out_specs=out_specs,
        compiler_params=_params("arbitrary", "arbitrary"),
        name="norm_inproj",
    )(x, mod, g1, wq, wkt, wv, wft, bf, wc)


def _cumsum_kernel(lf_ref, u_ref, o_ref):
    t = lf_ref.shape[2]
    tc = u_ref.shape[0]
    carry = jnp.zeros((N_HEADS, 1), F32)
    pad = jnp.zeros((HEAD_DIM - 3 * N_HEADS, tc), F32)
    for i in range(t // tc):
        blk = lf_ref[0, :, i * tc:(i + 1) * tc]
        cs = _dot3(blk, u_ref[...]) + carry
        carry = cs[:, tc - 1:tc]
        parts = [p.astype(F32) for p in _split3(cs * (-LOG2E))]
        o_ref[0, :, i * tc:(i + 1) * tc] = jnp.concatenate(parts + [pad], axis=0).astype(BF16)


def _cumsum(lft):
    g, h, t = lft.shape
    tc = 512
    idx = jnp.arange(tc)
    upper = (idx[:, None] <= idx[None, :]).astype(BF16)
    return pl.pallas_call(
        _cumsum_kernel,
        out_shape=jax.ShapeDtypeStruct((g, HEAD_DIM, t), BF16),
        grid=(g,),
        in_specs=[pl.BlockSpec((1, h, t), lambda b: (b, 0, 0)),
                  pl.BlockSpec((tc, tc), lambda b: (0, 0))],
        out_specs=pl.BlockSpec((1, HEAD_DIM, t), lambda b: (b, 0, 0)),
        compiler_params=_params("arbitrary"),
        name="logf_cumsum",
    )(lft, upper)


def _attn_prompt_kernel(q_ref, kt_ref, kb_ref, v_ref, wg_ref, wu_ref, wd_ref,
                        o_ref, xs_ref, wgb_ref, wub_ref, wdb_ref):
    xs_ref[...] = jnp.zeros_like(xs_ref)
    wgb_ref[...] = wg_ref[...].astype(BF16)
    wub_ref[...] = wu_ref[...].astype(BF16)
    wdb_ref[...] = wd_ref[...].astype(BF16)
    t = q_ref.shape[1]
    ta = ATTN_TILE
    n = t // ta
    row = lax.broadcasted_iota(jnp.int32, (ta, ta), 0)
    col = lax.broadcasted_iota(jnp.int32, (ta, ta), 1)
    causal = col <= row
    kb = kb_ref[0]
    kt_aug = [jnp.concatenate([kt_ref[0, hh * HEAD_DIM:(hh + 1) * HEAD_DIM, :], kb], axis=0)
              for hh in range(2)]
    for i in range(n):
        outs = []
        for hh in range(2):
            ls = slice(hh * LANES, (hh + 1) * LANES)
            qa = q_ref[0, i * ta:(i + 1) * ta, ls]
            m = jnp.full((ta, 1), -jnp.inf, F32)
            acc = jnp.zeros((ta, LANES), F32)
            for k0, k1 in ([(0, i * ta)] if i else []) + [(i * ta, (i + 1) * ta)]:
                s = jnp.dot(qa, kt_aug[hh][:, k0:k1], preferred_element_type=F32)
                if k0 == i * ta:
                    s = jnp.where(causal, s, -jnp.inf)
                m_new = jnp.maximum(m, jnp.max(s, axis=-1, keepdims=True))
                p = jnp.exp2(s - m_new).astype(BF16)
                acc = jnp.exp2(m - m_new) * acc + jnp.dot(p, v_ref[0, k0:k1, ls],
                                                          preferred_element_type=F32)
                m = m_new
            outs.append(acc[:, 0:HEAD_DIM] / acc[:, HEAD_DIM:HEAD_DIM + 1])
        o_ref[0, i * ta:(i + 1) * ta, :] = jnp.concatenate(outs, axis=-1)


def _attn_prompt(q_aug, ktb, kbias, v_aug, wg, wu, wd, xs_rows):
    g, t, _ = q_aug.shape
    a = ATTN_WIDTH
    pairs = a // LANES
    n_steps = g * pairs
    wide = pl.BlockSpec((1, t, 2 * LANES), lambda b, p: (b, 0, p))
    slab = lambda arr: pl.BlockSpec((arr.shape[0] // n_steps, arr.shape[1]), lambda b, p: (b * pairs + p, 0))
    flat = [w.reshape(-1, w.shape[-1]) for w in (wg, wu, wd)]
    xs_shape = jax.ShapeDtypeStruct((xs_rows, ROW_WIDTH), F32)
    assert all(arr.shape[0] % (8 * n_steps) == 0 for arr in flat + [xs_shape])
    outs = pl.pallas_call(
        _attn_prompt_kernel,
        out_shape=(jax.ShapeDtypeStruct((g, t, a), F32), xs_shape,
                   *[jax.ShapeDtypeStruct(w.shape, BF16) for w in flat]),
        grid=(g, pairs),
        in_specs=[wide, pl.BlockSpec((1, LANES, t), lambda b, p: (b, p, 0)),
                  pl.BlockSpec((1, HEAD_DIM, t), lambda b, p: (b, 0, 0)), wide,
                  *[slab(w) for w in flat]],
        out_specs=(pl.BlockSpec((1, t, LANES), lambda b, p: (b, 0, p)), slab(xs_shape),
                   *[slab(w) for w in flat]),
        compiler_params=_params("arbitrary", "arbitrary"),
        name="fox_prompt_attn",
    )(q_aug, ktb, kbias, v_aug, *flat)
    attn, xs, wgb, wub, wdb = outs
    return attn, xs, wgb.reshape(wg.shape), wub.reshape(wu.shape), wdb.reshape(wd.shape)


def _scores_sample_kernel(pt_ref, qbd_ref, kn_ref, fn_ref, lmat_ref, *rest):
    np_ = PAGES_PER_STEP
    kt_refs, f_refs = rest[0:np_], rest[np_:2 * np_]
    p_ref, pn_ref, l_ref, need_ref, m_sc, snew_sc, carry_sc, pmax_sc = rest[2 * np_:]
    step = pl.program_id(1)
    qbd = qbd_ref[0]

    @pl.when(step == 0)
    def _():
        kn = kn_ref[0].astype(BF16).astype(F32)
        s_new = jnp.sum(qbd.astype(F32) * kn, axis=-1, keepdims=True)
        snew_sc[...] = s_new
        m_sc[...] = s_new
        carry_sc[...] = fn_ref[0]
        pmax_sc[...] = jnp.full(pmax_sc.shape, -jnp.inf, F32)

    f_all = jnp.concatenate([f_refs[i][0] for i in range(np_)], axis=0)
    suffix = _dot3(f_all, lmat_ref[...])
    totals = jnp.sum(f_all, axis=-1, keepdims=True)
    carry = carry_sc[...]
    lane = lax.broadcasted_iota(jnp.int32, pmax_sc.shape, 1)
    pmax = pmax_sc[...]
    for i in range(np_):
        sl = slice(i * N_HEADS, (i + 1) * N_HEADS)
        order = step * np_ + i
        s = jnp.dot(qbd, kt_refs[i][0].astype(BF16), preferred_element_type=F32) + carry + suffix[sl]
        p_ref[0, order] = s
        pmax = jnp.where(lane == order, jnp.max(s, axis=-1, keepdims=True), pmax)
        carry = carry + totals[sl]
    carry_sc[...] = carry
    pmax_sc[...] = pmax
    m_sc[...] = jnp.maximum(m_sc[...], jnp.max(pmax, axis=-1, keepdims=True))

    @pl.when(step == pl.num_programs(1) - 1)
    def _():
        m = m_sc[...]
        p = jnp.exp(p_ref[0] - m[None])
        p_ref[0] = p
        pn = jnp.exp(snew_sc[...] - m)
        pn_ref[0] = pn
        l_ref[0] = jnp.sum(jnp.sum(p, axis=0), axis=-1, keepdims=True) + pn
        top = jnp.exp(pmax_sc[...] - m).astype(BF16).astype(F32)
        need_ref[0] = jnp.max(jnp.where(top != 0.0, 1.0, 0.0), axis=0, keepdims=True)


def _pv_sample_kernel(pages_ref, count_ref, p_ref, pn_ref, l_ref, vn_ref, vt_hbm, o_ref, vbuf, sem):
    b = pl.program_id(0)
    n = count_ref[b]
    base = b * p_ref.shape[1]

    def fetch(order, slot):
        return pltpu.make_async_copy(vt_hbm.at[pages_ref[base + order]], vbuf.at[slot], sem.at[slot])

    for slot in range(PV_SLOTS):
        @pl.when(slot < n)
        def _(slot=slot):
            fetch(slot, slot).start()

    def body(order, acc):
        slot = order % PV_SLOTS
        fetch(order, slot).wait()
        acc = acc + _dot_nt(p_ref[0, order].astype(BF16), vbuf[slot].astype(BF16))

        @pl.when(order + PV_SLOTS < n)
        def _():
            fetch(order + PV_SLOTS, slot).start()

        return acc

    acc = lax.fori_loop(0, n, body,
                        pn_ref[0].astype(BF16).astype(F32) * vn_ref[0].astype(BF16).astype(F32))
    res = acc / l_ref[0]
    head = lax.broadcasted_iota(jnp.int32, res.shape, 0)
    lane_head = lax.broadcasted_iota(jnp.int32, res.shape, 1) // HEAD_DIM
    o_ref[0] = jnp.sum(jnp.where(head == lane_head, res, 0.0), axis=0, keepdims=True)


def _attn_sample(page_table, qbd, k_new, v_new, f_new, cache_kt, cache_vt, cache_ft):
    bsz, n_pages = page_table.shape
    a, h = ATTN_WIDTH, N_HEADS
    np_ = PAGES_PER_STEP
    n_steps = n_pages // np_
    idx = jnp.arange(PAGE_SIZE)
    lmat = (idx[:, None] > idx[None, :]).astype(BF16)
    newest_first = page_table[:, ::-1].reshape(-1)
    per_b = lambda shp: pl.BlockSpec(shp, lambda b, s, *_: (b,) + (0,) * (len(shp) - 1))

    def page_spec(shape, i):
        return pl.BlockSpec(shape, lambda b, s, pt, *_: (pt[(b * n_steps + s) * np_ + i], 0, 0))

    p, pn, l, need = pl.pallas_call(
        _scores_sample_kernel,
        out_shape=(jax.ShapeDtypeStruct((bsz, n_pages, h, PAGE_SIZE), F32),
                   jax.ShapeDtypeStruct((bsz, h, 1), F32), jax.ShapeDtypeStruct((bsz, h, 1), F32),
                   jax.ShapeDtypeStruct((bsz, 1, n_pages), F32)),
        grid_spec=pltpu.PrefetchScalarGridSpec(
            num_scalar_prefetch=1,
            grid=(bsz, n_steps),
            in_specs=([per_b((1, h, a)), per_b((1, 1, a)), per_b((1, h, 1)),
                       pl.BlockSpec((PAGE_SIZE, PAGE_SIZE), lambda b, s, pt: (0, 0))]
                      + [page_spec((1, a, PAGE_SIZE), i) for i in range(np_)]
                      + [page_spec((1, h, PAGE_SIZE), i) for i in range(np_)]),
            out_specs=(per_b((1, n_pages, h, PAGE_SIZE)), per_b((1, h, 1)), per_b((1, h, 1)),
                       per_b((1, 1, n_pages))),
            scratch_shapes=[pltpu.VMEM((h, 1), F32), pltpu.VMEM((h, 1), F32), pltpu.VMEM((h, 1), F32),
                            pltpu.VMEM((h, n_pages), F32)]),
        compiler_params=_params("arbitrary", "arbitrary"),
        name="fox_sample_scores",
    )(newest_first, qbd, k_new, f_new, lmat, *([cache_kt] * np_), *([cache_ft] * np_))

    order = jnp.arange(1, n_pages + 1, dtype=I32)
    count = jnp.max(jnp.where(need[:, 0, :] > 0, order, 0), axis=-1)

    per_seq = lambda shp: pl.BlockSpec(shp, lambda b, *_: (b,) + (0,) * (len(shp) - 1))
    return pl.pallas_call(
        _pv_sample_kernel,
        out_shape=jax.ShapeDtypeStruct((bsz, 1, a), F32),
        grid_spec=pltpu.PrefetchScalarGridSpec(
            num_scalar_prefetch=2,
            grid=(bsz,),
            in_specs=[per_seq((1, n_pages, h, PAGE_SIZE)), per_seq((1, h, 1)), per_seq((1, h, 1)),
                      per_seq((1, 1, a)), pl.BlockSpec(memory_space=pl.ANY)],
            out_specs=per_seq((1, 1, a)),
            scratch_shapes=[pltpu.VMEM((PV_SLOTS, a, PAGE_SIZE), F32),
                            pltpu.SemaphoreType.DMA((PV_SLOTS,))]),
        compiler_params=_params("arbitrary"),
        name="fox_sample_pv",
    )(newest_first, count, p, pn, l, v_new, cache_vt)


def _group_rms(x, bd, g):
    ms = jnp.dot((x * x).astype(BF16), bd, preferred_element_type=F32)
    return x * lax.rsqrt(ms * (1.0 / HEAD_DIM) + EPS) * g


def _route(logits):
    lane = lax.broadcasted_iota(jnp.int32, logits.shape, 1).astype(F32)
    big = jnp.float32(1e9)
    g0 = float(ROUTER_GROUP_LANE0)
    epg = float(EXPERTS_PER_GROUP)
    glog = jnp.where((lane >= g0) & (lane < g0 + N_GROUPS), logits, -jnp.inf)
    gmax = jnp.max(glog, axis=-1, keepdims=True)
    gidx = jnp.min(jnp.where(glog == gmax, lane, big), axis=-1, keepdims=True) - g0
    gp = 1.0 / jnp.sum(jnp.exp(glog - gmax), axis=-1, keepdims=True)
    lo = gidx * epg
    el = jnp.where((lane >= lo) & (lane < lo + epg), logits, -jnp.inf)
    m1 = jnp.max(el, axis=-1, keepdims=True)
    i1 = jnp.min(jnp.where(el == m1, lane, big), axis=-1, keepdims=True)
    el2 = jnp.where(lane == i1, -jnp.inf, el)
    m2 = jnp.max(el2, axis=-1, keepdims=True)
    i2 = jnp.min(jnp.where(el2 == m2, lane, big), axis=-1, keepdims=True)
    t = jnp.exp(m2 - m1)
    w1 = gp / (1.0 + t)
    w2 = w1 * t
    first_low = i1 < i2
    ea = jnp.minimum(i1, i2) - lo
    eb = jnp.maximum(i1, i2) - lo
    pair = ea * (2.0 * epg - 1.0 - ea) * 0.5 + (eb - ea - 1.0)
    bucket = gidx * float(PAIRS_PER_GROUP) + pair
    return bucket, jnp.where(first_low, w1, w2), jnp.where(first_low, w2, w1)


def _merge_kernel(sample, attn_ref, cin_ref, prev_ref, prev2_ref, gb_ref, x_ref, mod_ref, cw_ref,
                  ga_ref, gc_ref, bd_ref, wout_ref, g2_ref, wr_ref, br_ref, tri_ref, cnt_ref,
                  x1_ref, hrow_ref, info_ref, cnto_ref, cnt_sc):
    d, a = D_MODEL, ATTN_WIDTH
    cin = cin_ref[0]
    tm = cin.shape[0]
    first_step = (pl.program_id(0) == 0) & (pl.program_id(1) == 0)

    @pl.when(first_step)
    def _():
        cnt_sc[...] = cnt_ref[...]

    w0, w1, w2 = cw_ref[0:1, :], cw_ref[1:2, :], cw_ref[2:3, :]
    if sample:
        conv_y = w0 * prev2_ref[0] + w1 * prev_ref[0] + w2 * cin
    else:
        first = pl.program_id(1) == 0
        tail = jnp.where(first, 0.0, prev_ref[0])
        ext = jnp.concatenate([tail, cin], axis=0)
        conv_y = w0 * ext[6:6 + tm] + w1 * ext[7:7 + tm] + w2 * cin
    bd = bd_ref[...]
    an = _group_rms(attn_ref[0], bd, ga_ref[...]).astype(BF16)
    cn = _group_rms(gb_ref[0] * conv_y, bd, gc_ref[...]).astype(BF16)
    mix = (jnp.dot(an, wout_ref[0:a, :], preferred_element_type=F32)
           + jnp.dot(cn, wout_ref[a:, :], preferred_element_type=F32))
    ga1 = mod_ref[0, :, 2 * d:3 * d]
    sh2 = mod_ref[0, :, 3 * d:4 * d]
    sc2 = mod_ref[0, :, 4 * d:5 * d]
    x1 = x_ref[0] + ga1 * mix
    x1_ref[0] = x1
    h2 = _rms(x1, g2_ref[...]) * (1.0 + sc2) + sh2
    hb = h2.astype(BF16)
    hl = (h2 - hb.astype(F32)).astype(BF16)
    both = jnp.dot(hb, wr_ref[...], preferred_element_type=F32)
    logits = (both[:, 0:LANES] + both[:, LANES:]
              + jnp.dot(hl, wr_ref[:, 0:LANES], preferred_element_type=F32)) + br_ref[...]
    bucket, w_lo, w_hi = _route(logits)

    lane = lax.broadcasted_iota(jnp.int32, (tm, LANES), 1)
    lanef = lane.astype(F32)
    onehot = lanef == bucket
    incl = jnp.dot(tri_ref[...], jnp.where(onehot, 1.0, 0.0).astype(BF16), preferred_element_type=F32)
    cnt = cnt_sc[...]
    rank = jnp.sum(jnp.where(onehot, incl - 1.0 + cnt, 0.0), axis=-1, keepdims=True)
    cnt_new = cnt + incl[tm - 1:tm, :]
    cnt_sc[...] = cnt_new
    cnto_ref[...] = cnt_new
    info_ref[0] = jnp.where(lane == 0, bucket, jnp.where(lane == 1, rank, 0.0))

    hrow_ref[0, :, 0:d] = h2
    hrow_ref[0, :, d:] = jnp.where(lane == 0, w_lo, jnp.where(lane == 1, w_hi, 0.0))


def _merge(sample, attn, cin, prev, prev2, gb, x, mod, cw, ga, gc, bd, wout, g2, wr, br, tri, cnt, tm):
    g, t, d = x.shape
    r = mod.shape[1]
    c = CONV_WIDTH
    rspec = lambda w: pl.BlockSpec((1, tm, w), lambda b, i: (b, i, 0))
    const = lambda arr: pl.BlockSpec(arr.shape, lambda b, i: (0,) * arr.ndim)
    if sample:
        prev_spec = rspec(c)
    else:
        prev_spec = pl.BlockSpec((1, 8, c), lambda b, i: (b, jnp.maximum(i * (tm // 8) - 1, 0), 0))
    return pl.pallas_call(
        functools.partial(_merge_kernel, sample),
        out_shape=(jax.ShapeDtypeStruct((g, t, d), F32), jax.ShapeDtypeStruct((g, t, ROW_WIDTH), F32),
                   jax.ShapeDtypeStruct((g, t, LANES), F32), jax.ShapeDtypeStruct((1, LANES), F32)),
        grid=(g, t // tm),
        in_specs=[rspec(ATTN_WIDTH), rspec(c), prev_spec, prev_spec, rspec(c), rspec(d),
                  pl.BlockSpec((1, r, 6 * d), lambda b, i: (b, 0, 0)),
                  const(cw), const(ga), const(gc), const(bd), const(wout), const(g2), const(wr),
                  const(br), const(tri), const(cnt)],
        out_specs=(rspec(d), rspec(ROW_WIDTH), rspec(LANES), pl.BlockSpec((1, LANES), lambda b, i: (0, 0))),
        scratch_shapes=[pltpu.VMEM((1, LANES), F32)],
        compiler_params=_params("arbitrary", "arbitrary"),
        name="mix_out_router",
    )(attn, cin, prev, prev2, gb, x, mod, cw, ga, gc, bd, wout, g2, wr, br, tri, cnt)


def _dispatch_kernel(n_steps, pos_ref, h_ref, xs_in_ref, xs_ref, buf, sem):
    del xs_in_ref
    tm = h_ref.shape[0]
    i = pl.program_id(0)
    slot = i % 2

    def drain(s):
        pltpu.make_async_copy(buf.at[s], xs_ref.at[pl.ds(0, tm), :], sem.at[s]).wait()

    @pl.when(i >= 2)
    def _():
        drain(slot)

    buf[slot] = h_ref[...]
    base = i * tm
    for r in range(tm):
        pltpu.make_async_copy(buf.at[slot, pl.ds(r, 1), :], xs_ref.at[pl.ds(pos_ref[base + r], 1), :],
                              sem.at[slot]).start(priority=r % 2)

    @pl.when(i == n_steps - 1)
    def _():
        drain(slot)
        if n_steps > 1:
            drain(1 - slot)


def _dispatch(pos, hrow, xs, tm):
    n, w = hrow.shape
    n_steps = n // tm
    return pl.pallas_call(
        functools.partial(_dispatch_kernel, n_steps),
        out_shape=jax.ShapeDtypeStruct(xs.shape, xs.dtype),
        grid_spec=pltpu.PrefetchScalarGridSpec(
            num_scalar_prefetch=1,
            grid=(n_steps,),
            in_specs=[pl.BlockSpec((tm, w), lambda i, pos: (i, 0)),
                      pl.BlockSpec(memory_space=pl.ANY)],
            out_specs=pl.BlockSpec(memory_space=pl.ANY),
            scratch_shapes=[pltpu.VMEM((2, tm, w), F32), pltpu.SemaphoreType.DMA((2,))]),
        input_output_aliases={2: 0},
        compiler_params=_params("arbitrary"),
        name="moe_dispatch",
    )(pos, hrow, xs)


def _experts_kernel(lo_ref, hi_ref, nt_ref, xs_ref, wgl_ref, wul_ref, wdl_ref, wgh_ref, wuh_ref, wdh_ref,
                    o_ref):
    del lo_ref, hi_ref
    live = pl.program_id(0) < nt_ref[0]

    @pl.when(jnp.logical_not(live))
    def _():
        o_ref[...] = jnp.zeros_like(o_ref)

    @pl.when(live)
    def _():
        d = D_MODEL
        x = xs_ref[:, 0:d].astype(BF16)
        w_lo = xs_ref[:, d:d + 1]
        w_hi = xs_ref[:, d + 1:d + 2]

        def hidden(wg_ref, wu_ref, w):
            gt = jnp.dot(x, wg_ref[0], preferred_element_type=F32)
            up = jnp.dot(x, wu_ref[0], preferred_element_type=F32)
            return (gt * jax.nn.sigmoid(gt) * up * w).astype(BF16)

        o_ref[...] = (jnp.dot(hidden(wgl_ref, wul_ref, w_lo), wdl_ref[0], preferred_element_type=F32)
                      + jnp.dot(hidden(wgh_ref, wuh_ref, w_hi), wdh_ref[0], preferred_element_type=F32))


def _experts(tile_lo, tile_hi, n_tiles, xs, wg, wu, wdn):
    rows, w = xs.shape
    d = D_MODEL
    tg = MOE_TILE
    tile = lambda i, lo, hi, nt: (jnp.minimum(i, nt[0] - 1), 0)
    wspec = lambda shp, which: pl.BlockSpec(
        (1,) + shp, (lambda i, lo, hi, nt: (lo[i], 0, 0)) if which == 0 else (lambda i, lo, hi, nt: (hi[i], 0, 0)))
    return pl.pallas_call(
        _experts_kernel,
        out_shape=jax.ShapeDtypeStruct((rows, d), F32),
        grid_spec=pltpu.PrefetchScalarGridSpec(
            num_scalar_prefetch=3,
            grid=(rows // tg,),
            in_specs=[pl.BlockSpec((tg, w), tile),
                      wspec((d, D_EXPERT), 0), wspec((d, D_EXPERT), 0), wspec((D_EXPERT, d), 0),
                      wspec((d, D_EXPERT), 1), wspec((d, D_EXPERT), 1), wspec((D_EXPERT, d), 1)],
            out_specs=pl.BlockSpec((tg, d), lambda i, lo, hi, nt: (i, 0))),
        compiler_params=_params("arbitrary"),
        name="moe_experts",
    )(tile_lo, tile_hi, n_tiles, xs, wg, wu, wdn, wg, wu, wdn)


def _final_kernel(n_tiles, pos_ref, x1_ref, mod_ref, modf_ref, gf_ref, os_ref, y_ref, buf, sem):
    d = D_MODEL
    tm = x1_ref.shape[0]
    i = pl.program_id(0)

    @pl.when(i < n_tiles)
    def _():
        slot = i % 2
        base = i * tm
        for r in range(tm):
            pltpu.make_async_copy(os_ref.at[pl.ds(pos_ref[base + r], 1), :],
                                  buf.at[slot, pl.ds(r, 1), :], sem.at[slot]).start(priority=r % 2)

    @pl.when(i > 0)
    def _():
        slot = (i - 1) % 2
        pltpu.make_async_copy(os_ref.at[pl.ds(0, tm), :], buf.at[slot], sem.at[slot]).wait()
        ga2 = mod_ref[0, :, 5 * d:6 * d]
        x2 = x1_ref[...] + ga2 * buf[slot]
        shf = modf_ref[0, :, 0:d]
        scf = modf_ref[0, :, d:2 * d]
        y_ref[...] = _rms(x2, gf_ref[...]) * (1.0 + scf) + shf


def _final(pos, x1, mod, modf, gf, out_sorted, tm, tiles_per_mod):
    n, d = x1.shape
    r = mod.shape[1]
    n_tiles = n // tm
    done = lambda i: jnp.maximum(i - 1, 0)
    return pl.pallas_call(
        functools.partial(_final_kernel, n_tiles),
        out_shape=jax.ShapeDtypeStruct((n, d), F32),
        grid_spec=pltpu.PrefetchScalarGridSpec(
            num_scalar_prefetch=1,
            grid=(n_tiles + 1,),
            in_specs=[pl.BlockSpec((tm, d), lambda i, pos: (done(i), 0)),
                      pl.BlockSpec((1, r, 6 * d), lambda i, pos: (done(i) // tiles_per_mod, 0, 0)),
                      pl.BlockSpec((1, r, 2 * d), lambda i, pos: (done(i) // tiles_per_mod, 0, 0)),
                      pl.BlockSpec((1, d), lambda i, pos: (0, 0)),
                      pl.BlockSpec(memory_space=pl.ANY)],
            out_specs=pl.BlockSpec((tm, d), lambda i, pos: (done(i), 0)),
            scratch_shapes=[pltpu.VMEM((2, tm, d), F32), pltpu.SemaphoreType.DMA((2,))]),
        compiler_params=_params("arbitrary"),
        name="moe_combine_final",
    )(pos, x1, mod, modf, gf, out_sorted)


def kernel(x_prompt, x_sample, c_prompt, c_sample, cache_k, cache_v, cache_logf, state_conv, page_table, w_ada, b_ada, g_norm1, w_in, b_forget, conv_w, g_attn_out, g_conv_out, w_out, g_norm2, w_router_group, b_router_group, w_router_expert, b_router_expert, w_expert_gate, w_expert_up, w_expert_down, w_ada_final, b_ada_final, g_final):
    d, a, c, h = D_MODEL, ATTN_WIDTH, CONV_WIDTH, N_HEADS
    bsz, t, _ = x_prompt.shape
    dbs = x_sample.shape[0]
    n_prompt = bsz * t
    assert w_ada.shape[0] == 1 and x_sample.shape[1] == 1

    c_all = jnp.concatenate([c_prompt, c_sample], axis=0)
    mod = _ada(c_all, w_ada[0], b_ada[0])
    modf = _ada(c_all, w_ada_final, b_ada_final)
    mod_p, mod_s = mod[:bsz].reshape(bsz, 1, 6 * d), mod[bsz:].reshape(1, dbs, 6 * d)
    modf_p, modf_s = modf[:bsz].reshape(bsz, 1, 2 * d), modf[bsz:].reshape(1, dbs, 2 * d)

    wt = w_in[0].T.astype(BF16)
    wq = wt[0:a].T
    wkt = wt[a:2 * a]
    wv = wt[2 * a:3 * a].T
    wft = jnp.zeros((16, d), BF16).at[0:h].set(wt[3 * a:3 * a + h])
    wc = wt[3 * a + h:].T
    bf = b_forget[0].reshape(h, 1)
    g1 = g_norm1[0].reshape(1, d)
    g2 = g_norm2[0].reshape(1, d)
    gf = g_final.reshape(1, d)
    ga = g_attn_out[0].reshape(1, a)
    gc = g_conv_out[0].reshape(1, c)
    cw = conv_w[0]
    wout = w_out[0].astype(BF16)
    lane_group = jnp.arange(a) // HEAD_DIM
    bd = (lane_group[:, None] == lane_group[None, :]).astype(BF16)
    wr = jnp.zeros((d, LANES), F32)
    wr = wr.at[:, 0:N_EXPERTS].set(w_router_expert[0])
    wr = wr.at[:, ROUTER_GROUP_LANE0:ROUTER_GROUP_LANE0 + N_GROUPS].set(w_router_group[0])
    wrh = wr.astype(BF16)
    wr = jnp.concatenate([wrh, (wr - wrh.astype(F32)).astype(BF16)], axis=1)
    br = jnp.zeros((1, LANES), F32)
    br = br.at[0, 0:N_EXPERTS].set(b_router_expert[0])
    br = br.at[0, ROUTER_GROUP_LANE0:ROUTER_GROUP_LANE0 + N_GROUPS].set(b_router_group[0])
    n_rows_max = n_prompt + dbs + N_BUCKETS * (MOE_TILE - 1)
    max_tiles = -(-n_rows_max // MOE_TILE)
    tri = lambda m: (jnp.arange(m)[:, None] >= jnp.arange(m)[None, :]).astype(BF16)

    q_p, kt_p, vt_p, ktb_p, vb_p, lft_p, cin_p, gb_p = _inproj(
        True, x_prompt, mod_p, g1, wq, wkt, wv, wft, bf, wc, TOKEN_TILE)
    attn_p, xs, wg, wu, wdn = _attn_prompt(
        q_p, ktb_p, _cumsum(lft_p), vb_p, w_expert_gate[0], w_expert_up[0], w_expert_down[0],
        max_tiles * MOE_TILE)
    x1_p, hrow_p, info_p, cnt_p = _merge(
        False, attn_p, cin_p, cin_p, cin_p, gb_p, x_prompt, mod_p, cw, ga, gc, bd, wout, g2, wr, br,
        tri(TOKEN_TILE), jnp.zeros((1, LANES), F32), TOKEN_TILE)

    xs_ = x_sample.reshape(1, dbs, d)
    q_s, k_s, v_s, lft_s, cin_s, gb_s = _inproj(False, xs_, mod_s, g1, wq, wkt, wv, wft, bf, wc, dbs)
    head_of_lane = jnp.arange(a) // HEAD_DIM
    qbd = jnp.where(head_of_lane[None, None, :] == jnp.arange(h)[None, :, None],
                    q_s.reshape(dbs, 1, a), jnp.zeros((), BF16))
    n_pool = cache_k.shape[1]
    page_major = lambda z: z[0].transpose(0, 2, 3, 1).reshape(n_pool, a, PAGE_SIZE)
    attn_s = _attn_sample(
        page_table, qbd, k_s.reshape(dbs, 1, a), v_s.reshape(dbs, 1, a),
        lft_s[0].T.reshape(dbs, h, 1),
        page_major(cache_k), page_major(cache_v), cache_logf[0].transpose(0, 2, 1))
    st = state_conv[0]
    x1_s, hrow_s, info_s, cnt_all = _merge(
        True, attn_s.reshape(1, dbs, a), cin_s, st[:, 1][None], st[:, 0][None], gb_s, xs_, mod_s, cw, ga,
        gc, bd, wout, g2, wr, br, tri(dbs), cnt_p, dbs)

    counts = cnt_all[0, 0:N_BUCKETS].astype(I32)
    padded = (counts + (MOE_TILE - 1)) // MOE_TILE * MOE_TILE
    ends = jnp.cumsum(padded)
    offsets = ends - padded
    n_tiles = (ends[-1] // MOE_TILE).astype(I32)
    tile_ids = jnp.minimum(jnp.arange(max_tiles, dtype=I32), n_tiles - 1)
    tile_bucket = jnp.sum(tile_ids[:, None] >= (ends // MOE_TILE)[None, :], axis=1)
    lo_tab, hi_tab = _pair_tables()
    tile_lo = jnp.asarray(lo_tab)[tile_bucket]
    tile_hi = jnp.asarray(hi_tab)[tile_bucket]
    bucket_ids = jnp.arange(N_BUCKETS, dtype=F32)

    def position(info):
        bucket, rank = info[..., 0:1], info[..., 1].astype(I32)
        base = jnp.sum(jnp.where(bucket == bucket_ids, offsets, 0), axis=-1)
        return (base + rank).reshape(-1)

    pos_p, pos_s = position(info_p), position(info_s)

    xs = _dispatch(pos_p, hrow_p.reshape(n_prompt, ROW_WIDTH), xs, TOKEN_TILE)
    xs = _dispatch(pos_s, hrow_s.reshape(dbs, ROW_WIDTH), xs, dbs)
    out_sorted = _experts(tile_lo, tile_hi, n_tiles.reshape(1), xs, wg, wu, wdn)
    y_p = _final(pos_p, x1_p.reshape(n_prompt, d), mod_p, modf_p, gf, out_sorted, TOKEN_TILE,
                 t // TOKEN_TILE)
    y_s = _final(pos_s, x1_s.reshape(dbs, d), mod_s, modf_s, gf, out_sorted, dbs, 1)

    kv_p = lambda zt: zt.reshape(bsz, h, HEAD_DIM, t).transpose(0, 3, 1, 2)[None]
    kv_s = lambda z: z.reshape(1, dbs, 1, h, HEAD_DIM)
    return (y_p.reshape(bsz, t, d), y_s.reshape(dbs, 1, d),
            kv_p(kt_p), kv_p(vt_p),
            lft_p.transpose(0, 2, 1)[None],
            cin_p[:, t - (CONV_K - 1):, :][None],
            kv_s(k_s), kv_s(v_s),
            lft_s[0].T.reshape(1, dbs, 1, h),
            jnp.stack([st[:, 1], cin_s[0]], axis=1)[None])
```

```python
import functools

import numpy as np
import jax
import jax.numpy as jnp
from jax import lax
from jax.experimental import pallas as pl
from jax.experimental.pallas import tpu as pltpu

F32 = jnp.float32
BF16 = jnp.bfloat16
I32 = jnp.int32

D_MODEL = 1024
N_HEADS = 8
HEAD_DIM = 64
ATTN_WIDTH = N_HEADS * HEAD_DIM
CONV_WIDTH = D_MODEL - ATTN_WIDTH
CONV_K = 3
N_GROUPS = 4
EXPERTS_PER_GROUP = 8
N_EXPERTS = N_GROUPS * EXPERTS_PER_GROUP
D_EXPERT = 256
PAGE_SIZE = 128
EPS = 1e-6

LANES = 128
ROUTER_GROUP_LANE0 = 32
VMEM_LIMIT = 56 * 1024 * 1024
TOKEN_TILE = 512
ATTN_TILE = 512
PAGES_PER_STEP = 64
PV_SLOTS = 8

PAIRS_PER_GROUP = EXPERTS_PER_GROUP * (EXPERTS_PER_GROUP - 1) // 2
N_BUCKETS = N_GROUPS * PAIRS_PER_GROUP
MOE_TILE = 256
ROW_WIDTH = D_MODEL + LANES
LOG2E = 1.4426950408889634


def _pair_tables():
    lo, hi = [], []
    for g in range(N_GROUPS):
        for a in range(EXPERTS_PER_GROUP):
            for b in range(a + 1, EXPERTS_PER_GROUP):
                lo.append(g * EXPERTS_PER_GROUP + a)
                hi.append(g * EXPERTS_PER_GROUP + b)
    return np.asarray(lo, np.int32), np.asarray(hi, np.int32)


def _params(*sem):
    return pltpu.CompilerParams(dimension_semantics=sem, vmem_limit_bytes=VMEM_LIMIT)


def _rms(x, g):
    return x * lax.rsqrt(jnp.mean(x * x, axis=-1, keepdims=True) + EPS) * g


def _split3(a):
    hi = a.astype(BF16)
    r = a - hi.astype(F32)
    mid = r.astype(BF16)
    lo = (r - mid.astype(F32)).astype(BF16)
    return hi, mid, lo


def _dot3(a, b_exact):
    hi, mid, lo = _split3(a)
    d = lambda t: jnp.dot(t, b_exact, preferred_element_type=F32)
    return d(hi) + d(mid) + d(lo)


def _dot_nt(a, b):
    return lax.dot_general(a, b, (((1,), (1,)), ((), ())), preferred_element_type=F32)


def _ada_kernel(c_ref, w_ref, b_ref, o_ref):
    c = c_ref[...]
    s = (c * jax.nn.sigmoid(c)).astype(BF16)
    o_ref[...] = jnp.dot(s, w_ref[...].astype(BF16), preferred_element_type=F32) + b_ref[...]


def _ada(c, w, b):
    m, d = c.shape
    n = w.shape[1]
    tn = 1024
    return pl.pallas_call(
        _ada_kernel,
        out_shape=jax.ShapeDtypeStruct((m, n), F32),
        grid=(n // tn,),
        in_specs=[pl.BlockSpec((m, d), lambda j: (0, 0)),
                  pl.BlockSpec((d, tn), lambda j: (0, j)),
                  pl.BlockSpec((1, tn), lambda j: (0, j))],
        out_specs=pl.BlockSpec((m, tn), lambda j: (0, j)),
        compiler_params=_params("arbitrary"),
        name="adaln_mod",
    )(c, w, b.reshape(1, n))


def _spread_heads(x, extra_ones):
    rows = x.shape[0]
    lane = lax.broadcasted_iota(jnp.int32, (rows, LANES), 1)
    low = lane < HEAD_DIM
    tiles = []
    for pair in range(N_HEADS // 2):
        tile = x[:, pair * LANES:(pair + 1) * LANES]
        for hh, vals in enumerate((tile, pltpu.roll(tile, HEAD_DIM, 1))):
            ones = functools.reduce(jnp.logical_or, [lane == o for o in extra_ones(2 * pair + hh)])
            tiles.append(jnp.where(low, vals, jnp.where(ones, 1.0, 0.0)).astype(BF16))
    return jnp.concatenate(tiles, axis=-1)


def _inproj_kernel(prompt, x_ref, mod_ref, g1_ref, wq_ref, wkt_ref, wv_ref, wft_ref, bf_ref,
                   wc_ref, *outs):
    c, d = CONV_WIDTH, D_MODEL
    x = x_ref[0]
    sh1 = mod_ref[0, :, 0:d]
    sc1 = mod_ref[0, :, d:2 * d]
    h = _rms(x, g1_ref[...]) * (1.0 + sc1) + sh1
    hb = h.astype(BF16)
    q = jnp.dot(hb, wq_ref[...], preferred_element_type=F32)
    if prompt:
        q_ref, kt32_ref, vt32_ref, ktb_ref, vb_ref, lft_ref, cin_ref, gb_ref = outs
        kt = _dot_nt(wkt_ref[...], hb)
        v = jnp.dot(hb, wv_ref[...], preferred_element_type=F32)
        kt32_ref[0] = kt
        vt32_ref[0] = v.T
        ktb_ref[0] = kt.astype(BF16)
        q_ref[0] = _spread_heads(q * (HEAD_DIM ** -0.5 * LOG2E),
                                 lambda hd: [HEAD_DIM + part * N_HEADS + hd for part in range(3)])
        vb_ref[0] = _spread_heads(v, lambda hd: [HEAD_DIM])
    else:
        q_ref, k32_ref, v32_ref, lft_ref, cin_ref, gb_ref = outs
        k32_ref[0] = _dot_nt(hb, wkt_ref[...])
        v32_ref[0] = jnp.dot(hb, wv_ref[...], preferred_element_type=F32)
        q_ref[0] = (q * (HEAD_DIM ** -0.5)).astype(BF16)
    z = _dot_nt(wft_ref[...], hb)[0:N_HEADS, :] + bf_ref[...]
    lft_ref[0] = jnp.minimum(z, 0.0) - jnp.log1p(jnp.exp(-jnp.abs(z)))
    cv = jnp.dot(hb, wc_ref[...], preferred_element_type=F32)
    cin_ref[0] = cv[:, 2 * c:3 * c] * cv[:, 0:c]
    gb_ref[0] = cv[:, c:2 * c]


def _inproj(prompt, x, mod, g1, wq, wkt, wv, wft, bf, wc, tm):
    g, t, d = x.shape
    r = mod.shape[1]
    a, c = ATTN_WIDTH, CONV_WIDTH
    row = lambda w, dt: jax.ShapeDtypeStruct((g, t, w), dt)
    col = lambda w, dt: jax.ShapeDtypeStruct((g, w, t), dt)
    rspec = lambda w: pl.BlockSpec((1, tm, w), lambda b, i: (b, i, 0))
    cspec = lambda w: pl.BlockSpec((1, w, tm), lambda b, i: (b, 0, i))
    const = lambda arr: pl.BlockSpec(arr.shape, lambda b, i: (0,) * arr.ndim)
    if prompt:
        out_shape = (row(2 * a, BF16), col(a, F32), col(a, F32), col(a, BF16), row(2 * a, BF16),
                     col(N_HEADS, F32), row(c, F32), row(c, F32))
        out_specs = (rspec(2 * a), cspec(a), cspec(a), cspec(a), rspec(2 * a), cspec(N_HEADS), rspec(c),
                     rspec(c))
    else:
        out_shape = (row(a, BF16), row(a, F32), row(a, F32), col(N_HEADS, F32), row(c, F32), row(c, F32))
        out_specs = (rspec(a), rspec(a), rspec(a), cspec(N_HEADS), rspec(c), rspec(c))
    return pl.pallas_call(
        functools.partial(_inproj_kernel, prompt),
        out_shape=out_shape,
        grid=(g, t // tm),
        in_specs=[rspec(d),
                  pl.BlockSpec((1, r, 6 * d), lambda b, i: (b, 0, 0)),
                  const(g1), const(wq), const(wkt), const(wv), const(wft), const(bf),
                  const(wc)],
        out_specs=out_specs,
        compiler_params=_params("arbitrary", "arbitrary"),
        name="norm_inproj",
    )(x, mod, g1, wq, wkt, wv, wft, bf, wc)


def _cumsum_kernel(lf_ref, u_ref, o_ref):
    t = lf_ref.shape[2]
    tc = u_ref.shape[0]
    carry = jnp.zeros((N_HEADS, 1), F32)
    pad = jnp.zeros((HEAD_DIM - 3 * N_HEADS, tc), F32)
    for i in range(t // tc):
        blk = lf_ref[0, :, i * tc:(i + 1) * tc]
        cs = _dot3(blk, u_ref[...]) + carry
        carry = cs[:, tc - 1:tc]
        parts = [p.astype(F32) for p in _split3(cs * (-LOG2E))]
        o_ref[0, :, i * tc:(i + 1) * tc] = jnp.concatenate(parts + [pad], axis=0).astype(BF16)


def _cumsum(lft):
    g, h, t = lft.shape
    tc = 512
    idx = jnp.arange(tc)
    upper = (idx[:, None] <= idx[None, :]).astype(BF16)
    return pl.pallas_call(
        _cumsum_kernel,
        out_shape=jax.ShapeDtypeStruct((g, HEAD_DIM, t), BF16),
        grid=(g,),
        in_specs=[pl.BlockSpec((1, h, t), lambda b: (b, 0, 0)),
                  pl.BlockSpec((tc, tc), lambda b: (0, 0))],
        out_specs=pl.BlockSpec((1, HEAD_DIM, t), lambda b: (b, 0, 0)),
        compiler_params=_params("arbitrary"),
        name="logf_cumsum",
    )(lft, upper)


def _attn_prompt_kernel(q_ref, kt_ref, kb_ref, v_ref, wg_ref, wu_ref, wd_ref,
                        o_ref, xs_ref, wgb_ref, wub_ref, wdb_ref):
    xs_ref[...] = jnp.zeros_like(xs_ref)
    wgb_ref[...] = wg_ref[...].astype(BF16)
    wub_ref[...] = wu_ref[...].astype(BF16)
    wdb_ref[...] = wd_ref[...].astype(BF16)
    t = q_ref.shape[1]
    ta = ATTN_TILE
    n = t // ta
    row = lax.broadcasted_iota(jnp.int32, (ta, ta), 0)
    col = lax.broadcasted_iota(jnp.int32, (ta, ta), 1)
    causal = col <= row
    kb = kb_ref[0]
    kt_aug = [jnp.concatenate([kt_ref[0, hh * HEAD_DIM:(hh + 1) * HEAD_DIM, :], kb], axis=0)
              for hh in range(2)]
    for i in range(n):
        outs = []
        for hh in range(2):
            ls = slice(hh * LANES, (hh + 1) * LANES)
            qa = q_ref[0, i * ta:(i + 1) * ta, ls]
            m = jnp.full((ta, 1), -jnp.inf, F32)
            acc = jnp.zeros((ta, LANES), F32)
            for k0, k1 in ([(0, i * ta)] if i else []) + [(i * ta, (i + 1) * ta)]:
                s = jnp.dot(qa, kt_aug[hh][:, k0:k1], preferred_element_type=F32)
                if k0 == i * ta:
                    s = jnp.where(causal, s, -jnp.inf)
                m_new = jnp.maximum(m, jnp.max(s, axis=-1, keepdims=True))
                p = jnp.exp2(s - m_new).astype(BF16)
                acc = jnp.exp2(m - m_new) * acc + jnp.dot(p, v_ref[0, k0:k1, ls],
                                                          preferred_element_type=F32)
                m = m_new
            outs.append(acc[:, 0:HEAD_DIM] / acc[:, HEAD_DIM:HEAD_DIM + 1])
        o_ref[0, i * ta:(i + 1) * ta, :] = jnp.concatenate(outs, axis=-1)


def _attn_prompt(q_aug, ktb, kbias, v_aug, wg, wu, wd, xs_rows):
    g, t, _ = q_aug.shape
    a = ATTN_WIDTH
    pairs = a // LANES
    n_steps = g * pairs
    wide = pl.BlockSpec((1, t, 2 * LANES), lambda b, p: (b, 0, p))
    slab = lambda arr: pl.BlockSpec((arr.shape[0] // n_steps, arr.shape[1]), lambda b, p: (b * pairs + p, 0))
    flat = [w.reshape(-1, w.shape[-1]) for w in (wg, wu, wd)]
    xs_shape = jax.ShapeDtypeStruct((xs_rows, ROW_WIDTH), F32)
    assert all(arr.shape[0] % (8 * n_steps) == 0 for arr in flat + [xs_shape])
    outs = pl.pallas_call(
        _attn_prompt_kernel,
        out_shape=(jax.ShapeDtypeStruct((g, t, a), F32), xs_shape,
                   *[jax.ShapeDtypeStruct(w.shape, BF16) for w in flat]),
        grid=(g, pairs),
        in_specs=[wide, pl.BlockSpec((1, LANES, t), lambda b, p: (b, p, 0)),
                  pl.BlockSpec((1, HEAD_DIM, t), lambda b, p: (b, 0, 0)), wide,
                  *[slab(w) for w in flat]],
        out_specs=(pl.BlockSpec((1, t, LANES), lambda b, p: (b, 0, p)), slab(xs_shape),
                   *[slab(w) for w in flat]),
        compiler_params=_params("arbitrary", "arbitrary"),
        name="fox_prompt_attn",
    )(q_aug, ktb, kbias, v_aug, *flat)
    attn, xs, wgb, wub, wdb = outs
    return attn, xs, wgb.reshape(wg.shape), wub.reshape(wu.shape), wdb.reshape(wd.shape)


def _scores_sample_kernel(pt_ref, qbd_ref, kn_ref, fn_ref, lmat_ref, *rest):
    np_ = PAGES_PER_STEP
    kt_refs, f_refs = rest[0:np_], rest[np_:2 * np_]
    p_ref, pn_ref, l_ref, need_ref, m_sc, snew_sc, carry_sc, pmax_sc = rest[2 * np_:]
    step = pl.program_id(1)
    qbd = qbd_ref[0]

    @pl.when(step == 0)
    def _():
        kn = kn_ref[0].astype(BF16).astype(F32)
        s_new = jnp.sum(qbd.astype(F32) * kn, axis=-1, keepdims=True)
        snew_sc[...] = s_new
        m_sc[...] = s_new
        carry_sc[...] = fn_ref[0]
        pmax_sc[...] = jnp.full(pmax_sc.shape, -jnp.inf, F32)

    f_all = jnp.concatenate([f_refs[i][0] for i in range(np_)], axis=0)
    suffix = _dot3(f_all, lmat_ref[...])
    totals = jnp.sum(f_all, axis=-1, keepdims=True)
    carry = carry_sc[...]
    lane = lax.broadcasted_iota(jnp.int32, pmax_sc.shape, 1)
    pmax = pmax_sc[...]
    for i in range(np_):
        sl = slice(i * N_HEADS, (i + 1) * N_HEADS)
        order = step * np_ + i
        s = jnp.dot(qbd, kt_refs[i][0].astype(BF16), preferred_element_type=F32) + carry + suffix[sl]
        p_ref[0, order] = s
        pmax = jnp.where(lane == order, jnp.max(s, axis=-1, keepdims=True), pmax)
        carry = carry + totals[sl]
    carry_sc[...] = carry
    pmax_sc[...] = pmax
    m_sc[...] = jnp.maximum(m_sc[...], jnp.max(pmax, axis=-1, keepdims=True))

    @pl.when(step == pl.num_programs(1) - 1)
    def _():
        m = m_sc[...]
        p = jnp.exp(p_ref[0] - m[None])
        p_ref[0] = p
        pn = jnp.exp(snew_sc[...] - m)
        pn_ref[0] = pn
        l_ref[0] = jnp.sum(jnp.sum(p, axis=0), axis=-1, keepdims=True) + pn
        top = jnp.exp(pmax_sc[...] - m).astype(BF16).astype(F32)
        need_ref[0] = jnp.max(jnp.where(top != 0.0, 1.0, 0.0), axis=0, keepdims=True)


def _pv_sample_kernel(pages_ref, count_ref, p_ref, pn_ref, l_ref, vn_ref, vt_hbm, o_ref, vbuf, sem):
    b = pl.program_id(0)
    n = count_ref[b]
    base = b * p_ref.shape[1]

    def fetch(order, slot):
        return pltpu.make_async_copy(vt_hbm.at[pages_ref[base + order]], vbuf.at[slot], sem.at[slot])

    for slot in range(PV_SLOTS):
        @pl.when(slot < n)
        def _(slot=slot):
            fetch(slot, slot).start()

    def body(order, acc):
        slot = order % PV_SLOTS
        fetch(order, slot).wait()
        acc = acc + _dot_nt(p_ref[0, order].astype(BF16), vbuf[slot].astype(BF16))

        @pl.when(order + PV_SLOTS < n)
        def _():
            fetch(order + PV_SLOTS, slot).start()

        return acc

    acc = lax.fori_loop(0, n, body,
                        pn_ref[0].astype(BF16).astype(F32) * vn_ref[0].astype(BF16).astype(F32))
    res = acc / l_ref[0]
    head = lax.broadcasted_iota(jnp.int32, res.shape, 0)
    lane_head = lax.broadcasted_iota(jnp.int32, res.shape, 1) // HEAD_DIM
    o_ref[0] = jnp.sum(jnp.where(head == lane_head, res, 0.0), axis=0, keepdims=True)


def _attn_sample(page_table, qbd, k_new, v_new, f_new, cache_kt, cache_vt, cache_ft):
    bsz, n_pages = page_table.shape
    a, h = ATTN_WIDTH, N_HEADS
    np_ = PAGES_PER_STEP
    n_steps = n_pages // np_
    idx = jnp.arange(PAGE_SIZE)
    lmat = (idx[:, None] > idx[None, :]).astype(BF16)
    newest_first = page_table[:, ::-1].reshape(-1)
    per_b = lambda shp: pl.BlockSpec(shp, lambda b, s, *_: (b,) + (0,) * (len(shp) - 1))

    def page_spec(shape, i):
        return pl.BlockSpec(shape, lambda b, s, pt, *_: (pt[(b * n_steps + s) * np_ + i], 0, 0))

    p, pn, l, need = pl.pallas_call(
        _scores_sample_kernel,
        out_shape=(jax.ShapeDtypeStruct((bsz, n_pages, h, PAGE_SIZE), F32),
                   jax.ShapeDtypeStruct((bsz, h, 1), F32), jax.ShapeDtypeStruct((bsz, h, 1), F32),
                   jax.ShapeDtypeStruct((bsz, 1, n_pages), F32)),
        grid_spec=pltpu.PrefetchScalarGridSpec(
            num_scalar_prefetch=1,
            grid=(bsz, n_steps),
            in_specs=([per_b((1, h, a)), per_b((1, 1, a)), per_b((1, h, 1)),
                       pl.BlockSpec((PAGE_SIZE, PAGE_SIZE), lambda b, s, pt: (0, 0))]
                      + [page_spec((1, a, PAGE_SIZE), i) for i in range(np_)]
                      + [page_spec((1, h, PAGE_SIZE), i) for i in range(np_)]),
            out_specs=(per_b((1, n_pages, h, PAGE_SIZE)), per_b((1, h, 1)), per_b((1, h, 1)),
                       per_b((1, 1, n_pages))),
            scratch_shapes=[pltpu.VMEM((h, 1), F32), pltpu.VMEM((h, 1), F32), pltpu.VMEM((h, 1), F32),
                            pltpu.VMEM((h, n_pages), F32)]),
        compiler_params=_params("arbitrary", "arbitrary"),
        name="fox_sample_scores",
    )(newest_first, qbd, k_new, f_new, lmat, *([cache_kt] * np_), *([cache_ft] * np_))

    order = jnp.arange(1, n_pages + 1, dtype=I32)
    count = jnp.max(jnp.where(need[:, 0, :] > 0, order, 0), axis=-1)

    per_seq = lambda shp: pl.BlockSpec(shp, lambda b, *_: (b,) + (0,) * (len(shp) - 1))
    return pl.pallas_call(
        _pv_sample_kernel,
        out_shape=jax.ShapeDtypeStruct((bsz, 1, a), F32),
        grid_spec=pltpu.PrefetchScalarGridSpec(
            num_scalar_prefetch=2,
            grid=(bsz,),
            in_specs=[per_seq((1, n_pages, h, PAGE_SIZE)), per_seq((1, h, 1)), per_seq((1, h, 1)),
                      per_seq((1, 1, a)), pl.BlockSpec(memory_space=pl.ANY)],
            out_specs=per_seq((1, 1, a)),
            scratch_shapes=[pltpu.VMEM((PV_SLOTS, a, PAGE_SIZE), F32),
                            pltpu.SemaphoreType.DMA((PV_SLOTS,))]),
        compiler_params=_params("arbitrary"),
        name="fox_sample_pv",
    )(newest_first, count, p, pn, l, v_new, cache_vt)


def _group_rms(x, bd, g):
    ms = jnp.dot((x * x).astype(BF16), bd, preferred_element_type=F32)
    return x * lax.rsqrt(ms * (1.0 / HEAD_DIM) + EPS) * g


def _route(logits):
    lane = lax.broadcasted_iota(jnp.int32, logits.shape, 1).astype(F32)
    big = jnp.float32(1e9)
    g0 = float(ROUTER_GROUP_LANE0)
    epg = float(EXPERTS_PER_GROUP)
    glog = jnp.where((lane >= g0) & (lane < g0 + N_GROUPS), logits, -jnp.inf)
    gmax = jnp.max(glog, axis=-1, keepdims=True)
    gidx = jnp.min(jnp.where(glog == gmax, lane, big), axis=-1, keepdims=True) - g0
    gp = 1.0 / jnp.sum(jnp.exp(glog - gmax), axis=-1, keepdims=True)
    lo = gidx * epg
    el = jnp.where((lane >= lo) & (lane < lo + epg), logits, -jnp.inf)
    m1 = jnp.max(el, axis=-1, keepdims=True)
    i1 = jnp.min(jnp.where(el == m1, lane, big), axis=-1, keepdims=True)
    el2 = jnp.where(lane == i1, -jnp.inf, el)
    m2 = jnp.max(el2, axis=-1, keepdims=True)
    i2 = jnp.min(jnp.where(el2 == m2, lane, big), axis=-1, keepdims=True)
    t = jnp.exp(m2 - m1)
    w1 = gp / (1.0 + t)
    w2 = w1 * t
    first_low = i1 < i2
    ea = jnp.minimum(i1, i2) - lo
    eb = jnp.maximum(i1, i2) - lo
    pair = ea * (2.0 * epg - 1.0 - ea) * 0.5 + (eb - ea - 1.0)
    bucket = gidx * float(PAIRS_PER_GROUP) + pair
    return bucket, jnp.where(first_low, w1, w2), jnp.where(first_low, w2, w1)


def _merge_kernel(sample, attn_ref, cin_ref, prev_ref, prev2_ref, gb_ref, x_ref, mod_ref, cw_ref,
                  ga_ref, gc_ref, bd_ref, wout_ref, g2_ref, wr_ref, br_ref, tri_ref, cnt_ref,
                  x1_ref, hrow_ref, info_ref, cnto_ref, cnt_sc):
    d, a = D_MODEL, ATTN_WIDTH
    cin = cin_ref[0]
    tm = cin.shape[0]
    first_step = (pl.program_id(0) == 0) & (pl.program_id(1) == 0)

    @pl.when(first_step)
    def _():
        cnt_sc[...] = cnt_ref[...]

    w0, w1, w2 = cw_ref[0:1, :], cw_ref[1:2, :], cw_ref[2:3, :]
    if sample:
        conv_y = w0 * prev2_ref[0] + w1 * prev_ref[0] + w2 * cin
    else:
        first = pl.program_id(1) == 0
        tail = jnp.where(first, 0.0, prev_ref[0])
        ext = jnp.concatenate([tail, cin], axis=0)
        conv_y = w0 * ext[6:6 + tm] + w1 * ext[7:7 + tm] + w2 * cin
    bd = bd_ref[...]
    an = _group_rms(attn_ref[0], bd, ga_ref[...]).astype(BF16)
    cn = _group_rms(gb_ref[0] * conv_y, bd, gc_ref[...]).astype(BF16)
    mix = (jnp.dot(an, wout_ref[0:a, :], preferred_element_type=F32)
           + jnp.dot(cn, wout_ref[a:, :], preferred_element_type=F32))
    ga1 = mod_ref[0, :, 2 * d:3 * d]
    sh2 = mod_ref[0, :, 3 * d:4 * d]
    sc2 = mod_ref[0, :, 4 * d:5 * d]
    x1 = x_ref[0] + ga1 * mix
    x1_ref[0] = x1
    h2 = _rms(x1, g2_ref[...]) * (1.0 + sc2) + sh2
    hb = h2.astype(BF16)
    hl = (h2 - hb.astype(F32)).astype(BF16)
    both = jnp.dot(hb, wr_ref[...], preferred_element_type=F32)
    logits = (both[:, 0:LANES] + both[:, LANES:]
              + jnp.dot(hl, wr_ref[:, 0:LANES], preferred_element_type=F32)) + br_ref[...]
    bucket, w_lo, w_hi = _route(logits)

    lane = lax.broadcasted_iota(jnp.int32, (tm, LANES), 1)
    lanef = lane.astype(F32)
    onehot = lanef == bucket
    incl = jnp.dot(tri_ref[...], jnp.where(onehot, 1.0, 0.0).astype(BF16), preferred_element_type=F32)
    cnt = cnt_sc[...]
    rank = jnp.sum(jnp.where(onehot, incl - 1.0 + cnt, 0.0), axis=-1, keepdims=True)
    cnt_new = cnt + incl[tm - 1:tm, :]
    cnt_sc[...] = cnt_new
    cnto_ref[...] = cnt_new
    info_ref[0] = jnp.where(lane == 0, bucket, jnp.where(lane == 1, rank, 0.0))

    hrow_ref[0, :, 0:d] = h2
    hrow_ref[0, :, d:] = jnp.where(lane == 0, w_lo, jnp.where(lane == 1, w_hi, 0.0))


def _merge(sample, attn, cin, prev, prev2, gb, x, mod, cw, ga, gc, bd, wout, g2, wr, br, tri, cnt, tm):
    g, t, d = x.shape
    r = mod.shape[1]
    c = CONV_WIDTH
    rspec = lambda w: pl.BlockSpec((1, tm, w), lambda b, i: (b, i, 0))
    const = lambda arr: pl.BlockSpec(arr.shape, lambda b, i: (0,) * arr.ndim)
    if sample:
        prev_spec = rspec(c)
    else:
        prev_spec = pl.BlockSpec((1, 8, c), lambda b, i: (b, jnp.maximum(i * (tm // 8) - 1, 0), 0))
    return pl.pallas_call(
        functools.partial(_merge_kernel, sample),
        out_shape=(jax.ShapeDtypeStruct((g, t, d), F32), jax.ShapeDtypeStruct((g, t, ROW_WIDTH), F32),
                   jax.ShapeDtypeStruct((g, t, LANES), F32), jax.ShapeDtypeStruct((1, LANES), F32)),
        grid=(g, t // tm),
        in_specs=[rspec(ATTN_WIDTH), rspec(c), prev_spec, prev_spec, rspec(c), rspec(d),
                  pl.BlockSpec((1, r, 6 * d), lambda b, i: (b, 0, 0)),
                  const(cw), const(ga), const(gc), const(bd), const(wout), const(g2), const(wr),
                  const(br), const(tri), const(cnt)],
        out_specs=(rspec(d), rspec(ROW_WIDTH), rspec(LANES), pl.BlockSpec((1, LANES), lambda b, i: (0, 0))),
        scratch_shapes=[pltpu.VMEM((1, LANES), F32)],
        compiler_params=_params("arbitrary", "arbitrary"),
        name="mix_out_router",
    )(attn, cin, prev, prev2, gb, x, mod, cw, ga, gc, bd, wout, g2, wr, br, tri, cnt)


def _dispatch_kernel(n_steps, pos_ref, h_ref, xs_in_ref, xs_ref, buf, sem):
    del xs_in_ref
    n_slots, tm, _ = buf.shape
    i = pl.program_id(0)

    def drain(slot):
        pltpu.make_async_copy(buf.at[slot], xs_ref.at[pl.ds(0, tm), :], sem.at[slot]).wait()

    for slot in range(n_slots):
        @pl.when(i > 0)
        def _(slot=slot):
            drain(slot)

        buf[slot] = h_ref[slot * tm:(slot + 1) * tm, :]
        base = (i * n_slots + slot) * tm
        for r in range(tm):
            pltpu.make_async_copy(buf.at[slot, pl.ds(r, 1), :],
                                  xs_ref.at[pl.ds(pos_ref[base + r], 1), :],
                                  sem.at[slot]).start(priority=r % 2)

    @pl.when(i == n_steps - 1)
    def _():
        for slot in range(n_slots):
            drain(slot)


def _dispatch(pos, hrow, xs, tm):
    n, w = hrow.shape
    n_slots = 2 if n % (2 * tm) == 0 else 1
    n_steps = n // (n_slots * tm)
    return pl.pallas_call(
        functools.partial(_dispatch_kernel, n_steps),
        out_shape=jax.ShapeDtypeStruct(xs.shape, xs.dtype),
        grid_spec=pltpu.PrefetchScalarGridSpec(
            num_scalar_prefetch=1,
            grid=(n_steps,),
            in_specs=[pl.BlockSpec((n_slots * tm, w), lambda i, pos: (i, 0)),
                      pl.BlockSpec(memory_space=pl.ANY)],
            out_specs=pl.BlockSpec(memory_space=pl.ANY),
            scratch_shapes=[pltpu.VMEM((n_slots, tm, w), F32), pltpu.SemaphoreType.DMA((n_slots,))]),
        input_output_aliases={2: 0},
        compiler_params=_params("arbitrary"),
        name="moe_dispatch",
    )(pos, hrow, xs)


def _experts_kernel(lo_ref, hi_ref, nt_ref, xs_ref, wgl_ref, wul_ref, wdl_ref, wgh_ref, wuh_ref, wdh_ref,
                    o_ref):
    del lo_ref, hi_ref
    live = pl.program_id(0) < nt_ref[0]

    @pl.when(jnp.logical_not(live))
    def _():
        o_ref[...] = jnp.zeros_like(o_ref)

    @pl.when(live)
    def _():
        d = D_MODEL
        x = xs_ref[:, 0:d].astype(BF16)
        w_lo = xs_ref[:, d:d + 1]
        w_hi = xs_ref[:, d + 1:d + 2]

        def hidden(wg_ref, wu_ref, w):
            gt = jnp.dot(x, wg_ref[0], preferred_element_type=F32)
            up = jnp.dot(x, wu_ref[0], preferred_element_type=F32)
            return (gt * jax.nn.sigmoid(gt) * up * w).astype(BF16)

        o_ref[...] = (jnp.dot(hidden(wgl_ref, wul_ref, w_lo), wdl_ref[0], preferred_element_type=F32)
                      + jnp.dot(hidden(wgh_ref, wuh_ref, w_hi), wdh_ref[0], preferred_element_type=F32))


def _experts(tile_lo, tile_hi, n_tiles, xs, wg, wu, wdn):
    rows, w = xs.shape
    d = D_MODEL
    tg = MOE_TILE
    tile = lambda i, lo, hi, nt: (jnp.minimum(i, nt[0] - 1), 0)
    wspec = lambda shp, which: pl.BlockSpec(
        (1,) + shp, (lambda i, lo, hi, nt: (lo[i], 0, 0)) if which == 0 else (lambda i, lo, hi, nt: (hi[i], 0, 0)))
    return pl.pallas_call(
        _experts_kernel,
        out_shape=jax.ShapeDtypeStruct((rows, d), F32),
        grid_spec=pltpu.PrefetchScalarGridSpec(
            num_scalar_prefetch=3,
            grid=(rows // tg,),
            in_specs=[pl.BlockSpec((tg, w), tile),
                      wspec((d, D_EXPERT), 0), wspec((d, D_EXPERT), 0), wspec((D_EXPERT, d), 0),
                      wspec((d, D_EXPERT), 1), wspec((d, D_EXPERT), 1), wspec((D_EXPERT, d), 1)],
            out_specs=pl.BlockSpec((tg, d), lambda i, lo, hi, nt: (i, 0))),
        compiler_params=_params("arbitrary"),
        name="moe_experts",
    )(tile_lo, tile_hi, n_tiles, xs, wg, wu, wdn, wg, wu, wdn)


def _final_kernel(n_steps, pos_ref, x1_ref, mod_ref, modf_ref, gf_ref, os_ref, y_ref, buf, sem):
    d = D_MODEL
    n_slots, tm, _ = buf.shape
    i = pl.program_id(0)
    for slot in range(n_slots):
        rows = slice(slot * tm, (slot + 1) * tm)

        @pl.when(i > 0)
        def _(slot=slot, rows=rows):
            pltpu.make_async_copy(os_ref.at[pl.ds(0, tm), :], buf.at[slot], sem.at[slot]).wait()
            ga2 = mod_ref[0, :, 5 * d:6 * d]
            x2 = x1_ref[rows, :] + ga2 * buf[slot]
            shf = modf_ref[0, :, 0:d]
            scf = modf_ref[0, :, d:2 * d]
            y_ref[rows, :] = _rms(x2, gf_ref[...]) * (1.0 + scf) + shf

        @pl.when(i < n_steps)
        def _(slot=slot):
            base = (i * n_slots + slot) * tm
            for r in range(tm):
                pltpu.make_async_copy(os_ref.at[pl.ds(pos_ref[base + r], 1), :],
                                      buf.at[slot, pl.ds(r, 1), :], sem.at[slot]).start(priority=r % 2)


def _final(pos, x1, mod, modf, gf, out_sorted, tm, tiles_per_mod):
    n, d = x1.shape
    r = mod.shape[1]
    n_slots = 2 if n % (2 * tm) == 0 and tiles_per_mod % 2 == 0 else 1
    n_steps = n // (n_slots * tm)
    steps_per_mod = max(tiles_per_mod // n_slots, 1)
    done = lambda i: jnp.maximum(i - 1, 0)
    return pl.pallas_call(
        functools.partial(_final_kernel, n_steps),
        out_shape=jax.ShapeDtypeStruct((n, d), F32),
        grid_spec=pltpu.PrefetchScalarGridSpec(
            num_scalar_prefetch=1,
            grid=(n_steps + 1,),
            in_specs=[pl.BlockSpec((n_slots * tm, d), lambda i, pos: (done(i), 0)),
                      pl.BlockSpec((1, r, 6 * d), lambda i, pos: (done(i) // steps_per_mod, 0, 0)),
                      pl.BlockSpec((1, r, 2 * d), lambda i, pos: (done(i) // steps_per_mod, 0, 0)),
                      pl.BlockSpec((1, d), lambda i, pos: (0, 0)),
                      pl.BlockSpec(memory_space=pl.ANY)],
            out_specs=pl.BlockSpec((n_slots * tm, d), lambda i, pos: (done(i), 0)),
            scratch_shapes=[pltpu.VMEM((n_slots, tm, d), F32), pltpu.SemaphoreType.DMA((n_slots,))]),
        compiler_params=_params("arbitrary"),
        name="moe_combine_final",
    )(pos, x1, mod, modf, gf, out_sorted)


def kernel(x_prompt, x_sample, c_prompt, c_sample, cache_k, cache_v, cache_logf, state_conv, page_table, w_ada, b_ada, g_norm1, w_in, b_forget, conv_w, g_attn_out, g_conv_out, w_out, g_norm2, w_router_group, b_router_group, w_router_expert, b_router_expert, w_expert_gate, w_expert_up, w_expert_down, w_ada_final, b_ada_final, g_final):
    d, a, c, h = D_MODEL, ATTN_WIDTH, CONV_WIDTH, N_HEADS
    bsz, t, _ = x_prompt.shape
    dbs = x_sample.shape[0]
    n_prompt = bsz * t
    assert w_ada.shape[0] == 1 and x_sample.shape[1] == 1

    c_all = jnp.concatenate([c_prompt, c_sample], axis=0)
    mod = _ada(c_all, w_ada[0], b_ada[0])
    modf = _ada(c_all, w_ada_final, b_ada_final)
    mod_p, mod_s = mod[:bsz].reshape(bsz, 1, 6 * d), mod[bsz:].reshape(1, dbs, 6 * d)
    modf_p, modf_s = modf[:bsz].reshape(bsz, 1, 2 * d), modf[bsz:].reshape(1, dbs, 2 * d)

    wt = w_in[0].T.astype(BF16)
    wq = wt[0:a].T
    wkt = wt[a:2 * a]
    wv = wt[2 * a:3 * a].T
    wft = jnp.zeros((16, d), BF16).at[0:h].set(wt[3 * a:3 * a + h])
    wc = wt[3 * a + h:].T
    bf = b_forget[0].reshape(h, 1)
    g1 = g_norm1[0].reshape(1, d)
    g2 = g_norm2[0].reshape(1, d)
    gf = g_final.reshape(1, d)
    ga = g_attn_out[0].reshape(1, a)
    gc = g_conv_out[0].reshape(1, c)
    cw = conv_w[0]
    wout = w_out[0].astype(BF16)
    lane_group = jnp.arange(a) // HEAD_DIM
    bd = (lane_group[:, None] == lane_group[None, :]).astype(BF16)
    wr = jnp.zeros((d, LANES), F32)
    wr = wr.at[:, 0:N_EXPERTS].set(w_router_expert[0])
    wr = wr.at[:, ROUTER_GROUP_LANE0:ROUTER_GROUP_LANE0 + N_GROUPS].set(w_router_group[0])
    wrh = wr.astype(BF16)
    wr = jnp.concatenate([wrh, (wr - wrh.astype(F32)).astype(BF16)], axis=1)
    br = jnp.zeros((1, LANES), F32)
    br = br.at[0, 0:N_EXPERTS].set(b_router_expert[0])
    br = br.at[0, ROUTER_GROUP_LANE0:ROUTER_GROUP_LANE0 + N_GROUPS].set(b_router_group[0])
    n_rows_max = n_prompt + dbs + N_BUCKETS * (MOE_TILE - 1)
    max_tiles = -(-n_rows_max // MOE_TILE)
    tri = lambda m: (jnp.arange(m)[:, None] >= jnp.arange(m)[None, :]).astype(BF16)

    q_p, kt_p, vt_p, ktb_p, vb_p, lft_p, cin_p, gb_p = _inproj(
        True, x_prompt, mod_p, g1, wq, wkt, wv, wft, bf, wc, TOKEN_TILE)
    attn_p, xs, wg, wu, wdn = _attn_prompt(
        q_p, ktb_p, _cumsum(lft_p), vb_p, w_expert_gate[0], w_expert_up[0], w_expert_down[0],
        max_tiles * MOE_TILE)
    x1_p, hrow_p, info_p, cnt_p = _merge(
        False, attn_p, cin_p, cin_p, cin_p, gb_p, x_prompt, mod_p, cw, ga, gc, bd, wout, g2, wr, br,
        tri(TOKEN_TILE), jnp.zeros((1, LANES), F32), TOKEN_TILE)

    xs_ = x_sample.reshape(1, dbs, d)
    q_s, k_s, v_s, lft_s, cin_s, gb_s = _inproj(False, xs_, mod_s, g1, wq, wkt, wv, wft, bf, wc, dbs)
    head_of_lane = jnp.arange(a) // HEAD_DIM
    qbd = jnp.where(head_of_lane[None, None, :] == jnp.arange(h)[None, :, None],
                    q_s.reshape(dbs, 1, a), jnp.zeros((), BF16))
    n_pool = cache_k.shape[1]
    page_major = lambda z: z[0].transpose(0, 2, 3, 1).reshape(n_pool, a, PAGE_SIZE)
    attn_s = _attn_sample(
        page_table, qbd, k_s.reshape(dbs, 1, a), v_s.reshape(dbs, 1, a),
        lft_s[0].T.reshape(dbs, h, 1),
        page_major(cache_k), page_major(cache_v), cache_logf[0].transpose(0, 2, 1))
    st = state_conv[0]
    x1_s, hrow_s, info_s, cnt_all = _merge(
        True, attn_s.reshape(1, dbs, a), cin_s, st[:, 1][None], st[:, 0][None], gb_s, xs_, mod_s, cw, ga,
        gc, bd, wout, g2, wr, br, tri(dbs), cnt_p, dbs)

    counts = cnt_all[0, 0:N_BUCKETS].astype(I32)
    padded = (counts + (MOE_TILE - 1)) // MOE_TILE * MOE_TILE
    ends = jnp.cumsum(padded)
    offsets = ends - padded
    n_tiles = (ends[-1] // MOE_TILE).astype(I32)
    tile_ids = jnp.minimum(jnp.arange(max_tiles, dtype=I32), n_tiles - 1)
    tile_bucket = jnp.sum(tile_ids[:, None] >= (ends // MOE_TILE)[None, :], axis=1)
    lo_tab, hi_tab = _pair_tables()
    tile_lo = jnp.asarray(lo_tab)[tile_bucket]
    tile_hi = jnp.asarray(hi_tab)[tile_bucket]
    bucket_ids = jnp.arange(N_BUCKETS, dtype=F32)

    def position(info):
        bucket, rank = info[..., 0:1], info[..., 1].astype(I32)
        base = jnp.sum(jnp.where(bucket == bucket_ids, offsets, 0), axis=-1)
        return (base + rank).reshape(-1)

    pos_p, pos_s = position(info_p), position(info_s)

    xs = _dispatch(pos_p, hrow_p.reshape(n_prompt, ROW_WIDTH), xs, TOKEN_TILE)
    xs = _dispatch(pos_s, hrow_s.reshape(dbs, ROW_WIDTH), xs, dbs)
    out_sorted = _experts(tile_lo, tile_hi, n_tiles.reshape(1), xs, wg, wu, wdn)
    y_p = _final(pos_p, x1_p.reshape(n_prompt, d), mod_p, modf_p, gf, out_sorted, TOKEN_TILE,
                 t // TOKEN_TILE)
    y_s = _final(pos_s, x1_s.reshape(dbs, d), mod_s, modf_s, gf, out_sorted, dbs, 1)

    kv_p = lambda zt: zt.reshape(bsz, h, HEAD_DIM, t).transpose(0, 3, 1, 2)[None]
    kv_s = lambda z: z.reshape(1, dbs, 1, h, HEAD_DIM)
    return (y_p.reshape(bsz, t, d), y_s.reshape(dbs, 1, d),
            kv_p(kt_p), kv_p(vt_p),
            lft_p.transpose(0, 2, 1)[None],
            cin_p[:, t - (CONV_K - 1):, :][None],
            kv_s(k_s), kv_s(v_s),
            lft_s[0].T.reshape(1, dbs, 1, h),
            jnp.stack([st[:, 1], cin_s[0]], axis=1)[None])
```

```python
import functools

import numpy as np
import jax
import jax.numpy as jnp
from jax import lax
from jax.experimental import pallas as pl
from jax.experimental.pallas import tpu as pltpu

F32 = jnp.float32
BF16 = jnp.bfloat16
I32 = jnp.int32

D_MODEL = 1024
N_HEADS = 8
HEAD_DIM = 64
ATTN_WIDTH = N_HEADS * HEAD_DIM
CONV_WIDTH = D_MODEL - ATTN_WIDTH
CONV_K = 3
N_GROUPS = 4
EXPERTS_PER_GROUP = 8
N_EXPERTS = N_GROUPS * EXPERTS_PER_GROUP
D_EXPERT = 256
PAGE_SIZE = 128
EPS = 1e-6

LANES = 128
ROUTER_GROUP_LANE0 = 32
VMEM_LIMIT = 56 * 1024 * 1024
TOKEN_TILE = 512
ATTN_TILE = 512
PAGES_PER_STEP = 64
PV_SLOTS = 8

PAIRS_PER_GROUP = EXPERTS_PER_GROUP * (EXPERTS_PER_GROUP - 1) // 2
N_BUCKETS = N_GROUPS * PAIRS_PER_GROUP
MOE_TILE = 256
ROW_WIDTH = D_MODEL + LANES
LOG2E = 1.4426950408889634


def _pair_tables():
    lo, hi = [], []
    for g in range(N_GROUPS):
        for a in range(EXPERTS_PER_GROUP):
            for b in range(a + 1, EXPERTS_PER_GROUP):
                lo.append(g * EXPERTS_PER_GROUP + a)
                hi.append(g * EXPERTS_PER_GROUP + b)
    return np.asarray(lo, np.int32), np.asarray(hi, np.int32)


def _params(*sem):
    return pltpu.CompilerParams(dimension_semantics=sem, vmem_limit_bytes=VMEM_LIMIT)


def _rms(x, g):
    return x * lax.rsqrt(jnp.mean(x * x, axis=-1, keepdims=True) + EPS) * g


def _split3(a):
    hi = a.astype(BF16)
    r = a - hi.astype(F32)
    mid = r.astype(BF16)
    lo = (r - mid.astype(F32)).astype(BF16)
    return hi, mid, lo


def _dot3(a, b_exact):
    hi, mid, lo = _split3(a)
    d = lambda t: jnp.dot(t, b_exact, preferred_element_type=F32)
    return d(hi) + d(mid) + d(lo)


def _dot_nt(a, b):
    return lax.dot_general(a, b, (((1,), (1,)), ((), ())), preferred_element_type=F32)


def _ada_kernel(c_ref, w_ref, b_ref, o_ref):
    c = c_ref[...]
    s = (c * jax.nn.sigmoid(c)).astype(BF16)
    o_ref[...] = jnp.dot(s, w_ref[...].astype(BF16), preferred_element_type=F32) + b_ref[...]


def _ada(c, w, b):
    m, d = c.shape
    n = w.shape[1]
    tn = 1024
    return pl.pallas_call(
        _ada_kernel,
        out_shape=jax.ShapeDtypeStruct((m, n), F32),
        grid=(n // tn,),
        in_specs=[pl.BlockSpec((m, d), lambda j: (0, 0)),
                  pl.BlockSpec((d, tn), lambda j: (0, j)),
                  pl.BlockSpec((1, tn), lambda j: (0, j))],
        out_specs=pl.BlockSpec((m, tn), lambda j: (0, j)),
        compiler_params=_params("arbitrary"),
        name="adaln_mod",
    )(c, w, b.reshape(1, n))


def _spread_heads(x, extra_ones):
    rows = x.shape[0]
    lane = lax.broadcasted_iota(jnp.int32, (rows, LANES), 1)
    low = lane < HEAD_DIM
    tiles = []
    for pair in range(N_HEADS // 2):
        tile = x[:, pair * LANES:(pair + 1) * LANES]
        for hh, vals in enumerate((tile, pltpu.roll(tile, HEAD_DIM, 1))):
            ones = functools.reduce(jnp.logical_or, [lane == o for o in extra_ones(2 * pair + hh)])
            tiles.append(jnp.where(low, vals, jnp.where(ones, 1.0, 0.0)).astype(BF16))
    return jnp.concatenate(tiles, axis=-1)


def _inproj_kernel(prompt, x_ref, mod_ref, g1_ref, wq_ref, wkt_ref, wv_ref, wft_ref, bf_ref,
                   wc_ref, *outs):
    c, d = CONV_WIDTH, D_MODEL
    x = x_ref[0]
    sh1 = mod_ref[0, :, 0:d]
    sc1 = mod_ref[0, :, d:2 * d]
    h = _rms(x, g1_ref[...]) * (1.0 + sc1) + sh1
    hb = h.astype(BF16)
    q = jnp.dot(hb, wq_ref[...], preferred_element_type=F32)
    if prompt:
        q_ref, kt32_ref, vt32_ref, ktb_ref, vb_ref, lft_ref, cin_ref, gb_ref = outs
        kt = _dot_nt(wkt_ref[...], hb)
        v = jnp.dot(hb, wv_ref[...], preferred_element_type=F32)
        kt32_ref[0] = kt
        vt32_ref[0] = v.T
        ktb_ref[0] = kt.astype(BF16)
        q_ref[0] = _spread_heads(q * (HEAD_DIM ** -0.5 * LOG2E),
                                 lambda hd: [HEAD_DIM + part * N_HEADS + hd for part in range(3)])
        vb_ref[0] = _spread_heads(v, lambda hd: [HEAD_DIM])
    else:
        q_ref, k32_ref, v32_ref, lft_ref, cin_ref, gb_ref = outs
        k32_ref[0] = _dot_nt(hb, wkt_ref[...])
        v32_ref[0] = jnp.dot(hb, wv_ref[...], preferred_element_type=F32)
        q_ref[0] = (q * (HEAD_DIM ** -0.5)).astype(BF16)
    z = _dot_nt(wft_ref[...], hb)[0:N_HEADS, :] + bf_ref[...]
    lft_ref[0] = jnp.minimum(z, 0.0) - jnp.log1p(jnp.exp(-jnp.abs(z)))
    cv = jnp.dot(hb, wc_ref[...], preferred_element_type=F32)
    cin_ref[0] = cv[:, 2 * c:3 * c] * cv[:, 0:c]
    gb_ref[0] = cv[:, c:2 * c]


def _inproj(prompt, x, mod, g1, wq, wkt, wv, wft, bf, wc, tm):
    g, t, d = x.shape
    r = mod.shape[1]
    a, c = ATTN_WIDTH, CONV_WIDTH
    row = lambda w, dt: jax.ShapeDtypeStruct((g, t, w), dt)
    col = lambda w, dt: jax.ShapeDtypeStruct((g, w, t), dt)
    rspec = lambda w: pl.BlockSpec((1, tm, w), lambda b, i: (b, i, 0))
    cspec = lambda w: pl.BlockSpec((1, w, tm), lambda b, i: (b, 0, i))
    const = lambda arr: pl.BlockSpec(arr.shape, lambda b, i: (0,) * arr.ndim)
    if prompt:
        out_shape = (row(2 * a, BF16), col(a, F32), col(a, F32), col(a, BF16), row(2 * a, BF16),
                     col(N_HEADS, F32), row(c, F32), row(c, F32))
        out_specs = (rspec(2 * a), cspec(a), cspec(a), cspec(a), rspec(2 * a), cspec(N_HEADS), rspec(c),
                     rspec(c))
    else:
        out_shape = (row(a, BF16), row(a, F32), row(a, F32), col(N_HEADS, F32), row(c, F32), row(c, F32))
        out_specs = (rspec(a), rspec(a), rspec(a), cspec(N_HEADS), rspec(c), rspec(c))
    return pl.pallas_call(
        functools.partial(_inproj_kernel, prompt),
        out_shape=out_shape,
        grid=(g, t // tm),
        in_specs=[rspec(d),
                  pl.BlockSpec((1, r, 6 * d), lambda b, i: (b, 0, 0)),
                  const(g1), const(wq), const(wkt), const(wv), const(wft), const(bf),
                  const(wc)],
        out_specs=out_specs,
        compiler_params=_params("arbitrary", "arbitrary"),
        name="norm_inproj",
    )(x, mod, g1, wq, wkt, wv, wft, bf, wc)


def _cumsum_kernel(lf_ref, u_ref, o_ref):
    t = lf_ref.shape[2]
    tc = u_ref.shape[0]
    carry = jnp.zeros((N_HEADS, 1), F32)
    pad = jnp.zeros((HEAD_DIM - 3 * N_HEADS, tc), F32)
    for i in range(t // tc):
        blk = lf_ref[0, :, i * tc:(i + 1) * tc]
        cs = _dot3(blk, u_ref[...]) + carry
        carry = cs[:, tc - 1:tc]
        parts = [p.astype(F32) for p in _split3(cs * (-LOG2E))]
        o_ref[0, :, i * tc:(i + 1) * tc] = jnp.concatenate(parts + [pad], axis=0).astype(BF16)


def _cumsum(lft):
    g, h, t = lft.shape
    tc = 512
    idx = jnp.arange(tc)
    upper = (idx[:, None] <= idx[None, :]).astype(BF16)
    return pl.pallas_call(
        _cumsum_kernel,
        out_shape=jax.ShapeDtypeStruct((g, HEAD_DIM, t), BF16),
        grid=(g,),
        in_specs=[pl.BlockSpec((1, h, t), lambda b: (b, 0, 0)),
                  pl.BlockSpec((tc, tc), lambda b: (0, 0))],
        out_specs=pl.BlockSpec((1, HEAD_DIM, t), lambda b: (b, 0, 0)),
        compiler_params=_params("arbitrary"),
        name="logf_cumsum",
    )(lft, upper)


def _attn_prompt_kernel(q_ref, kt_ref, kb_ref, v_ref, wg_ref, wu_ref, wd_ref,
                        o_ref, xs_ref, os_ref, wgb_ref, wub_ref, wdb_ref):
    xs_ref[...] = jnp.zeros_like(xs_ref)
    os_ref[...] = jnp.zeros_like(os_ref)
    wgb_ref[...] = wg_ref[...].astype(BF16)
    wub_ref[...] = wu_ref[...].astype(BF16)
    wdb_ref[...] = wd_ref[...].astype(BF16)
    t = q_ref.shape[1]
    ta = ATTN_TILE
    n = t // ta
    row = lax.broadcasted_iota(jnp.int32, (ta, ta), 0)
    col = lax.broadcasted_iota(jnp.int32, (ta, ta), 1)
    causal = col <= row
    kb = kb_ref[0]
    kt_aug = [jnp.concatenate([kt_ref[0, hh * HEAD_DIM:(hh + 1) * HEAD_DIM, :], kb], axis=0)
              for hh in range(2)]
    for i in range(n):
        outs = []
        for hh in range(2):
            ls = slice(hh * LANES, (hh + 1) * LANES)
            qa = q_ref[0, i * ta:(i + 1) * ta, ls]
            m = jnp.full((ta, 1), -jnp.inf, F32)
            acc = jnp.zeros((ta, LANES), F32)
            for k0, k1 in ([(0, i * ta)] if i else []) + [(i * ta, (i + 1) * ta)]:
                s = jnp.dot(qa, kt_aug[hh][:, k0:k1], preferred_element_type=F32)
                if k0 == i * ta:
                    s = jnp.where(causal, s, -jnp.inf)
                m_new = jnp.maximum(m, jnp.max(s, axis=-1, keepdims=True))
                p = jnp.exp2(s - m_new).astype(BF16)
                acc = jnp.exp2(m - m_new) * acc + jnp.dot(p, v_ref[0, k0:k1, ls],
                                                          preferred_element_type=F32)
                m = m_new
            outs.append(acc[:, 0:HEAD_DIM] / acc[:, HEAD_DIM:HEAD_DIM + 1])
        o_ref[0, i * ta:(i + 1) * ta, :] = jnp.concatenate(outs, axis=-1)


def _attn_prompt(q_aug, ktb, kbias, v_aug, wg, wu, wd, xs_rows):
    g, t, _ = q_aug.shape
    a = ATTN_WIDTH
    pairs = a // LANES
    n_steps = g * pairs
    wide = pl.BlockSpec((1, t, 2 * LANES), lambda b, p: (b, 0, p))
    slab = lambda arr: pl.BlockSpec((arr.shape[0] // n_steps, arr.shape[1]), lambda b, p: (b * pairs + p, 0))
    flat = [w.reshape(-1, w.shape[-1]) for w in (wg, wu, wd)]
    zeroed = [jax.ShapeDtypeStruct((xs_rows, ROW_WIDTH), F32), jax.ShapeDtypeStruct((xs_rows, D_MODEL), F32)]
    assert all(arr.shape[0] % (8 * n_steps) == 0 for arr in flat + zeroed)
    outs = pl.pallas_call(
        _attn_prompt_kernel,
        out_shape=(jax.ShapeDtypeStruct((g, t, a), F32), *zeroed,
                   *[jax.ShapeDtypeStruct(w.shape, BF16) for w in flat]),
        grid=(g, pairs),
        in_specs=[wide, pl.BlockSpec((1, LANES, t), lambda b, p: (b, p, 0)),
                  pl.BlockSpec((1, HEAD_DIM, t), lambda b, p: (b, 0, 0)), wide,
                  *[slab(w) for w in flat]],
        out_specs=(pl.BlockSpec((1, t, LANES), lambda b, p: (b, 0, p)), *[slab(z) for z in zeroed],
                   *[slab(w) for w in flat]),
        compiler_params=_params("arbitrary", "arbitrary"),
        name="fox_prompt_attn",
    )(q_aug, ktb, kbias, v_aug, *flat)
    attn, xs, os_zero, wgb, wub, wdb = outs
    return attn, xs, os_zero, wgb.reshape(wg.shape), wub.reshape(wu.shape), wdb.reshape(wd.shape)


def _scores_sample_kernel(pt_ref, qbd_ref, kn_ref, fn_ref, lmat_ref, *rest):
    np_ = PAGES_PER_STEP
    kt_refs, f_refs = rest[0:np_], rest[np_:2 * np_]
    p_ref, pn_ref, l_ref, need_ref, m_sc, snew_sc, carry_sc, pmax_sc = rest[2 * np_:]
    step = pl.program_id(1)
    qbd = qbd_ref[0]

    @pl.when(step == 0)
    def _():
        kn = kn_ref[0].astype(BF16).astype(F32)
        s_new = jnp.sum(qbd.astype(F32) * kn, axis=-1, keepdims=True)
        snew_sc[...] = s_new
        m_sc[...] = s_new
        carry_sc[...] = fn_ref[0]
        pmax_sc[...] = jnp.full(pmax_sc.shape, -jnp.inf, F32)

    f_all = jnp.concatenate([f_refs[i][0] for i in range(np_)], axis=0)
    suffix = _dot3(f_all, lmat_ref[...])
    totals = jnp.sum(f_all, axis=-1, keepdims=True)
    carry = carry_sc[...]
    lane = lax.broadcasted_iota(jnp.int32, pmax_sc.shape, 1)
    pmax = pmax_sc[...]
    for i in range(np_):
        sl = slice(i * N_HEADS, (i + 1) * N_HEADS)
        order = step * np_ + i
        s = jnp.dot(qbd, kt_refs[i][0].astype(BF16), preferred_element_type=F32) + carry + suffix[sl]
        p_ref[0, order] = s
        pmax = jnp.where(lane == order, jnp.max(s, axis=-1, keepdims=True), pmax)
        carry = carry + totals[sl]
    carry_sc[...] = carry
    pmax_sc[...] = pmax
    m_sc[...] = jnp.maximum(m_sc[...], jnp.max(pmax, axis=-1, keepdims=True))

    @pl.when(step == pl.num_programs(1) - 1)
    def _():
        m = m_sc[...]
        p = jnp.exp(p_ref[0] - m[None])
        p_ref[0] = p
        pn = jnp.exp(snew_sc[...] - m)
        pn_ref[0] = pn
        l_ref[0] = jnp.sum(jnp.sum(p, axis=0), axis=-1, keepdims=True) + pn
        top = jnp.exp(pmax_sc[...] - m).astype(BF16).astype(F32)
        need_ref[0] = jnp.max(jnp.where(top != 0.0, 1.0, 0.0), axis=0, keepdims=True)


def _pv_sample_kernel(pages_ref, count_ref, p_ref, pn_ref, l_ref, vn_ref, vt_hbm, o_ref, vbuf, sem):
    b = pl.program_id(0)
    n = count_ref[b]
    n_pages = p_ref.shape[1]

    def fetch(seq, order, slot):
        return pltpu.make_async_copy(vt_hbm.at[pages_ref[seq * n_pages + order]], vbuf.at[slot],
                                     sem.at[slot])

    def start_first_pages(seq):
        for slot in range(PV_SLOTS):
            @pl.when(slot < count_ref[seq])
            def _(slot=slot):
                fetch(seq, slot, slot).start()

    @pl.when(b == 0)
    def _():
        start_first_pages(0)

    def body(order, acc):
        slot = order % PV_SLOTS
        fetch(b, order, slot).wait()
        acc = acc + _dot_nt(p_ref[0, order].astype(BF16), vbuf[slot].astype(BF16))

        @pl.when(order + PV_SLOTS < n)
        def _():
            fetch(b, order + PV_SLOTS, slot).start()

        return acc

    acc = lax.fori_loop(0, n, body,
                        pn_ref[0].astype(BF16).astype(F32) * vn_ref[0].astype(BF16).astype(F32))

    @pl.when(b + 1 < pl.num_programs(0))
    def _():
        start_first_pages(b + 1)

    res = acc / l_ref[0]
    head = lax.broadcasted_iota(jnp.int32, res.shape, 0)
    lane_head = lax.broadcasted_iota(jnp.int32, res.shape, 1) // HEAD_DIM
    o_ref[0] = jnp.sum(jnp.where(head == lane_head, res, 0.0), axis=0, keepdims=True)


def _attn_sample(page_table, qbd, k_new, v_new, f_new, cache_kt, cache_vt, cache_ft):
    bsz, n_pages = page_table.shape
    a, h = ATTN_WIDTH, N_HEADS
    np_ = PAGES_PER_STEP
    n_steps = n_pages // np_
    idx = jnp.arange(PAGE_SIZE)
    lmat = (idx[:, None] > idx[None, :]).astype(BF16)
    newest_first = page_table[:, ::-1].reshape(-1)
    per_b = lambda shp: pl.BlockSpec(shp, lambda b, s, *_: (b,) + (0,) * (len(shp) - 1))

    def page_spec(shape, i):
        return pl.BlockSpec(shape, lambda b, s, pt, *_: (pt[(b * n_steps + s) * np_ + i], 0, 0))

    p, pn, l, need = pl.pallas_call(
        _scores_sample_kernel,
        out_shape=(jax.ShapeDtypeStruct((bsz, n_pages, h, PAGE_SIZE), F32),
                   jax.ShapeDtypeStruct((bsz, h, 1), F32), jax.ShapeDtypeStruct((bsz, h, 1), F32),
                   jax.ShapeDtypeStruct((bsz, 1, n_pages), F32)),
        grid_spec=pltpu.PrefetchScalarGridSpec(
            num_scalar_prefetch=1,
            grid=(bsz, n_steps),
            in_specs=([per_b((1, h, a)), per_b((1, 1, a)), per_b((1, h, 1)),
                       pl.BlockSpec((PAGE_SIZE, PAGE_SIZE), lambda b, s, pt: (0, 0))]
                      + [page_spec((1, a, PAGE_SIZE), i) for i in range(np_)]
                      + [page_spec((1, h, PAGE_SIZE), i) for i in range(np_)]),
            out_specs=(per_b((1, n_pages, h, PAGE_SIZE)), per_b((1, h, 1)), per_b((1, h, 1)),
                       per_b((1, 1, n_pages))),
            scratch_shapes=[pltpu.VMEM((h, 1), F32), pltpu.VMEM((h, 1), F32), pltpu.VMEM((h, 1), F32),
                            pltpu.VMEM((h, n_pages), F32)]),
        compiler_params=_params("arbitrary", "arbitrary"),
        name="fox_sample_scores",
    )(newest_first, qbd, k_new, f_new, lmat, *([cache_kt] * np_), *([cache_ft] * np_))

    order = jnp.arange(1, n_pages + 1, dtype=I32)
    count = jnp.max(jnp.where(need[:, 0, :] > 0, order, 0), axis=-1)

    per_seq = lambda shp: pl.BlockSpec(shp, lambda b, *_: (b,) + (0,) * (len(shp) - 1))
    return pl.pallas_call(
        _pv_sample_kernel,
        out_shape=jax.ShapeDtypeStruct((bsz, 1, a), F32),
        grid_spec=pltpu.PrefetchScalarGridSpec(
            num_scalar_prefetch=2,
            grid=(bsz,),
            in_specs=[per_seq((1, n_pages, h, PAGE_SIZE)), per_seq((1, h, 1)), per_seq((1, h, 1)),
                      per_seq((1, 1, a)), pl.BlockSpec(memory_space=pl.ANY)],
            out_specs=per_seq((1, 1, a)),
            scratch_shapes=[pltpu.VMEM((PV_SLOTS, a, PAGE_SIZE), F32),
                            pltpu.SemaphoreType.DMA((PV_SLOTS,))]),
        compiler_params=_params("arbitrary"),
        name="fox_sample_pv",
    )(newest_first, count, p, pn, l, v_new, cache_vt)


def _group_rms(x, bd, g):
    ms = jnp.dot((x * x).astype(BF16), bd, preferred_element_type=F32)
    return x * lax.rsqrt(ms * (1.0 / HEAD_DIM) + EPS) * g


def _route(logits):
    lane = lax.broadcasted_iota(jnp.int32, logits.shape, 1).astype(F32)
    big = jnp.float32(1e9)
    g0 = float(ROUTER_GROUP_LANE0)
    epg = float(EXPERTS_PER_GROUP)
    glog = jnp.where((lane >= g0) & (lane < g0 + N_GROUPS), logits, -jnp.inf)
    gmax = jnp.max(glog, axis=-1, keepdims=True)
    gidx = jnp.min(jnp.where(glog == gmax, lane, big), axis=-1, keepdims=True) - g0
    gp = 1.0 / jnp.sum(jnp.exp(glog - gmax), axis=-1, keepdims=True)
    lo = gidx * epg
    el = jnp.where((lane >= lo) & (lane < lo + epg), logits, -jnp.inf)
    m1 = jnp.max(el, axis=-1, keepdims=True)
    i1 = jnp.min(jnp.where(el == m1, lane, big), axis=-1, keepdims=True)
    el2 = jnp.where(lane == i1, -jnp.inf, el)
    m2 = jnp.max(el2, axis=-1, keepdims=True)
    i2 = jnp.min(jnp.where(el2 == m2, lane, big), axis=-1, keepdims=True)
    t = jnp.exp(m2 - m1)
    w1 = gp / (1.0 + t)
    w2 = w1 * t
    first_low = i1 < i2
    ea = jnp.minimum(i1, i2) - lo
    eb = jnp.maximum(i1, i2) - lo
    pair = ea * (2.0 * epg - 1.0 - ea) * 0.5 + (eb - ea - 1.0)
    bucket = gidx * float(PAIRS_PER_GROUP) + pair
    return bucket, jnp.where(first_low, w1, w2), jnp.where(first_low, w2, w1)


def _merge_kernel(sample, attn_ref, cin_ref, prev_ref, prev2_ref, gb_ref, x_ref, mod_ref, cw_ref,
                  ga_ref, gc_ref, bd_ref, wout_ref, g2_ref, wr_ref, br_ref, tri_ref, cnt_ref,
                  x1_ref, hrow_ref, info_ref, cnto_ref, cnt_sc):
    d, a = D_MODEL, ATTN_WIDTH
    cin = cin_ref[0]
    tm = cin.shape[0]
    first_step = (pl.program_id(0) == 0) & (pl.program_id(1) == 0)

    @pl.when(first_step)
    def _():
        cnt_sc[...] = cnt_ref[...]

    w0, w1, w2 = cw_ref[0:1, :], cw_ref[1:2, :], cw_ref[2:3, :]
    if sample:
        conv_y = w0 * prev2_ref[0] + w1 * prev_ref[0] + w2 * cin
    else:
        first = pl.program_id(1) == 0
        tail = jnp.where(first, 0.0, prev_ref[0])
        ext = jnp.concatenate([tail, cin], axis=0)
        conv_y = w0 * ext[6:6 + tm] + w1 * ext[7:7 + tm] + w2 * cin
    bd = bd_ref[...]
    an = _group_rms(attn_ref[0], bd, ga_ref[...]).astype(BF16)
    cn = _group_rms(gb_ref[0] * conv_y, bd, gc_ref[...]).astype(BF16)
    mix = (jnp.dot(an, wout_ref[0:a, :], preferred_element_type=F32)
           + jnp.dot(cn, wout_ref[a:, :], preferred_element_type=F32))
    ga1 = mod_ref[0, :, 2 * d:3 * d]
    sh2 = mod_ref[0, :, 3 * d:4 * d]
    sc2 = mod_ref[0, :, 4 * d:5 * d]
    x1 = x_ref[0] + ga1 * mix
    x1_ref[0] = x1
    h2 = _rms(x1, g2_ref[...]) * (1.0 + sc2) + sh2
    hb = h2.astype(BF16)
    hl = (h2 - hb.astype(F32)).astype(BF16)
    both = jnp.dot(hb, wr_ref[...], preferred_element_type=F32)
    logits = (both[:, 0:LANES] + both[:, LANES:]
              + jnp.dot(hl, wr_ref[:, 0:LANES], preferred_element_type=F32)) + br_ref[...]
    bucket, w_lo, w_hi = _route(logits)

    lane = lax.broadcasted_iota(jnp.int32, (tm, LANES), 1)
    lanef = lane.astype(F32)
    onehot = lanef == bucket
    incl = jnp.dot(tri_ref[...], jnp.where(onehot, 1.0, 0.0).astype(BF16), preferred_element_type=F32)
    cnt = cnt_sc[...]
    rank = jnp.sum(jnp.where(onehot, incl - 1.0 + cnt, 0.0), axis=-1, keepdims=True)
    cnt_new = cnt + incl[tm - 1:tm, :]
    cnt_sc[...] = cnt_new
    cnto_ref[...] = cnt_new
    info_ref[0] = jnp.where(lane == 0, bucket, jnp.where(lane == 1, rank, 0.0))

    hrow_ref[0, :, 0:d] = h2
    hrow_ref[0, :, d:] = jnp.where(lane == 0, w_lo, jnp.where(lane == 1, w_hi, 0.0))


def _merge(sample, attn, cin, prev, prev2, gb, x, mod, cw, ga, gc, bd, wout, g2, wr, br, tri, cnt, tm):
    g, t, d = x.shape
    r = mod.shape[1]
    c = CONV_WIDTH
    rspec = lambda w: pl.BlockSpec((1, tm, w), lambda b, i: (b, i, 0))
    const = lambda arr: pl.BlockSpec(arr.shape, lambda b, i: (0,) * arr.ndim)
    if sample:
        prev_spec = rspec(c)
    else:
        prev_spec = pl.BlockSpec((1, 8, c), lambda b, i: (b, jnp.maximum(i * (tm // 8) - 1, 0), 0))
    return pl.pallas_call(
        functools.partial(_merge_kernel, sample),
        out_shape=(jax.ShapeDtypeStruct((g, t, d), F32), jax.ShapeDtypeStruct((g, t, ROW_WIDTH), F32),
                   jax.ShapeDtypeStruct((g, t, LANES), F32), jax.ShapeDtypeStruct((1, LANES), F32)),
        grid=(g, t // tm),
        in_specs=[rspec(ATTN_WIDTH), rspec(c), prev_spec, prev_spec, rspec(c), rspec(d),
                  pl.BlockSpec((1, r, 6 * d), lambda b, i: (b, 0, 0)),
                  const(cw), const(ga), const(gc), const(bd), const(wout), const(g2), const(wr),
                  const(br), const(tri), const(cnt)],
        out_specs=(rspec(d), rspec(ROW_WIDTH), rspec(LANES), pl.BlockSpec((1, LANES), lambda b, i: (0, 0))),
        scratch_shapes=[pltpu.VMEM((1, LANES), F32)],
        compiler_params=_params("arbitrary", "arbitrary"),
        name="mix_out_router",
    )(attn, cin, prev, prev2, gb, x, mod, cw, ga, gc, bd, wout, g2, wr, br, tri, cnt)


def _dispatch_kernel(n_steps, pos_ref, h_ref, xs_in_ref, xs_ref, buf, sem):
    del xs_in_ref
    n_slots, tm, _ = buf.shape
    i = pl.program_id(0)

    def drain(slot):
        pltpu.make_async_copy(buf.at[slot], xs_ref.at[pl.ds(0, tm), :], sem.at[slot]).wait()

    for slot in range(n_slots):
        @pl.when(i > 0)
        def _(slot=slot):
            drain(slot)

        buf[slot] = h_ref[slot * tm:(slot + 1) * tm, :]
        base = (i * n_slots + slot) * tm
        for r in range(tm):
            pltpu.make_async_copy(buf.at[slot, pl.ds(r, 1), :],
                                  xs_ref.at[pl.ds(pos_ref[base + r], 1), :],
                                  sem.at[slot]).start(priority=r % 2)

    @pl.when(i == n_steps - 1)
    def _():
        for slot in range(n_slots):
            drain(slot)


def _dispatch(pos, hrow, xs, tm):
    n, w = hrow.shape
    n_slots = 2 if n % (2 * tm) == 0 else 1
    n_steps = n // (n_slots * tm)
    return pl.pallas_call(
        functools.partial(_dispatch_kernel, n_steps),
        out_shape=jax.ShapeDtypeStruct(xs.shape, xs.dtype),
        grid_spec=pltpu.PrefetchScalarGridSpec(
            num_scalar_prefetch=1,
            grid=(n_steps,),
            in_specs=[pl.BlockSpec((n_slots * tm, w), lambda i, pos: (i, 0)),
                      pl.BlockSpec(memory_space=pl.ANY)],
            out_specs=pl.BlockSpec(memory_space=pl.ANY),
            scratch_shapes=[pltpu.VMEM((n_slots, tm, w), F32), pltpu.SemaphoreType.DMA((n_slots,))]),
        input_output_aliases={2: 0},
        compiler_params=_params("arbitrary"),
        name="moe_dispatch",
    )(pos, hrow, xs)


def _experts_kernel(lo_ref, hi_ref, nt_ref, xs_ref, wgl_ref, wul_ref, wdl_ref, wgh_ref, wuh_ref, wdh_ref,
                    zero_ref, o_ref):
    del lo_ref, hi_ref, zero_ref

    @pl.when(pl.program_id(0) < nt_ref[0])
    def _():
        d = D_MODEL
        x = xs_ref[:, 0:d].astype(BF16)
        w_lo = xs_ref[:, d:d + 1]
        w_hi = xs_ref[:, d + 1:d + 2]

        def hidden(wg_ref, wu_ref, w):
            gt = jnp.dot(x, wg_ref[0], preferred_element_type=F32)
            up = jnp.dot(x, wu_ref[0], preferred_element_type=F32)
            return (gt * jax.nn.sigmoid(gt) * up * w).astype(BF16)

        o_ref[...] = (jnp.dot(hidden(wgl_ref, wul_ref, w_lo), wdl_ref[0], preferred_element_type=F32)
                      + jnp.dot(hidden(wgh_ref, wuh_ref, w_hi), wdh_ref[0], preferred_element_type=F32))


def _experts(tile_lo, tile_hi, n_tiles, xs, wg, wu, wdn, out_zero):
    rows, w = xs.shape
    d = D_MODEL
    tg = MOE_TILE
    tile = lambda i, lo, hi, nt: (jnp.minimum(i, nt[0] - 1), 0)
    wspec = lambda shp, which: pl.BlockSpec(
        (1,) + shp, (lambda i, lo, hi, nt: (lo[i], 0, 0)) if which == 0 else (lambda i, lo, hi, nt: (hi[i], 0, 0)))
    return pl.pallas_call(
        _experts_kernel,
        out_shape=jax.ShapeDtypeStruct((rows, d), F32),
        grid_spec=pltpu.PrefetchScalarGridSpec(
            num_scalar_prefetch=3,
            grid=(rows // tg,),
            in_specs=[pl.BlockSpec((tg, w), tile),
                      wspec((d, D_EXPERT), 0), wspec((d, D_EXPERT), 0), wspec((D_EXPERT, d), 0),
                      wspec((d, D_EXPERT), 1), wspec((d, D_EXPERT), 1), wspec((D_EXPERT, d), 1),
                      pl.BlockSpec(memory_space=pl.ANY)],
            out_specs=pl.BlockSpec((tg, d), tile)),
        input_output_aliases={10: 0},
        compiler_params=_params("arbitrary"),
        name="moe_experts",
    )(tile_lo, tile_hi, n_tiles, xs, wg, wu, wdn, wg, wu, wdn, out_zero)


def _final_kernel(n_steps, pos_ref, x1_ref, mod_ref, modf_ref, gf_ref, os_ref, y_ref, buf, sem):
    d = D_MODEL
    n_slots, tm, _ = buf.shape
    i = pl.program_id(0)
    for slot in range(n_slots):
        rows = slice(slot * tm, (slot + 1) * tm)

        @pl.when(i > 0)
        def _(slot=slot, rows=rows):
            pltpu.make_async_copy(os_ref.at[pl.ds(0, tm), :], buf.at[slot], sem.at[slot]).wait()
            ga2 = mod_ref[0, :, 5 * d:6 * d]
            x2 = x1_ref[rows, :] + ga2 * buf[slot]
            shf = modf_ref[0, :, 0:d]
            scf = modf_ref[0, :, d:2 * d]
            y_ref[rows, :] = _rms(x2, gf_ref[...]) * (1.0 + scf) + shf

        @pl.when(i < n_steps)
        def _(slot=slot):
            base = (i * n_slots + slot) * tm
            for r in range(tm):
                pltpu.make_async_copy(os_ref.at[pl.ds(pos_ref[base + r], 1), :],
                                      buf.at[slot, pl.ds(r, 1), :], sem.at[slot]).start(priority=r % 2)


def _final(pos, x1, mod, modf, gf, out_sorted, tm, tiles_per_mod):
    n, d = x1.shape
    r = mod.shape[1]
    n_slots = 2 if n % (2 * tm) == 0 and tiles_per_mod % 2 == 0 else 1
    n_steps = n // (n_slots * tm)
    steps_per_mod = max(tiles_per_mod // n_slots, 1)
    done = lambda i: jnp.maximum(i - 1, 0)
    return pl.pallas_call(
        functools.partial(_final_kernel, n_steps),
        out_shape=jax.ShapeDtypeStruct((n, d), F32),
        grid_spec=pltpu.PrefetchScalarGridSpec(
            num_scalar_prefetch=1,
            grid=(n_steps + 1,),
            in_specs=[pl.BlockSpec((n_slots * tm, d), lambda i, pos: (done(i), 0)),
                      pl.BlockSpec((1, r, 6 * d), lambda i, pos: (done(i) // steps_per_mod, 0, 0)),
                      pl.BlockSpec((1, r, 2 * d), lambda i, pos: (done(i) // steps_per_mod, 0, 0)),
                      pl.BlockSpec((1, d), lambda i, pos: (0, 0)),
                      pl.BlockSpec(memory_space=pl.ANY)],
            out_specs=pl.BlockSpec((n_slots * tm, d), lambda i, pos: (done(i), 0)),
            scratch_shapes=[pltpu.VMEM((n_slots, tm, d), F32), pltpu.SemaphoreType.DMA((n_slots,))]),
        compiler_params=_params("arbitrary"),
        name="moe_combine_final",
    )(pos, x1, mod, modf, gf, out_sorted)


def kernel(x_prompt, x_sample, c_prompt, c_sample, cache_k, cache_v, cache_logf, state_conv, page_table, w_ada, b_ada, g_norm1, w_in, b_forget, conv_w, g_attn_out, g_conv_out, w_out, g_norm2, w_router_group, b_router_group, w_router_expert, b_router_expert, w_expert_gate, w_expert_up, w_expert_down, w_ada_final, b_ada_final, g_final):
    d, a, c, h = D_MODEL, ATTN_WIDTH, CONV_WIDTH, N_HEADS
    bsz, t, _ = x_prompt.shape
    dbs = x_sample.shape[0]
    n_prompt = bsz * t
    assert w_ada.shape[0] == 1 and x_sample.shape[1] == 1

    c_all = jnp.concatenate([c_prompt, c_sample], axis=0)
    mod = _ada(c_all, w_ada[0], b_ada[0])
    modf = _ada(c_all, w_ada_final, b_ada_final)
    mod_p, mod_s = mod[:bsz].reshape(bsz, 1, 6 * d), mod[bsz:].reshape(1, dbs, 6 * d)
    modf_p, modf_s = modf[:bsz].reshape(bsz, 1, 2 * d), modf[bsz:].reshape(1, dbs, 2 * d)

    wt = w_in[0].T.astype(BF16)
    wq = wt[0:a].T
    wkt = wt[a:2 * a]
    wv = wt[2 * a:3 * a].T
    wft = jnp.zeros((16, d), BF16).at[0:h].set(wt[3 * a:3 * a + h])
    wc = wt[3 * a + h:].T
    bf = b_forget[0].reshape(h, 1)
    g1 = g_norm1[0].reshape(1, d)
    g2 = g_norm2[0].reshape(1, d)
    gf = g_final.reshape(1, d)
    ga = g_attn_out[0].reshape(1, a)
    gc = g_conv_out[0].reshape(1, c)
    cw = conv_w[0]
    wout = w_out[0].astype(BF16)
    lane_group = jnp.arange(a) // HEAD_DIM
    bd = (lane_group[:, None] == lane_group[None, :]).astype(BF16)
    wr = jnp.zeros((d, LANES), F32)
    wr = wr.at[:, 0:N_EXPERTS].set(w_router_expert[0])
    wr = wr.at[:, ROUTER_GROUP_LANE0:ROUTER_GROUP_LANE0 + N_GROUPS].set(w_router_group[0])
    wrh = wr.astype(BF16)
    wr = jnp.concatenate([wrh, (wr - wrh.astype(F32)).astype(BF16)], axis=1)
    br = jnp.zeros((1, LANES), F32)
    br = br.at[0, 0:N_EXPERTS].set(b_router_expert[0])
    br = br.at[0, ROUTER_GROUP_LANE0:ROUTER_GROUP_LANE0 + N_GROUPS].set(b_router_group[0])
    n_rows_max = n_prompt + dbs + N_BUCKETS * (MOE_TILE - 1)
    max_tiles = -(-n_rows_max // MOE_TILE)
    tri = lambda m: (jnp.arange(m)[:, None] >= jnp.arange(m)[None, :]).astype(BF16)

    q_p, kt_p, vt_p, ktb_p, vb_p, lft_p, cin_p, gb_p = _inproj(
        True, x_prompt, mod_p, g1, wq, wkt, wv, wft, bf, wc, TOKEN_TILE)
    attn_p, xs, os_zero, wg, wu, wdn = _attn_prompt(
        q_p, ktb_p, _cumsum(lft_p), vb_p, w_expert_gate[0], w_expert_up[0], w_expert_down[0],
        max_tiles * MOE_TILE)
    x1_p, hrow_p, info_p, cnt_p = _merge(
        False, attn_p, cin_p, cin_p, cin_p, gb_p, x_prompt, mod_p, cw, ga, gc, bd, wout, g2, wr, br,
        tri(TOKEN_TILE), jnp.zeros((1, LANES), F32), TOKEN_TILE)

    xs_ = x_sample.reshape(1, dbs, d)
    q_s, k_s, v_s, lft_s, cin_s, gb_s = _inproj(False, xs_, mod_s, g1, wq, wkt, wv, wft, bf, wc, dbs)
    head_of_lane = jnp.arange(a) // HEAD_DIM
    qbd = jnp.where(head_of_lane[None, None, :] == jnp.arange(h)[None, :, None],
                    q_s.reshape(dbs, 1, a), jnp.zeros((), BF16))
    n_pool = cache_k.shape[1]
    page_major = lambda z: z[0].transpose(0, 2, 3, 1).reshape(n_pool, a, PAGE_SIZE)
    attn_s = _attn_sample(
        page_table, qbd, k_s.reshape(dbs, 1, a), v_s.reshape(dbs, 1, a),
        lft_s[0].T.reshape(dbs, h, 1),
        page_major(cache_k), page_major(cache_v), cache_logf[0].transpose(0, 2, 1))
    st = state_conv[0]
    x1_s, hrow_s, info_s, cnt_all = _merge(
        True, attn_s.reshape(1, dbs, a), cin_s, st[:, 1][None], st[:, 0][None], gb_s, xs_, mod_s, cw, ga,
        gc, bd, wout, g2, wr, br, tri(dbs), cnt_p, dbs)

    counts = cnt_all[0, 0:N_BUCKETS].astype(I32)
    padded = (counts + (MOE_TILE - 1)) // MOE_TILE * MOE_TILE
    ends = jnp.cumsum(padded)
    offsets = ends - padded
    n_tiles = (ends[-1] // MOE_TILE).astype(I32)
    tile_ids = jnp.minimum(jnp.arange(max_tiles, dtype=I32), n_tiles - 1)
    tile_bucket = jnp.sum(tile_ids[:, None] >= (ends // MOE_TILE)[None, :], axis=1)
    lo_tab, hi_tab = _pair_tables()
    tile_lo = jnp.asarray(lo_tab)[tile_bucket]
    tile_hi = jnp.asarray(hi_tab)[tile_bucket]
    bucket_ids = jnp.arange(N_BUCKETS, dtype=F32)

    def position(info):
        bucket, rank = info[..., 0:1], info[..., 1].astype(I32)
        base = jnp.sum(jnp.where(bucket == bucket_ids, offsets, 0), axis=-1)
        return (base + rank).reshape(-1)

    pos_p, pos_s = position(info_p), position(info_s)

    xs = _dispatch(pos_p, hrow_p.reshape(n_prompt, ROW_WIDTH), xs, TOKEN_TILE)
    xs = _dispatch(pos_s, hrow_s.reshape(dbs, ROW_WIDTH), xs, dbs)
    out_sorted = _experts(tile_lo, tile_hi, n_tiles.reshape(1), xs, wg, wu, wdn, os_zero)
    y_p = _final(pos_p, x1_p.reshape(n_prompt, d), mod_p, modf_p, gf, out_sorted, TOKEN_TILE,
                 t // TOKEN_TILE)
    y_s = _final(pos_s, x1_s.reshape(dbs, d), mod_s, modf_s, gf, out_sorted, dbs, 1)

    kv_p = lambda zt: zt.reshape(bsz, h, HEAD_DIM, t).transpose(0, 3, 1, 2)[None]
    kv_s = lambda z: z.reshape(1, dbs, 1, h, HEAD_DIM)
    return (y_p.reshape(bsz, t, d), y_s.reshape(dbs, 1, d),
            kv_p(kt_p), kv_p(vt_p),
            lft_p.transpose(0, 2, 1)[None],
            cin_p[:, t - (CONV_K - 1):, :][None],
            kv_s(k_s), kv_s(v_s),
            lft_s[0].T.reshape(1, dbs, 1, h),
            jnp.stack([st[:, 1], cin_s[0]], axis=1)[None])
```

```python
import functools

import numpy as np
import jax
import jax.numpy as jnp
from jax import lax
from jax.experimental import pallas as pl
from jax.experimental.pallas import tpu as pltpu

F32 = jnp.float32
BF16 = jnp.bfloat16
I32 = jnp.int32

D_MODEL = 1024
N_HEADS = 8
HEAD_DIM = 64
ATTN_WIDTH = N_HEADS * HEAD_DIM
CONV_WIDTH = D_MODEL - ATTN_WIDTH
CONV_K = 3
N_GROUPS = 4
EXPERTS_PER_GROUP = 8
N_EXPERTS = N_GROUPS * EXPERTS_PER_GROUP
D_EXPERT = 256
PAGE_SIZE = 128
EPS = 1e-6

LANES = 128
ROUTER_GROUP_LANE0 = 32
VMEM_LIMIT = 56 * 1024 * 1024
TOKEN_TILE = 512
ATTN_TILE = 512
PAGES_PER_STEP = 64
PV_SLOTS = 8

PAIRS_PER_GROUP = EXPERTS_PER_GROUP * (EXPERTS_PER_GROUP - 1) // 2
N_BUCKETS = N_GROUPS * PAIRS_PER_GROUP
MOE_TILE = 256
ROW_WIDTH = D_MODEL + LANES
LOG2E = 1.4426950408889634


def _pair_tables():
    lo, hi = [], []
    for g in range(N_GROUPS):
        for a in range(EXPERTS_PER_GROUP):
            for b in range(a + 1, EXPERTS_PER_GROUP):
                lo.append(g * EXPERTS_PER_GROUP + a)
                hi.append(g * EXPERTS_PER_GROUP + b)
    return np.asarray(lo, np.int32), np.asarray(hi, np.int32)


def _params(*sem):
    return pltpu.CompilerParams(dimension_semantics=sem, vmem_limit_bytes=VMEM_LIMIT)


def _rms(x, g):
    return x * lax.rsqrt(jnp.mean(x * x, axis=-1, keepdims=True) + EPS) * g


def _split3(a):
    hi = a.astype(BF16)
    r = a - hi.astype(F32)
    mid = r.astype(BF16)
    lo = (r - mid.astype(F32)).astype(BF16)
    return hi, mid, lo


def _dot3(a, b_exact):
    hi, mid, lo = _split3(a)
    d = lambda t: jnp.dot(t, b_exact, preferred_element_type=F32)
    return d(hi) + d(mid) + d(lo)


def _dot_nt(a, b):
    return lax.dot_general(a, b, (((1,), (1,)), ((), ())), preferred_element_type=F32)


def _ada_kernel(c_ref, w_ref, b_ref, o_ref):
    c = c_ref[...]
    s = (c * jax.nn.sigmoid(c)).astype(BF16)
    o_ref[...] = jnp.dot(s, w_ref[...].astype(BF16), preferred_element_type=F32) + b_ref[...]


def _ada(c, w, b):
    m, d = c.shape
    n = w.shape[1]
    tn = 1024
    return pl.pallas_call(
        _ada_kernel,
        out_shape=jax.ShapeDtypeStruct((m, n), F32),
        grid=(n // tn,),
        in_specs=[pl.BlockSpec((m, d), lambda j: (0, 0)),
                  pl.BlockSpec((d, tn), lambda j: (0, j)),
                  pl.BlockSpec((1, tn), lambda j: (0, j))],
        out_specs=pl.BlockSpec((m, tn), lambda j: (0, j)),
        compiler_params=_params("arbitrary"),
        name="adaln_mod",
    )(c, w, b.reshape(1, n))


def _spread_heads(x, extra_ones):
    rows = x.shape[0]
    lane = lax.broadcasted_iota(jnp.int32, (rows, LANES), 1)
    low = lane < HEAD_DIM
    tiles = []
    for pair in range(N_HEADS // 2):
        tile = x[:, pair * LANES:(pair + 1) * LANES]
        for hh, vals in enumerate((tile, pltpu.roll(tile, HEAD_DIM, 1))):
            ones = functools.reduce(jnp.logical_or, [lane == o for o in extra_ones(2 * pair + hh)])
            tiles.append(jnp.where(low, vals, jnp.where(ones, 1.0, 0.0)).astype(BF16))
    return jnp.concatenate(tiles, axis=-1)


def _inproj_kernel(prompt, x_ref, mod_ref, g1_ref, wq_ref, wkt_ref, wv_ref, wft_ref, bf_ref,
                   wc_ref, *outs):
    c, d = CONV_WIDTH, D_MODEL
    x = x_ref[0]
    sh1 = mod_ref[0, :, 0:d]
    sc1 = mod_ref[0, :, d:2 * d]
    h = _rms(x, g1_ref[...]) * (1.0 + sc1) + sh1
    hb = h.astype(BF16)
    q = jnp.dot(hb, wq_ref[...], preferred_element_type=F32)
    if prompt:
        q_ref, kt32_ref, vt32_ref, ktb_ref, vb_ref, lft_ref, cin_ref, gb_ref = outs
        kt = _dot_nt(wkt_ref[...], hb)
        v = jnp.dot(hb, wv_ref[...], preferred_element_type=F32)
        kt32_ref[0] = kt
        vt32_ref[0] = v.T
        ktb_ref[0] = kt.astype(BF16)
        q_ref[0] = _spread_heads(q * (HEAD_DIM ** -0.5 * LOG2E),
                                 lambda hd: [HEAD_DIM + part * N_HEADS + hd for part in range(3)])
        vb_ref[0] = _spread_heads(v, lambda hd: [HEAD_DIM])
    else:
        q_ref, k32_ref, v32_ref, lft_ref, cin_ref, gb_ref = outs
        k32_ref[0] = _dot_nt(hb, wkt_ref[...])
        v32_ref[0] = jnp.dot(hb, wv_ref[...], preferred_element_type=F32)
        q_ref[0] = (q * (HEAD_DIM ** -0.5)).astype(BF16)
    z = _dot_nt(wft_ref[...], hb)[0:N_HEADS, :] + bf_ref[...]
    lft_ref[0] = jnp.minimum(z, 0.0) - jnp.log1p(jnp.exp(-jnp.abs(z)))
    cv = jnp.dot(hb, wc_ref[...], preferred_element_type=F32)
    cin_ref[0] = cv[:, 2 * c:3 * c] * cv[:, 0:c]
    gb_ref[0] = cv[:, c:2 * c]


def _inproj(prompt, x, mod, g1, wq, wkt, wv, wft, bf, wc, tm):
    g, t, d = x.shape
    r = mod.shape[1]
    a, c = ATTN_WIDTH, CONV_WIDTH
    row = lambda w, dt: jax.ShapeDtypeStruct((g, t, w), dt)
    col = lambda w, dt: jax.ShapeDtypeStruct((g, w, t), dt)
    rspec = lambda w: pl.BlockSpec((1, tm, w), lambda b, i: (b, i, 0))
    cspec = lambda w: pl.BlockSpec((1, w, tm), lambda b, i: (b, 0, i))
    const = lambda arr: pl.BlockSpec(arr.shape, lambda b, i: (0,) * arr.ndim)
    if prompt:
        out_shape = (row(2 * a, BF16), col(a, F32), col(a, F32), col(a, BF16), row(2 * a, BF16),
                     col(N_HEADS, F32), row(c, F32), row(c, F32))
        out_specs = (rspec(2 * a), cspec(a), cspec(a), cspec(a), rspec(2 * a), cspec(N_HEADS), rspec(c),
                     rspec(c))
    else:
        out_shape = (row(a, BF16), row(a, F32), row(a, F32), col(N_HEADS, F32), row(c, F32), row(c, F32))
        out_specs = (rspec(a), rspec(a), rspec(a), cspec(N_HEADS), rspec(c), rspec(c))
    return pl.pallas_call(
        functools.partial(_inproj_kernel, prompt),
        out_shape=out_shape,
        grid=(g, t // tm),
        in_specs=[rspec(d),
                  pl.BlockSpec((1, r, 6 * d), lambda b, i: (b, 0, 0)),
                  const(g1), const(wq), const(wkt), const(wv), const(wft), const(bf),
                  const(wc)],
        out_specs=out_specs,
        compiler_params=_params("arbitrary", "arbitrary"),
        name="norm_inproj",
    )(x, mod, g1, wq, wkt, wv, wft, bf, wc)


def _cumsum_kernel(lf_ref, u_ref, o_ref):
    t = lf_ref.shape[2]
    tc = u_ref.shape[0]
    carry = jnp.zeros((N_HEADS, 1), F32)
    pad = jnp.zeros((HEAD_DIM - 3 * N_HEADS, tc), F32)
    for i in range(t // tc):
        blk = lf_ref[0, :, i * tc:(i + 1) * tc]
        cs = _dot3(blk, u_ref[...]) + carry
        carry = cs[:, tc - 1:tc]
        parts = [p.astype(F32) for p in _split3(cs * (-LOG2E))]
        o_ref[0, :, i * tc:(i + 1) * tc] = jnp.concatenate(parts + [pad], axis=0).astype(BF16)


def _cumsum(lft):
    g, h, t = lft.shape
    tc = 512
    idx = jnp.arange(tc)
    upper = (idx[:, None] <= idx[None, :]).astype(BF16)
    return pl.pallas_call(
        _cumsum_kernel,
        out_shape=jax.ShapeDtypeStruct((g, HEAD_DIM, t), BF16),
        grid=(g,),
        in_specs=[pl.BlockSpec((1, h, t), lambda b: (b, 0, 0)),
                  pl.BlockSpec((tc, tc), lambda b: (0, 0))],
        out_specs=pl.BlockSpec((1, HEAD_DIM, t), lambda b: (b, 0, 0)),
        compiler_params=_params("arbitrary"),
        name="logf_cumsum",
    )(lft, upper)


def _attn_prompt_kernel(q_ref, kt_ref, kb_ref, v_ref, wg_ref, wu_ref, wd_ref,
                        o_ref, xs_hbm, os_hbm, wgb_ref, wub_ref, wdb_ref, zbuf, zsem):
    step = pl.program_id(0) * pl.num_programs(1) + pl.program_id(1)
    n_steps = pl.num_programs(0) * pl.num_programs(1)
    slab_rows = zbuf.shape[0]

    def zero_fills(s):
        rows = pl.ds(s * slab_rows, slab_rows)
        return (pltpu.make_async_copy(zbuf, xs_hbm.at[rows, :], zsem.at[0]),
                pltpu.make_async_copy(zbuf.at[:, 0:D_MODEL], os_hbm.at[rows, :], zsem.at[1]))

    @pl.when(step == 0)
    def _():
        zbuf[...] = jnp.zeros_like(zbuf)

    @pl.when(step > 0)
    def _():
        for copy in zero_fills(step - 1):
            copy.wait()

    for copy in zero_fills(step):
        copy.start()

    @pl.when(step == n_steps - 1)
    def _():
        for copy in zero_fills(step):
            copy.wait()

    wgb_ref[...] = wg_ref[...].astype(BF16)
    wub_ref[...] = wu_ref[...].astype(BF16)
    wdb_ref[...] = wd_ref[...].astype(BF16)
    t = q_ref.shape[1]
    ta = ATTN_TILE
    n = t // ta
    row = lax.broadcasted_iota(jnp.int32, (ta, ta), 0)
    col = lax.broadcasted_iota(jnp.int32, (ta, ta), 1)
    causal = col <= row
    kb = kb_ref[0]
    kt_aug = [jnp.concatenate([kt_ref[0, hh * HEAD_DIM:(hh + 1) * HEAD_DIM, :], kb], axis=0)
              for hh in range(2)]
    for i in range(n):
        outs = []
        for hh in range(2):
            ls = slice(hh * LANES, (hh + 1) * LANES)
            qa = q_ref[0, i * ta:(i + 1) * ta, ls]
            m = jnp.full((ta, 1), -jnp.inf, F32)
            acc = jnp.zeros((ta, LANES), F32)
            for k0, k1 in ([(0, i * ta)] if i else []) + [(i * ta, (i + 1) * ta)]:
                s = jnp.dot(qa, kt_aug[hh][:, k0:k1], preferred_element_type=F32)
                if k0 == i * ta:
                    s = jnp.where(causal, s, -jnp.inf)
                m_new = jnp.maximum(m, jnp.max(s, axis=-1, keepdims=True))
                p = jnp.exp2(s - m_new).astype(BF16)
                acc = jnp.exp2(m - m_new) * acc + jnp.dot(p, v_ref[0, k0:k1, ls],
                                                          preferred_element_type=F32)
                m = m_new
            outs.append(acc[:, 0:HEAD_DIM] / acc[:, HEAD_DIM:HEAD_DIM + 1])
        o_ref[0, i * ta:(i + 1) * ta, :] = jnp.concatenate(outs, axis=-1)


def _attn_prompt(q_aug, ktb, kbias, v_aug, wg, wu, wd, xs_rows):
    g, t, _ = q_aug.shape
    a = ATTN_WIDTH
    pairs = a // LANES
    n_steps = g * pairs
    wide = pl.BlockSpec((1, t, 2 * LANES), lambda b, p: (b, 0, p))
    slab = lambda arr: pl.BlockSpec((arr.shape[0] // n_steps, arr.shape[1]), lambda b, p: (b * pairs + p, 0))
    flat = [w.reshape(-1, w.shape[-1]) for w in (wg, wu, wd)]
    zeroed = [jax.ShapeDtypeStruct((xs_rows, ROW_WIDTH), F32), jax.ShapeDtypeStruct((xs_rows, D_MODEL), F32)]
    assert all(arr.shape[0] % (8 * n_steps) == 0 for arr in flat + zeroed)
    outs = pl.pallas_call(
        _attn_prompt_kernel,
        out_shape=(jax.ShapeDtypeStruct((g, t, a), F32), *zeroed,
                   *[jax.ShapeDtypeStruct(w.shape, BF16) for w in flat]),
        grid=(g, pairs),
        in_specs=[wide, pl.BlockSpec((1, LANES, t), lambda b, p: (b, p, 0)),
                  pl.BlockSpec((1, HEAD_DIM, t), lambda b, p: (b, 0, 0)), wide,
                  *[slab(w) for w in flat]],
        out_specs=(pl.BlockSpec((1, t, LANES), lambda b, p: (b, 0, p)),
                   *[pl.BlockSpec(memory_space=pl.ANY) for _ in zeroed], *[slab(w) for w in flat]),
        scratch_shapes=[pltpu.VMEM((xs_rows // n_steps, ROW_WIDTH), F32), pltpu.SemaphoreType.DMA((2,))],
        compiler_params=_params("arbitrary", "arbitrary"),
        name="fox_prompt_attn",
    )(q_aug, ktb, kbias, v_aug, *flat)
    attn, xs, os_zero, wgb, wub, wdb = outs
    return attn, xs, os_zero, wgb.reshape(wg.shape), wub.reshape(wu.shape), wdb.reshape(wd.shape)


def _scores_sample_kernel(pt_ref, qbd_ref, kn_ref, fn_ref, lmat_ref, *rest):
    np_ = PAGES_PER_STEP
    kt_refs, f_refs = rest[0:np_], rest[np_:2 * np_]
    p_ref, pn_ref, l_ref, need_ref, m_sc, snew_sc, carry_sc, pmax_sc = rest[2 * np_:]
    step = pl.program_id(1)
    qbd = qbd_ref[0]

    @pl.when(step == 0)
    def _():
        kn = kn_ref[0].astype(BF16).astype(F32)
        s_new = jnp.sum(qbd.astype(F32) * kn, axis=-1, keepdims=True)
        snew_sc[...] = s_new
        m_sc[...] = s_new
        carry_sc[...] = fn_ref[0]
        pmax_sc[...] = jnp.full(pmax_sc.shape, -jnp.inf, F32)

    f_all = jnp.concatenate([f_refs[i][0] for i in range(np_)], axis=0)
    suffix = _dot3(f_all, lmat_ref[...])
    totals = jnp.sum(f_all, axis=-1, keepdims=True)
    carry = carry_sc[...]
    lane = lax.broadcasted_iota(jnp.int32, pmax_sc.shape, 1)
    pmax = pmax_sc[...]
    for i in range(np_):
        sl = slice(i * N_HEADS, (i + 1) * N_HEADS)
        order = step * np_ + i
        s = jnp.dot(qbd, kt_refs[i][0].astype(BF16), preferred_element_type=F32) + carry + suffix[sl]
        p_ref[0, order] = s
        pmax = jnp.where(lane == order, jnp.max(s, axis=-1, keepdims=True), pmax)
        carry = carry + totals[sl]
    carry_sc[...] = carry
    pmax_sc[...] = pmax
    m_sc[...] = jnp.maximum(m_sc[...], jnp.max(pmax, axis=-1, keepdims=True))

    @pl.when(step == pl.num_programs(1) - 1)
    def _():
        m = m_sc[...]
        p = jnp.exp(p_ref[0] - m[None])
        p_ref[0] = p
        pn = jnp.exp(snew_sc[...] - m)
        pn_ref[0] = pn
        l_ref[0] = jnp.sum(jnp.sum(p, axis=0), axis=-1, keepdims=True) + pn
        top = jnp.exp(pmax_sc[...] - m).astype(BF16).astype(F32)
        need_ref[0] = jnp.max(jnp.where(top != 0.0, 1.0, 0.0), axis=0, keepdims=True)


def _pv_sample_kernel(pages_ref, count_ref, p_ref, pn_ref, l_ref, vn_ref, vt_hbm, o_ref, vbuf, sem):
    b = pl.program_id(0)
    n = count_ref[b]
    n_pages = p_ref.shape[1]

    def fetch(seq, order, slot):
        return pltpu.make_async_copy(vt_hbm.at[pages_ref[seq * n_pages + order]], vbuf.at[slot],
                                     sem.at[slot])

    def start_first_pages(seq):
        for slot in range(PV_SLOTS):
            @pl.when(slot < count_ref[seq])
            def _(slot=slot):
                fetch(seq, slot, slot).start()

    @pl.when(b == 0)
    def _():
        start_first_pages(0)

    def body(order, acc):
        slot = order % PV_SLOTS
        fetch(b, order, slot).wait()
        acc = acc + _dot_nt(p_ref[0, order].astype(BF16), vbuf[slot].astype(BF16))

        @pl.when(order + PV_SLOTS < n)
        def _():
            fetch(b, order + PV_SLOTS, slot).start()

        return acc

    acc = lax.fori_loop(0, n, body,
                        pn_ref[0].astype(BF16).astype(F32) * vn_ref[0].astype(BF16).astype(F32))

    @pl.when(b + 1 < pl.num_programs(0))
    def _():
        start_first_pages(b + 1)

    res = acc / l_ref[0]
    head = lax.broadcasted_iota(jnp.int32, res.shape, 0)
    lane_head = lax.broadcasted_iota(jnp.int32, res.shape, 1) // HEAD_DIM
    o_ref[0] = jnp.sum(jnp.where(head == lane_head, res, 0.0), axis=0, keepdims=True)


def _attn_sample(page_table, qbd, k_new, v_new, f_new, cache_kt, cache_vt, cache_ft):
    bsz, n_pages = page_table.shape
    a, h = ATTN_WIDTH, N_HEADS
    np_ = PAGES_PER_STEP
    n_steps = n_pages // np_
    idx = jnp.arange(PAGE_SIZE)
    lmat = (idx[:, None] > idx[None, :]).astype(BF16)
    newest_first = page_table[:, ::-1].reshape(-1)
    per_b = lambda shp: pl.BlockSpec(shp, lambda b, s, *_: (b,) + (0,) * (len(shp) - 1))

    def page_spec(shape, i):
        return pl.BlockSpec(shape, lambda b, s, pt, *_: (pt[(b * n_steps + s) * np_ + i], 0, 0))

    p, pn, l, need = pl.pallas_call(
        _scores_sample_kernel,
        out_shape=(jax.ShapeDtypeStruct((bsz, n_pages, h, PAGE_SIZE), F32),
                   jax.ShapeDtypeStruct((bsz, h, 1), F32), jax.ShapeDtypeStruct((bsz, h, 1), F32),
                   jax.ShapeDtypeStruct((bsz, 1, n_pages), F32)),
        grid_spec=pltpu.PrefetchScalarGridSpec(
            num_scalar_prefetch=1,
            grid=(bsz, n_steps),
            in_specs=([per_b((1, h, a)), per_b((1, 1, a)), per_b((1, h, 1)),
                       pl.BlockSpec((PAGE_SIZE, PAGE_SIZE), lambda b, s, pt: (0, 0))]
                      + [page_spec((1, a, PAGE_SIZE), i) for i in range(np_)]
                      + [page_spec((1, h, PAGE_SIZE), i) for i in range(np_)]),
            out_specs=(per_b((1, n_pages, h, PAGE_SIZE)), per_b((1, h, 1)), per_b((1, h, 1)),
                       per_b((1, 1, n_pages))),
            scratch_shapes=[pltpu.VMEM((h, 1), F32), pltpu.VMEM((h, 1), F32), pltpu.VMEM((h, 1), F32),
                            pltpu.VMEM((h, n_pages), F32)]),
        compiler_params=_params("arbitrary", "arbitrary"),
        name="fox_sample_scores",
    )(newest_first, qbd, k_new, f_new, lmat, *([cache_kt] * np_), *([cache_ft] * np_))

    order = jnp.arange(1, n_pages + 1, dtype=I32)
    count = jnp.max(jnp.where(need[:, 0, :] > 0, order, 0), axis=-1)

    per_seq = lambda shp: pl.BlockSpec(shp, lambda b, *_: (b,) + (0,) * (len(shp) - 1))
    return pl.pallas_call(
        _pv_sample_kernel,
        out_shape=jax.ShapeDtypeStruct((bsz, 1, a), F32),
        grid_spec=pltpu.PrefetchScalarGridSpec(
            num_scalar_prefetch=2,
            grid=(bsz,),
            in_specs=[per_seq((1, n_pages, h, PAGE_SIZE)), per_seq((1, h, 1)), per_seq((1, h, 1)),
                      per_seq((1, 1, a)), pl.BlockSpec(memory_space=pl.ANY)],
            out_specs=per_seq((1, 1, a)),
            scratch_shapes=[pltpu.VMEM((PV_SLOTS, a, PAGE_SIZE), F32),
                            pltpu.SemaphoreType.DMA((PV_SLOTS,))]),
        compiler_params=_params("arbitrary"),
        name="fox_sample_pv",
    )(newest_first, count, p, pn, l, v_new, cache_vt)


def _group_rms(x, bd, g):
    ms = jnp.dot((x * x).astype(BF16), bd, preferred_element_type=F32)
    return x * lax.rsqrt(ms * (1.0 / HEAD_DIM) + EPS) * g


def _route(logits):
    lane = lax.broadcasted_iota(jnp.int32, logits.shape, 1).astype(F32)
    big = jnp.float32(1e9)
    g0 = float(ROUTER_GROUP_LANE0)
    epg = float(EXPERTS_PER_GROUP)
    glog = jnp.where((lane >= g0) & (lane < g0 + N_GROUPS), logits, -jnp.inf)
    gmax = jnp.max(glog, axis=-1, keepdims=True)
    gidx = jnp.min(jnp.where(glog == gmax, lane, big), axis=-1, keepdims=True) - g0
    gp = 1.0 / jnp.sum(jnp.exp(glog - gmax), axis=-1, keepdims=True)
    lo = gidx * epg
    el = jnp.where((lane >= lo) & (lane < lo + epg), logits, -jnp.inf)
    m1 = jnp.max(el, axis=-1, keepdims=True)
    i1 = jnp.min(jnp.where(el == m1, lane, big), axis=-1, keepdims=True)
    el2 = jnp.where(lane == i1, -jnp.inf, el)
    m2 = jnp.max(el2, axis=-1, keepdims=True)
    i2 = jnp.min(jnp.where(el2 == m2, lane, big), axis=-1, keepdims=True)
    t = jnp.exp(m2 - m1)
    w1 = gp / (1.0 + t)
    w2 = w1 * t
    first_low = i1 < i2
    ea = jnp.minimum(i1, i2) - lo
    eb = jnp.maximum(i1, i2) - lo
    pair = ea * (2.0 * epg - 1.0 - ea) * 0.5 + (eb - ea - 1.0)
    bucket = gidx * float(PAIRS_PER_GROUP) + pair
    return bucket, jnp.where(first_low, w1, w2), jnp.where(first_low, w2, w1)


def _merge_kernel(sample, attn_ref, cin_ref, prev_ref, prev2_ref, gb_ref, x_ref, mod_ref, cw_ref,
                  ga_ref, gc_ref, bd_ref, wout_ref, g2_ref, wr_ref, br_ref, tri_ref, cnt_ref,
                  x1_ref, hrow_ref, info_ref, cnto_ref, cnt_sc):
    d, a = D_MODEL, ATTN_WIDTH
    cin = cin_ref[0]
    tm = cin.shape[0]
    first_step = (pl.program_id(0) == 0) & (pl.program_id(1) == 0)

    @pl.when(first_step)
    def _():
        cnt_sc[...] = cnt_ref[...]

    w0, w1, w2 = cw_ref[0:1, :], cw_ref[1:2, :], cw_ref[2:3, :]
    if sample:
        conv_y = w0 * prev2_ref[0] + w1 * prev_ref[0] + w2 * cin
    else:
        first = pl.program_id(1) == 0
        tail = jnp.where(first, 0.0, prev_ref[0])
        ext = jnp.concatenate([tail, cin], axis=0)
        conv_y = w0 * ext[6:6 + tm] + w1 * ext[7:7 + tm] + w2 * cin
    bd = bd_ref[...]
    an = _group_rms(attn_ref[0], bd, ga_ref[...]).astype(BF16)
    cn = _group_rms(gb_ref[0] * conv_y, bd, gc_ref[...]).astype(BF16)
    mix = (jnp.dot(an, wout_ref[0:a, :], preferred_element_type=F32)
           + jnp.dot(cn, wout_ref[a:, :], preferred_element_type=F32))
    ga1 = mod_ref[0, :, 2 * d:3 * d]
    sh2 = mod_ref[0, :, 3 * d:4 * d]
    sc2 = mod_ref[0, :, 4 * d:5 * d]
    x1 = x_ref[0] + ga1 * mix
    x1_ref[0] = x1
    h2 = _rms(x1, g2_ref[...]) * (1.0 + sc2) + sh2
    hb = h2.astype(BF16)
    hl = (h2 - hb.astype(F32)).astype(BF16)
    both = jnp.dot(hb, wr_ref[...], preferred_element_type=F32)
    logits = (both[:, 0:LANES] + both[:, LANES:]
              + jnp.dot(hl, wr_ref[:, 0:LANES], preferred_element_type=F32)) + br_ref[...]
    bucket, w_lo, w_hi = _route(logits)

    lane = lax.broadcasted_iota(jnp.int32, (tm, LANES), 1)
    lanef = lane.astype(F32)
    onehot = lanef == bucket
    incl = jnp.dot(tri_ref[...], jnp.where(onehot, 1.0, 0.0).astype(BF16), preferred_element_type=F32)
    cnt = cnt_sc[...]
    rank = jnp.sum(jnp.where(onehot, incl - 1.0 + cnt, 0.0), axis=-1, keepdims=True)
    cnt_new = cnt + incl[tm - 1:tm, :]
    cnt_sc[...] = cnt_new
    cnto_ref[...] = cnt_new
    info_ref[0] = jnp.where(lane == 0, bucket, jnp.where(lane == 1, rank, 0.0))

    hrow_ref[0, :, 0:d] = h2
    hrow_ref[0, :, d:] = jnp.where(lane == 0, w_lo, jnp.where(lane == 1, w_hi, 0.0))


def _merge(sample, attn, cin, prev, prev2, gb, x, mod, cw, ga, gc, bd, wout, g2, wr, br, tri, cnt, tm):
    g, t, d = x.shape
    r = mod.shape[1]
    c = CONV_WIDTH
    rspec = lambda w: pl.BlockSpec((1, tm, w), lambda b, i: (b, i, 0))
    const = lambda arr: pl.BlockSpec(arr.shape, lambda b, i: (0,) * arr.ndim)
    if sample:
        prev_spec = rspec(c)
    else:
        prev_spec = pl.BlockSpec((1, 8, c), lambda b, i: (b, jnp.maximum(i * (tm // 8) - 1, 0), 0))
    return pl.pallas_call(
        functools.partial(_merge_kernel, sample),
        out_shape=(jax.ShapeDtypeStruct((g, t, d), F32), jax.ShapeDtypeStruct((g, t, ROW_WIDTH), F32),
                   jax.ShapeDtypeStruct((g, t, LANES), F32), jax.ShapeDtypeStruct((1, LANES), F32)),
        grid=(g, t // tm),
        in_specs=[rspec(ATTN_WIDTH), rspec(c), prev_spec, prev_spec, rspec(c), rspec(d),
                  pl.BlockSpec((1, r, 6 * d), lambda b, i: (b, 0, 0)),
                  const(cw), const(ga), const(gc), const(bd), const(wout), const(g2), const(wr),
                  const(br), const(tri), const(cnt)],
        out_specs=(rspec(d), rspec(ROW_WIDTH), rspec(LANES), pl.BlockSpec((1, LANES), lambda b, i: (0, 0))),
        scratch_shapes=[pltpu.VMEM((1, LANES), F32)],
        compiler_params=_params("arbitrary", "arbitrary"),
        name="mix_out_router",
    )(attn, cin, prev, prev2, gb, x, mod, cw, ga, gc, bd, wout, g2, wr, br, tri, cnt)


def _dispatch_kernel(n_steps, pos_ref, h_ref, xs_in_ref, xs_ref, buf, sem):
    del xs_in_ref
    n_slots, tm, _ = buf.shape
    i = pl.program_id(0)

    def drain(slot):
        pltpu.make_async_copy(buf.at[slot], xs_ref.at[pl.ds(0, tm), :], sem.at[slot]).wait()

    for slot in range(n_slots):
        @pl.when(i > 0)
        def _(slot=slot):
            drain(slot)

        buf[slot] = h_ref[slot * tm:(slot + 1) * tm, :]
        base = (i * n_slots + slot) * tm
        for r in range(tm):
            pltpu.make_async_copy(buf.at[slot, pl.ds(r, 1), :],
                                  xs_ref.at[pl.ds(pos_ref[base + r], 1), :],
                                  sem.at[slot]).start(priority=r % 2)

    @pl.when(i == n_steps - 1)
    def _():
        for slot in range(n_slots):
            drain(slot)


def _dispatch(pos, hrow, xs, tm):
    n, w = hrow.shape
    n_slots = 2 if n % (2 * tm) == 0 else 1
    n_steps = n // (n_slots * tm)
    return pl.pallas_call(
        functools.partial(_dispatch_kernel, n_steps),
        out_shape=jax.ShapeDtypeStruct(xs.shape, xs.dtype),
        grid_spec=pltpu.PrefetchScalarGridSpec(
            num_scalar_prefetch=1,
            grid=(n_steps,),
            in_specs=[pl.BlockSpec((n_slots * tm, w), lambda i, pos: (i, 0)),
                      pl.BlockSpec(memory_space=pl.ANY)],
            out_specs=pl.BlockSpec(memory_space=pl.ANY),
            scratch_shapes=[pltpu.VMEM((n_slots, tm, w), F32), pltpu.SemaphoreType.DMA((n_slots,))]),
        input_output_aliases={2: 0},
        compiler_params=_params("arbitrary"),
        name="moe_dispatch",
    )(pos, hrow, xs)


def _experts_kernel(lo_ref, hi_ref, nt_ref, xs_ref, wgl_ref, wul_ref, wdl_ref, wgh_ref, wuh_ref, wdh_ref,
                    zero_ref, o_ref):
    del lo_ref, hi_ref, zero_ref

    @pl.when(pl.program_id(0) < nt_ref[0])
    def _():
        d = D_MODEL
        x = xs_ref[:, 0:d].astype(BF16)
        w_lo = xs_ref[:, d:d + 1]
        w_hi = xs_ref[:, d + 1:d + 2]

        def hidden(wg_ref, wu_ref, w):
            gt = jnp.dot(x, wg_ref[0], preferred_element_type=F32)
            up = jnp.dot(x, wu_ref[0], preferred_element_type=F32)
            return (gt * jax.nn.sigmoid(gt) * up * w).astype(BF16)

        o_ref[...] = (jnp.dot(hidden(wgl_ref, wul_ref, w_lo), wdl_ref[0], preferred_element_type=F32)
                      + jnp.dot(hidden(wgh_ref, wuh_ref, w_hi), wdh_ref[0], preferred_element_type=F32))


def _experts(tile_lo, tile_hi, n_tiles, xs, wg, wu, wdn, out_zero):
    rows, w = xs.shape
    d = D_MODEL
    tg = MOE_TILE
    tile = lambda i, lo, hi, nt: (jnp.minimum(i, nt[0] - 1), 0)
    wspec = lambda shp, which: pl.BlockSpec(
        (1,) + shp, (lambda i, lo, hi, nt: (lo[i], 0, 0)) if which == 0 else (lambda i, lo, hi, nt: (hi[i], 0, 0)))
    return pl.pallas_call(
        _experts_kernel,
        out_shape=jax.ShapeDtypeStruct((rows, d), F32),
        grid_spec=pltpu.PrefetchScalarGridSpec(
            num_scalar_prefetch=3,
            grid=(rows // tg,),
            in_specs=[pl.BlockSpec((tg, w), tile),
                      wspec((d, D_EXPERT), 0), wspec((d, D_EXPERT), 0), wspec((D_EXPERT, d), 0),
                      wspec((d, D_EXPERT), 1), wspec((d, D_EXPERT), 1), wspec((D_EXPERT, d), 1),
                      pl.BlockSpec(memory_space=pl.ANY)],
            out_specs=pl.BlockSpec((tg, d), tile)),
        input_output_aliases={10: 0},
        compiler_params=_params("arbitrary"),
        name="moe_experts",
    )(tile_lo, tile_hi, n_tiles, xs, wg, wu, wdn, wg, wu, wdn, out_zero)


def _final_kernel(n_steps, pos_ref, x1_ref, mod_ref, modf_ref, gf_ref, os_ref, y_ref, buf, sem):
    d = D_MODEL
    n_slots, tm, _ = buf.shape
    i = pl.program_id(0)
    for slot in range(n_slots):
        rows = slice(slot * tm, (slot + 1) * tm)

        @pl.when(i > 0)
        def _(slot=slot, rows=rows):
            pltpu.make_async_copy(os_ref.at[pl.ds(0, tm), :], buf.at[slot], sem.at[slot]).wait()
            ga2 = mod_ref[0, :, 5 * d:6 * d]
            x2 = x1_ref[rows, :] + ga2 * buf[slot]
            shf = modf_ref[0, :, 0:d]
            scf = modf_ref[0, :, d:2 * d]
            y_ref[rows, :] = _rms(x2, gf_ref[...]) * (1.0 + scf) + shf

        @pl.when(i < n_steps)
        def _(slot=slot):
            base = (i * n_slots + slot) * tm
            for r in range(tm):
                pltpu.make_async_copy(os_ref.at[pl.ds(pos_ref[base + r], 1), :],
                                      buf.at[slot, pl.ds(r, 1), :], sem.at[slot]).start(priority=r % 2)


def _final(pos, x1, mod, modf, gf, out_sorted, tm, tiles_per_mod):
    n, d = x1.shape
    r = mod.shape[1]
    n_slots = 2 if n % (2 * tm) == 0 and tiles_per_mod % 2 == 0 else 1
    n_steps = n // (n_slots * tm)
    steps_per_mod = max(tiles_per_mod // n_slots, 1)
    done = lambda i: jnp.maximum(i - 1, 0)
    return pl.pallas_call(
        functools.partial(_final_kernel, n_steps),
        out_shape=jax.ShapeDtypeStruct((n, d), F32),
        grid_spec=pltpu.PrefetchScalarGridSpec(
            num_scalar_prefetch=1,
            grid=(n_steps + 1,),
            in_specs=[pl.BlockSpec((n_slots * tm, d), lambda i, pos: (done(i), 0)),
                      pl.BlockSpec((1, r, 6 * d), lambda i, pos: (done(i) // steps_per_mod, 0, 0)),
                      pl.BlockSpec((1, r, 2 * d), lambda i, pos: (done(i) // steps_per_mod, 0, 0)),
                      pl.BlockSpec((1, d), lambda i, pos: (0, 0)),
                      pl.BlockSpec(memory_space=pl.ANY)],
            out_specs=pl.BlockSpec((n_slots * tm, d), lambda i, pos: (done(i), 0)),
            scratch_shapes=[pltpu.VMEM((n_slots, tm, d), F32), pltpu.SemaphoreType.DMA((n_slots,))]),
        compiler_params=_params("arbitrary"),
        name="moe_combine_final",
    )(pos, x1, mod, modf, gf, out_sorted)


def kernel(x_prompt, x_sample, c_prompt, c_sample, cache_k, cache_v, cache_logf, state_conv, page_table, w_ada, b_ada, g_norm1, w_in, b_forget, conv_w, g_attn_out, g_conv_out, w_out, g_norm2, w_router_group, b_router_group, w_router_expert, b_router_expert, w_expert_gate, w_expert_up, w_expert_down, w_ada_final, b_ada_final, g_final):
    d, a, c, h = D_MODEL, ATTN_WIDTH, CONV_WIDTH, N_HEADS
    bsz, t, _ = x_prompt.shape
    dbs = x_sample.shape[0]
    n_prompt = bsz * t
    assert w_ada.shape[0] == 1 and x_sample.shape[1] == 1

    c_all = jnp.concatenate([c_prompt, c_sample], axis=0)
    mod = _ada(c_all, w_ada[0], b_ada[0])
    modf = _ada(c_all, w_ada_final, b_ada_final)
    mod_p, mod_s = mod[:bsz].reshape(bsz, 1, 6 * d), mod[bsz:].reshape(1, dbs, 6 * d)
    modf_p, modf_s = modf[:bsz].reshape(bsz, 1, 2 * d), modf[bsz:].reshape(1, dbs, 2 * d)

    wt = w_in[0].T.astype(BF16)
    wq = wt[0:a].T
    wkt = wt[a:2 * a]
    wv = wt[2 * a:3 * a].T
    wft = jnp.zeros((16, d), BF16).at[0:h].set(wt[3 * a:3 * a + h])
    wc = wt[3 * a + h:].T
    bf = b_forget[0].reshape(h, 1)
    g1 = g_norm1[0].reshape(1, d)
    g2 = g_norm2[0].reshape(1, d)
    gf = g_final.reshape(1, d)
    ga = g_attn_out[0].reshape(1, a)
    gc = g_conv_out[0].reshape(1, c)
    cw = conv_w[0]
    wout = w_out[0].astype(BF16)
    lane_group = jnp.arange(a) // HEAD_DIM
    bd = (lane_group[:, None] == lane_group[None, :]).astype(BF16)
    wr = jnp.zeros((d, LANES), F32)
    wr = wr.at[:, 0:N_EXPERTS].set(w_router_expert[0])
    wr = wr.at[:, ROUTER_GROUP_LANE0:ROUTER_GROUP_LANE0 + N_GROUPS].set(w_router_group[0])
    wrh = wr.astype(BF16)
    wr = jnp.concatenate([wrh, (wr - wrh.astype(F32)).astype(BF16)], axis=1)
    br = jnp.zeros((1, LANES), F32)
    br = br.at[0, 0:N_EXPERTS].set(b_router_expert[0])
    br = br.at[0, ROUTER_GROUP_LANE0:ROUTER_GROUP_LANE0 + N_GROUPS].set(b_router_group[0])
    n_rows_max = n_prompt + dbs + N_BUCKETS * (MOE_TILE - 1)
    max_tiles = -(-n_rows_max // MOE_TILE)
    tri = lambda m: (jnp.arange(m)[:, None] >= jnp.arange(m)[None, :]).astype(BF16)

    q_p, kt_p, vt_p, ktb_p, vb_p, lft_p, cin_p, gb_p = _inproj(
        True, x_prompt, mod_p, g1, wq, wkt, wv, wft, bf, wc, TOKEN_TILE)
    attn_p, xs, os_zero, wg, wu, wdn = _attn_prompt(
        q_p, ktb_p, _cumsum(lft_p), vb_p, w_expert_gate[0], w_expert_up[0], w_expert_down[0],
        max_tiles * MOE_TILE)
    x1_p, hrow_p, info_p, cnt_p = _merge(
        False, attn_p, cin_p, cin_p, cin_p, gb_p, x_prompt, mod_p, cw, ga, gc, bd, wout, g2, wr, br,
        tri(TOKEN_TILE), jnp.zeros((1, LANES), F32), TOKEN_TILE)

    xs_ = x_sample.reshape(1, dbs, d)
    q_s, k_s, v_s, lft_s, cin_s, gb_s = _inproj(False, xs_, mod_s, g1, wq, wkt, wv, wft, bf, wc, dbs)
    head_of_lane = jnp.arange(a) // HEAD_DIM
    qbd = jnp.where(head_of_lane[None, None, :] == jnp.arange(h)[None, :, None],
                    q_s.reshape(dbs, 1, a), jnp.zeros((), BF16))
    n_pool = cache_k.shape[1]
    page_major = lambda z: z[0].transpose(0, 2, 3, 1).reshape(n_pool, a, PAGE_SIZE)
    attn_s = _attn_sample(
        page_table, qbd, k_s.reshape(dbs, 1, a), v_s.reshape(dbs, 1, a),
        lft_s[0].T.reshape(dbs, h, 1),
        page_major(cache_k), page_major(cache_v), cache_logf[0].transpose(0, 2, 1))
    st = state_conv[0]
    x1_s, hrow_s, info_s, cnt_all = _merge(
        True, attn_s.reshape(1, dbs, a), cin_s, st[:, 1][None], st[:, 0][None], gb_s, xs_, mod_s, cw, ga,
        gc, bd, wout, g2, wr, br, tri(dbs), cnt_p, dbs)

    counts = cnt_all[0, 0:N_BUCKETS].astype(I32)
    padded = (counts + (MOE_TILE - 1)) // MOE_TILE * MOE_TILE
    ends = jnp.cumsum(padded)
    offsets = ends - padded
    n_tiles = (ends[-1] // MOE_TILE).astype(I32)
    tile_ids = jnp.minimum(jnp.arange(max_tiles, dtype=I32), n_tiles - 1)
    tile_bucket = jnp.sum(tile_ids[:, None] >= (ends // MOE_TILE)[None, :], axis=1)
    lo_tab, hi_tab = _pair_tables()
    tile_lo = jnp.asarray(lo_tab)[tile_bucket]
    tile_hi = jnp.asarray(hi_tab)[tile_bucket]
    bucket_ids = jnp.arange(N_BUCKETS, dtype=F32)

    def position(info):
        bucket, rank = info[..., 0:1], info[..., 1].astype(I32)
        base = jnp.sum(jnp.where(bucket == bucket_ids, offsets, 0), axis=-1)
        return (base + rank).reshape(-1)

    pos_p, pos_s = position(info_p), position(info_s)

    xs = _dispatch(pos_p, hrow_p.reshape(n_prompt, ROW_WIDTH), xs, TOKEN_TILE)
    xs = _dispatch(pos_s, hrow_s.reshape(dbs, ROW_WIDTH), xs, dbs)
    out_sorted = _experts(tile_lo, tile_hi, n_tiles.reshape(1), xs, wg, wu, wdn, os_zero)
    y_p = _final(pos_p, x1_p.reshape(n_prompt, d), mod_p, modf_p, gf, out_sorted, TOKEN_TILE,
                 t // TOKEN_TILE)
    y_s = _final(pos_s, x1_s.reshape(dbs, d), mod_s, modf_s, gf, out_sorted, dbs, 1)

    kv_p = lambda zt: zt.reshape(bsz, h, HEAD_DIM, t).transpose(0, 3, 1, 2)[None]
    kv_s = lambda z: z.reshape(1, dbs, 1, h, HEAD_DIM)
    return (y_p.reshape(bsz, t, d), y_s.reshape(dbs, 1, d),
            kv_p(kt_p), kv_p(vt_p),
            lft_p.transpose(0, 2, 1)[None],
            cin_p[:, t - (CONV_K - 1):, :][None],
            kv_s(k_s), kv_s(v_s),
            lft_s[0].T.reshape(1, dbs, 1, h),
            jnp.stack([st[:, 1], cin_s[0]], axis=1)[None])
```

```python
import functools

import numpy as np
import jax
import jax.numpy as jnp
from jax import lax
from jax.experimental import pallas as pl
from jax.experimental.pallas import tpu as pltpu

F32 = jnp.float32
BF16 = jnp.bfloat16
I32 = jnp.int32

D_MODEL = 1024
N_HEADS = 8
HEAD_DIM = 64
ATTN_WIDTH = N_HEADS * HEAD_DIM
CONV_WIDTH = D_MODEL - ATTN_WIDTH
CONV_K = 3
N_GROUPS = 4
EXPERTS_PER_GROUP = 8
N_EXPERTS = N_GROUPS * EXPERTS_PER_GROUP
D_EXPERT = 256
PAGE_SIZE = 128
EPS = 1e-6

LANES = 128
ROUTER_GROUP_LANE0 = 32
VMEM_LIMIT = 56 * 1024 * 1024
TOKEN_TILE = 512
ATTN_TILE = 512
PAGES_PER_STEP = 64
PV_SLOTS = 8
EXPERT_ROW_SLOTS = 3

PAIRS_PER_GROUP = EXPERTS_PER_GROUP * (EXPERTS_PER_GROUP - 1) // 2
N_BUCKETS = N_GROUPS * PAIRS_PER_GROUP
MOE_TILE = 256
ROW_WIDTH = D_MODEL + LANES
LOG2E = 1.4426950408889634


def _pair_tables():
    lo, hi = [], []
    for g in range(N_GROUPS):
        for a in range(EXPERTS_PER_GROUP):
            for b in range(a + 1, EXPERTS_PER_GROUP):
                lo.append(g * EXPERTS_PER_GROUP + a)
                hi.append(g * EXPERTS_PER_GROUP + b)
    return np.asarray(lo, np.int32), np.asarray(hi, np.int32)


def _params(*sem):
    return pltpu.CompilerParams(dimension_semantics=sem, vmem_limit_bytes=VMEM_LIMIT)


def _rms(x, g):
    return x * lax.rsqrt(jnp.mean(x * x, axis=-1, keepdims=True) + EPS) * g


def _split3(a):
    hi = a.astype(BF16)
    r = a - hi.astype(F32)
    mid = r.astype(BF16)
    lo = (r - mid.astype(F32)).astype(BF16)
    return hi, mid, lo


def _dot3(a, b_exact):
    hi, mid, lo = _split3(a)
    d = lambda t: jnp.dot(t, b_exact, preferred_element_type=F32)
    return d(hi) + d(mid) + d(lo)


def _dot_nt(a, b):
    return lax.dot_general(a, b, (((1,), (1,)), ((), ())), preferred_element_type=F32)


def _ada_kernel(c_ref, w_ref, b_ref, o_ref):
    c = c_ref[...]
    s = (c * jax.nn.sigmoid(c)).astype(BF16)
    o_ref[...] = jnp.dot(s, w_ref[...].astype(BF16), preferred_element_type=F32) + b_ref[...]


def _ada(c, w, b):
    m, d = c.shape
    n = w.shape[1]
    tn = 1024
    return pl.pallas_call(
        _ada_kernel,
        out_shape=jax.ShapeDtypeStruct((m, n), F32),
        grid=(n // tn,),
        in_specs=[pl.BlockSpec((m, d), lambda j: (0, 0)),
                  pl.BlockSpec((d, tn), lambda j: (0, j)),
                  pl.BlockSpec((1, tn), lambda j: (0, j))],
        out_specs=pl.BlockSpec((m, tn), lambda j: (0, j)),
        compiler_params=_params("arbitrary"),
        name="adaln_mod",
    )(c, w, b.reshape(1, n))


def _spread_heads(x, extra_ones):
    rows = x.shape[0]
    lane = lax.broadcasted_iota(jnp.int32, (rows, LANES), 1)
    low = lane < HEAD_DIM
    tiles = []
    for pair in range(N_HEADS // 2):
        tile = x[:, pair * LANES:(pair + 1) * LANES]
        for hh, vals in enumerate((tile, pltpu.roll(tile, HEAD_DIM, 1))):
            ones = functools.reduce(jnp.logical_or, [lane == o for o in extra_ones(2 * pair + hh)])
            tiles.append(jnp.where(low, vals, jnp.where(ones, 1.0, 0.0)).astype(BF16))
    return jnp.concatenate(tiles, axis=-1)


def _inproj_kernel(prompt, x_ref, mod_ref, g1_ref, wq_ref, wkt_ref, wv_ref, wft_ref, bf_ref,
                   wc_ref, *outs):
    c, d = CONV_WIDTH, D_MODEL
    x = x_ref[0]
    sh1 = mod_ref[0, :, 0:d]
    sc1 = mod_ref[0, :, d:2 * d]
    h = _rms(x, g1_ref[...]) * (1.0 + sc1) + sh1
    hb = h.astype(BF16)
    q = jnp.dot(hb, wq_ref[...], preferred_element_type=F32)
    if prompt:
        q_ref, kt32_ref, vt32_ref, ktb_ref, vb_ref, lft_ref, cin_ref, gb_ref = outs
        kt = _dot_nt(wkt_ref[...], hb)
        v = jnp.dot(hb, wv_ref[...], preferred_element_type=F32)
        kt32_ref[0] = kt
        vt32_ref[0] = v.T
        ktb_ref[0] = kt.astype(BF16)
        q_ref[0] = _spread_heads(q * (HEAD_DIM ** -0.5 * LOG2E),
                                 lambda hd: [HEAD_DIM + part * N_HEADS + hd for part in range(3)])
        vb_ref[0] = _spread_heads(v, lambda hd: [HEAD_DIM])
    else:
        q_ref, k32_ref, v32_ref, lft_ref, cin_ref, gb_ref = outs
        k32_ref[0] = _dot_nt(hb, wkt_ref[...])
        v32_ref[0] = jnp.dot(hb, wv_ref[...], preferred_element_type=F32)
        q_ref[0] = (q * (HEAD_DIM ** -0.5)).astype(BF16)
    z = _dot_nt(wft_ref[...], hb)[0:N_HEADS, :] + bf_ref[...]
    lft_ref[0] = jnp.minimum(z, 0.0) - jnp.log1p(jnp.exp(-jnp.abs(z)))
    cv = jnp.dot(hb, wc_ref[...], preferred_element_type=F32)
    cin_ref[0] = cv[:, 2 * c:3 * c] * cv[:, 0:c]
    gb_ref[0] = cv[:, c:2 * c]


def _inproj(prompt, x, mod, g1, wq, wkt, wv, wft, bf, wc, tm):
    g, t, d = x.shape
    r = mod.shape[1]
    a, c = ATTN_WIDTH, CONV_WIDTH
    row = lambda w, dt: jax.ShapeDtypeStruct((g, t, w), dt)
    col = lambda w, dt: jax.ShapeDtypeStruct((g, w, t), dt)
    rspec = lambda w: pl.BlockSpec((1, tm, w), lambda b, i: (b, i, 0))
    cspec = lambda w: pl.BlockSpec((1, w, tm), lambda b, i: (b, 0, i))
    const = lambda arr: pl.BlockSpec(arr.shape, lambda b, i: (0,) * arr.ndim)
    if prompt:
        out_shape = (row(2 * a, BF16), col(a, F32), col(a, F32), col(a, BF16), row(2 * a, BF16),
                     col(N_HEADS, F32), row(c, F32), row(c, F32))
        out_specs = (rspec(2 * a), cspec(a), cspec(a), cspec(a), rspec(2 * a), cspec(N_HEADS), rspec(c),
                     rspec(c))
    else:
        out_shape = (row(a, BF16), row(a, F32), row(a, F32), col(N_HEADS, F32), row(c, F32), row(c, F32))
        out_specs = (rspec(a), rspec(a), rspec(a), cspec(N_HEADS), rspec(c), rspec(c))
    return pl.pallas_call(
        functools.partial(_inproj_kernel, prompt),
        out_shape=out_shape,
        grid=(g, t // tm),
        in_specs=[rspec(d),
                  pl.BlockSpec((1, r, 6 * d), lambda b, i: (b, 0, 0)),
                  const(g1), const(wq), const(wkt), const(wv), const(wft), const(bf),
                  const(wc)],
        out_specs=out_specs,
        compiler_params=_params("arbitrary", "arbitrary"),
        name="norm_inproj",
    )(x, mod, g1, wq, wkt, wv, wft, bf, wc)


def _cumsum_kernel(lf_ref, u_ref, o_ref):
    t = lf_ref.shape[2]
    tc = u_ref.shape[0]
    carry = jnp.zeros((N_HEADS, 1), F32)
    pad = jnp.zeros((HEAD_DIM - 3 * N_HEADS, tc), F32)
    for i in range(t // tc):
        blk = lf_ref[0, :, i * tc:(i + 1) * tc]
        cs = _dot3(blk, u_ref[...]) + carry
        carry = cs[:, tc - 1:tc]
        parts = [p.astype(F32) for p in _split3(cs * (-LOG2E))]
        o_ref[0, :, i * tc:(i + 1) * tc] = jnp.concatenate(parts + [pad], axis=0).astype(BF16)


def _cumsum(lft):
    g, h, t = lft.shape
    tc = 512
    idx = jnp.arange(tc)
    upper = (idx[:, None] <= idx[None, :]).astype(BF16)
    return pl.pallas_call(
        _cumsum_kernel,
        out_shape=jax.ShapeDtypeStruct((g, HEAD_DIM, t), BF16),
        grid=(g,),
        in_specs=[pl.BlockSpec((1, h, t), lambda b: (b, 0, 0)),
                  pl.BlockSpec((tc, tc), lambda b: (0, 0))],
        out_specs=pl.BlockSpec((1, HEAD_DIM, t), lambda b: (b, 0, 0)),
        compiler_params=_params("arbitrary"),
        name="logf_cumsum",
    )(lft, upper)


def _attn_prompt_kernel(q_ref, kt_ref, kb_ref, v_ref, wg_ref, wu_ref, wd_ref,
                        o_ref, xs_hbm, os_hbm, wgb_ref, wub_ref, wdb_ref, zbuf, zsem):
    step = pl.program_id(0) * pl.num_programs(1) + pl.program_id(1)
    n_steps = pl.num_programs(0) * pl.num_programs(1)
    slab_rows = zbuf.shape[0]

    def zero_fills(s):
        rows = pl.ds(s * slab_rows, slab_rows)
        return (pltpu.make_async_copy(zbuf, xs_hbm.at[rows, :], zsem.at[0]),
                pltpu.make_async_copy(zbuf.at[:, 0:D_MODEL], os_hbm.at[rows, :], zsem.at[1]))

    @pl.when(step == 0)
    def _():
        zbuf[...] = jnp.zeros_like(zbuf)

    @pl.when(step > 0)
    def _():
        for copy in zero_fills(step - 1):
            copy.wait()

    for copy in zero_fills(step):
        copy.start()

    @pl.when(step == n_steps - 1)
    def _():
        for copy in zero_fills(step):
            copy.wait()

    wgb_ref[...] = wg_ref[...].astype(BF16)
    wub_ref[...] = wu_ref[...].astype(BF16)
    wdb_ref[...] = wd_ref[...].astype(BF16)
    t = q_ref.shape[1]
    ta = ATTN_TILE
    n = t // ta
    row = lax.broadcasted_iota(jnp.int32, (ta, ta), 0)
    col = lax.broadcasted_iota(jnp.int32, (ta, ta), 1)
    causal = col <= row
    kb = kb_ref[0]
    kt_aug = [jnp.concatenate([kt_ref[0, hh * HEAD_DIM:(hh + 1) * HEAD_DIM, :], kb], axis=0)
              for hh in range(2)]
    for i in range(n):
        outs = []
        for hh in range(2):
            ls = slice(hh * LANES, (hh + 1) * LANES)
            qa = q_ref[0, i * ta:(i + 1) * ta, ls]
            m = jnp.full((ta, 1), -jnp.inf, F32)
            acc = jnp.zeros((ta, LANES), F32)
            for k0, k1 in ([(0, i * ta)] if i else []) + [(i * ta, (i + 1) * ta)]:
                s = jnp.dot(qa, kt_aug[hh][:, k0:k1], preferred_element_type=F32)
                if k0 == i * ta:
                    s = jnp.where(causal, s, -jnp.inf)
                m_new = jnp.maximum(m, jnp.max(s, axis=-1, keepdims=True))
                p = jnp.exp2(s - m_new).astype(BF16)
                acc = jnp.exp2(m - m_new) * acc + jnp.dot(p, v_ref[0, k0:k1, ls],
                                                          preferred_element_type=F32)
                m = m_new
            outs.append(acc[:, 0:HEAD_DIM] / acc[:, HEAD_DIM:HEAD_DIM + 1])
        o_ref[0, i * ta:(i + 1) * ta, :] = jnp.concatenate(outs, axis=-1)


def _attn_prompt(q_aug, ktb, kbias, v_aug, wg, wu, wd, xs_rows):
    g, t, _ = q_aug.shape
    a = ATTN_WIDTH
    pairs = a // LANES
    n_steps = g * pairs
    wide = pl.BlockSpec((1, t, 2 * LANES), lambda b, p: (b, 0, p))
    slab = lambda arr: pl.BlockSpec((arr.shape[0] // n_steps, arr.shape[1]), lambda b, p: (b * pairs + p, 0))
    flat = [w.reshape(-1, w.shape[-1]) for w in (wg, wu, wd)]
    zeroed = [jax.ShapeDtypeStruct((xs_rows, ROW_WIDTH), F32), jax.ShapeDtypeStruct((xs_rows, D_MODEL), F32)]
    assert all(arr.shape[0] % (8 * n_steps) == 0 for arr in flat + zeroed)
    outs = pl.pallas_call(
        _attn_prompt_kernel,
        out_shape=(jax.ShapeDtypeStruct((g, t, a), F32), *zeroed,
                   *[jax.ShapeDtypeStruct(w.shape, BF16) for w in flat]),
        grid=(g, pairs),
        in_specs=[wide, pl.BlockSpec((1, LANES, t), lambda b, p: (b, p, 0)),
                  pl.BlockSpec((1, HEAD_DIM, t), lambda b, p: (b, 0, 0)), wide,
                  *[slab(w) for w in flat]],
        out_specs=(pl.BlockSpec((1, t, LANES), lambda b, p: (b, 0, p)),
                   *[pl.BlockSpec(memory_space=pl.ANY) for _ in zeroed], *[slab(w) for w in flat]),
        scratch_shapes=[pltpu.VMEM((xs_rows // n_steps, ROW_WIDTH), F32), pltpu.SemaphoreType.DMA((2,))],
        compiler_params=_params("arbitrary", "arbitrary"),
        name="fox_prompt_attn",
    )(q_aug, ktb, kbias, v_aug, *flat)
    attn, xs, os_zero, wgb, wub, wdb = outs
    return attn, xs, os_zero, wgb.reshape(wg.shape), wub.reshape(wu.shape), wdb.reshape(wd.shape)


def _scores_sample_kernel(pt_ref, qbd_ref, kn_ref, fn_ref, lmat_ref, *rest):
    np_ = PAGES_PER_STEP
    kt_refs, f_refs = rest[0:np_], rest[np_:2 * np_]
    p_ref, pn_ref, l_ref, need_ref, m_sc, snew_sc, carry_sc, pmax_sc = rest[2 * np_:]
    step = pl.program_id(1)
    qbd = qbd_ref[0]

    @pl.when(step == 0)
    def _():
        kn = kn_ref[0].astype(BF16).astype(F32)
        s_new = jnp.sum(qbd.astype(F32) * kn, axis=-1, keepdims=True)
        snew_sc[...] = s_new
        m_sc[...] = s_new
        carry_sc[...] = fn_ref[0]
        pmax_sc[...] = jnp.full(pmax_sc.shape, -jnp.inf, F32)

    f_all = jnp.concatenate([f_refs[i][0] for i in range(np_)], axis=0)
    suffix = _dot3(f_all, lmat_ref[...])
    totals = jnp.sum(f_all, axis=-1, keepdims=True)
    carry = carry_sc[...]
    lane = lax.broadcasted_iota(jnp.int32, pmax_sc.shape, 1)
    pmax = pmax_sc[...]
    for i in range(np_):
        sl = slice(i * N_HEADS, (i + 1) * N_HEADS)
        order = step * np_ + i
        s = jnp.dot(qbd, kt_refs[i][0].astype(BF16), preferred_element_type=F32) + carry + suffix[sl]
        p_ref[0, order] = s
        pmax = jnp.where(lane == order, jnp.max(s, axis=-1, keepdims=True), pmax)
        carry = carry + totals[sl]
    carry_sc[...] = carry
    pmax_sc[...] = pmax
    m_sc[...] = jnp.maximum(m_sc[...], jnp.max(pmax, axis=-1, keepdims=True))

    @pl.when(step == pl.num_programs(1) - 1)
    def _():
        m = m_sc[...]
        p = jnp.exp(p_ref[0] - m[None])
        p_ref[0] = p
        pn = jnp.exp(snew_sc[...] - m)
        pn_ref[0] = pn
        l_ref[0] = jnp.sum(jnp.sum(p, axis=0), axis=-1, keepdims=True) + pn
        top = jnp.exp(pmax_sc[...] - m).astype(BF16).astype(F32)
        need_ref[0] = jnp.max(jnp.where(top != 0.0, 1.0, 0.0), axis=0, keepdims=True)


def _pv_sample_kernel(pages_ref, count_ref, p_ref, pn_ref, l_ref, vn_ref, vt_hbm, o_ref, vbuf, sem):
    b = pl.program_id(0)
    n = count_ref[b]
    n_pages = p_ref.shape[1]

    def fetch(seq, order, slot):
        return pltpu.make_async_copy(vt_hbm.at[pages_ref[seq * n_pages + order]], vbuf.at[slot],
                                     sem.at[slot])

    def start_first_pages(seq):
        for slot in range(PV_SLOTS):
            @pl.when(slot < count_ref[seq])
            def _(slot=slot):
                fetch(seq, slot, slot).start()

    @pl.when(b == 0)
    def _():
        start_first_pages(0)

    def body(order, acc):
        slot = order % PV_SLOTS
        fetch(b, order, slot).wait()
        acc = acc + _dot_nt(p_ref[0, order].astype(BF16), vbuf[slot].astype(BF16))

        @pl.when(order + PV_SLOTS < n)
        def _():
            fetch(b, order + PV_SLOTS, slot).start()

        return acc

    acc = lax.fori_loop(0, n, body,
                        pn_ref[0].astype(BF16).astype(F32) * vn_ref[0].astype(BF16).astype(F32))

    @pl.when(b + 1 < pl.num_programs(0))
    def _():
        start_first_pages(b + 1)

    res = acc / l_ref[0]
    head = lax.broadcasted_iota(jnp.int32, res.shape, 0)
    lane_head = lax.broadcasted_iota(jnp.int32, res.shape, 1) // HEAD_DIM
    o_ref[0] = jnp.sum(jnp.where(head == lane_head, res, 0.0), axis=0, keepdims=True)


def _attn_sample(page_table, qbd, k_new, v_new, f_new, cache_kt, cache_vt, cache_ft):
    bsz, n_pages = page_table.shape
    a, h = ATTN_WIDTH, N_HEADS
    np_ = PAGES_PER_STEP
    n_steps = n_pages // np_
    idx = jnp.arange(PAGE_SIZE)
    lmat = (idx[:, None] > idx[None, :]).astype(BF16)
    newest_first = page_table[:, ::-1].reshape(-1)
    per_b = lambda shp: pl.BlockSpec(shp, lambda b, s, *_: (b,) + (0,) * (len(shp) - 1))

    def page_spec(shape, i):
        return pl.BlockSpec(shape, lambda b, s, pt, *_: (pt[(b * n_steps + s) * np_ + i], 0, 0))

    p, pn, l, need = pl.pallas_call(
        _scores_sample_kernel,
        out_shape=(jax.ShapeDtypeStruct((bsz, n_pages, h, PAGE_SIZE), F32),
                   jax.ShapeDtypeStruct((bsz, h, 1), F32), jax.ShapeDtypeStruct((bsz, h, 1), F32),
                   jax.ShapeDtypeStruct((bsz, 1, n_pages), F32)),
        grid_spec=pltpu.PrefetchScalarGridSpec(
            num_scalar_prefetch=1,
            grid=(bsz, n_steps),
            in_specs=([per_b((1, h, a)), per_b((1, 1, a)), per_b((1, h, 1)),
                       pl.BlockSpec((PAGE_SIZE, PAGE_SIZE), lambda b, s, pt: (0, 0))]
                      + [page_spec((1, a, PAGE_SIZE), i) for i in range(np_)]
                      + [page_spec((1, h, PAGE_SIZE), i) for i in range(np_)]),
            out_specs=(per_b((1, n_pages, h, PAGE_SIZE)), per_b((1, h, 1)), per_b((1, h, 1)),
                       per_b((1, 1, n_pages))),
            scratch_shapes=[pltpu.VMEM((h, 1), F32), pltpu.VMEM((h, 1), F32), pltpu.VMEM((h, 1), F32),
                            pltpu.VMEM((h, n_pages), F32)]),
        compiler_params=_params("arbitrary", "arbitrary"),
        name="fox_sample_scores",
    )(newest_first, qbd, k_new, f_new, lmat, *([cache_kt] * np_), *([cache_ft] * np_))

    order = jnp.arange(1, n_pages + 1, dtype=I32)
    count = jnp.max(jnp.where(need[:, 0, :] > 0, order, 0), axis=-1)

    per_seq = lambda shp: pl.BlockSpec(shp, lambda b, *_: (b,) + (0,) * (len(shp) - 1))
    return pl.pallas_call(
        _pv_sample_kernel,
        out_shape=jax.ShapeDtypeStruct((bsz, 1, a), F32),
        grid_spec=pltpu.PrefetchScalarGridSpec(
            num_scalar_prefetch=2,
            grid=(bsz,),
            in_specs=[per_seq((1, n_pages, h, PAGE_SIZE)), per_seq((1, h, 1)), per_seq((1, h, 1)),
                      per_seq((1, 1, a)), pl.BlockSpec(memory_space=pl.ANY)],
            out_specs=per_seq((1, 1, a)),
            scratch_shapes=[pltpu.VMEM((PV_SLOTS, a, PAGE_SIZE), F32),
                            pltpu.SemaphoreType.DMA((PV_SLOTS,))]),
        compiler_params=_params("arbitrary"),
        name="fox_sample_pv",
    )(newest_first, count, p, pn, l, v_new, cache_vt)


def _group_rms(x, bd, g):
    ms = jnp.dot((x * x).astype(BF16), bd, preferred_element_type=F32)
    return x * lax.rsqrt(ms * (1.0 / HEAD_DIM) + EPS) * g


def _route(logits):
    lane = lax.broadcasted_iota(jnp.int32, logits.shape, 1).astype(F32)
    big = jnp.float32(1e9)
    g0 = float(ROUTER_GROUP_LANE0)
    epg = float(EXPERTS_PER_GROUP)
    glog = jnp.where((lane >= g0) & (lane < g0 + N_GROUPS), logits, -jnp.inf)
    gmax = jnp.max(glog, axis=-1, keepdims=True)
    gidx = jnp.min(jnp.where(glog == gmax, lane, big), axis=-1, keepdims=True) - g0
    gp = 1.0 / jnp.sum(jnp.exp(glog - gmax), axis=-1, keepdims=True)
    lo = gidx * epg
    el = jnp.where((lane >= lo) & (lane < lo + epg), logits, -jnp.inf)
    m1 = jnp.max(el, axis=-1, keepdims=True)
    i1 = jnp.min(jnp.where(el == m1, lane, big), axis=-1, keepdims=True)
    el2 = jnp.where(lane == i1, -jnp.inf, el)
    m2 = jnp.max(el2, axis=-1, keepdims=True)
    i2 = jnp.min(jnp.where(el2 == m2, lane, big), axis=-1, keepdims=True)
    t = jnp.exp(m2 - m1)
    w1 = gp / (1.0 + t)
    w2 = w1 * t
    first_low = i1 < i2
    ea = jnp.minimum(i1, i2) - lo
    eb = jnp.maximum(i1, i2) - lo
    pair = ea * (2.0 * epg - 1.0 - ea) * 0.5 + (eb - ea - 1.0)
    bucket = gidx * float(PAIRS_PER_GROUP) + pair
    return bucket, jnp.where(first_low, w1, w2), jnp.where(first_low, w2, w1)


def _merge_kernel(sample, attn_ref, cin_ref, prev_ref, prev2_ref, gb_ref, x_ref, mod_ref, cw_ref,
                  ga_ref, gc_ref, bd_ref, wout_ref, g2_ref, wr_ref, br_ref, tri_ref, cnt_ref,
                  x1_ref, hrow_ref, info_ref, cnto_ref, cnt_sc):
    d, a = D_MODEL, ATTN_WIDTH
    cin = cin_ref[0]
    tm = cin.shape[0]
    first_step = (pl.program_id(0) == 0) & (pl.program_id(1) == 0)

    @pl.when(first_step)
    def _():
        cnt_sc[...] = cnt_ref[...]

    w0, w1, w2 = cw_ref[0:1, :], cw_ref[1:2, :], cw_ref[2:3, :]
    if sample:
        conv_y = w0 * prev2_ref[0] + w1 * prev_ref[0] + w2 * cin
    else:
        first = pl.program_id(1) == 0
        tail = jnp.where(first, 0.0, prev_ref[0])
        ext = jnp.concatenate([tail, cin], axis=0)
        conv_y = w0 * ext[6:6 + tm] + w1 * ext[7:7 + tm] + w2 * cin
    bd = bd_ref[...]
    an = _group_rms(attn_ref[0], bd, ga_ref[...]).astype(BF16)
    cn = _group_rms(gb_ref[0] * conv_y, bd, gc_ref[...]).astype(BF16)
    mix = (jnp.dot(an, wout_ref[0:a, :], preferred_element_type=F32)
           + jnp.dot(cn, wout_ref[a:, :], preferred_element_type=F32))
    ga1 = mod_ref[0, :, 2 * d:3 * d]
    sh2 = mod_ref[0, :, 3 * d:4 * d]
    sc2 = mod_ref[0, :, 4 * d:5 * d]
    x1 = x_ref[0] + ga1 * mix
    x1_ref[0] = x1
    h2 = _rms(x1, g2_ref[...]) * (1.0 + sc2) + sh2
    hb = h2.astype(BF16)
    hl = (h2 - hb.astype(F32)).astype(BF16)
    both = jnp.dot(hb, wr_ref[...], preferred_element_type=F32)
    logits = (both[:, 0:LANES] + both[:, LANES:]
              + jnp.dot(hl, wr_ref[:, 0:LANES], preferred_element_type=F32)) + br_ref[...]
    bucket, w_lo, w_hi = _route(logits)

    lane = lax.broadcasted_iota(jnp.int32, (tm, LANES), 1)
    lanef = lane.astype(F32)
    onehot = lanef == bucket
    incl = jnp.dot(tri_ref[...], jnp.where(onehot, 1.0, 0.0).astype(BF16), preferred_element_type=F32)
    cnt = cnt_sc[...]
    rank = jnp.sum(jnp.where(onehot, incl - 1.0 + cnt, 0.0), axis=-1, keepdims=True)
    cnt_new = cnt + incl[tm - 1:tm, :]
    cnt_sc[...] = cnt_new
    cnto_ref[...] = cnt_new
    info_ref[0] = jnp.where(lane == 0, bucket, jnp.where(lane == 1, rank, 0.0))

    hrow_ref[0, :, 0:d] = h2
    hrow_ref[0, :, d:] = jnp.where(lane == 0, w_lo, jnp.where(lane == 1, w_hi, 0.0))


def _merge(sample, attn, cin, prev, prev2, gb, x, mod, cw, ga, gc, bd, wout, g2, wr, br, tri, cnt, tm):
    g, t, d = x.shape
    r = mod.shape[1]
    c = CONV_WIDTH
    rspec = lambda w: pl.BlockSpec((1, tm, w), lambda b, i: (b, i, 0))
    const = lambda arr: pl.BlockSpec(arr.shape, lambda b, i: (0,) * arr.ndim)
    if sample:
        prev_spec = rspec(c)
    else:
        prev_spec = pl.BlockSpec((1, 8, c), lambda b, i: (b, jnp.maximum(i * (tm // 8) - 1, 0), 0))
    return pl.pallas_call(
        functools.partial(_merge_kernel, sample),
        out_shape=(jax.ShapeDtypeStruct((g, t, d), F32), jax.ShapeDtypeStruct((g, t, ROW_WIDTH), F32),
                   jax.ShapeDtypeStruct((g, t, LANES), F32), jax.ShapeDtypeStruct((1, LANES), F32)),
        grid=(g, t // tm),
        in_specs=[rspec(ATTN_WIDTH), rspec(c), prev_spec, prev_spec, rspec(c), rspec(d),
                  pl.BlockSpec((1, r, 6 * d), lambda b, i: (b, 0, 0)),
                  const(cw), const(ga), const(gc), const(bd), const(wout), const(g2), const(wr),
                  const(br), const(tri), const(cnt)],
        out_specs=(rspec(d), rspec(ROW_WIDTH), rspec(LANES), pl.BlockSpec((1, LANES), lambda b, i: (0, 0))),
        scratch_shapes=[pltpu.VMEM((1, LANES), F32)],
        compiler_params=_params("arbitrary", "arbitrary"),
        name="mix_out_router",
    )(attn, cin, prev, prev2, gb, x, mod, cw, ga, gc, bd, wout, g2, wr, br, tri, cnt)


def _dispatch_kernel(n_steps, pos_ref, h_ref, xs_in_ref, xs_ref, buf, sem):
    del xs_in_ref
    n_slots, tm, _ = buf.shape
    i = pl.program_id(0)

    def drain(slot):
        pltpu.make_async_copy(buf.at[slot], xs_ref.at[pl.ds(0, tm), :], sem.at[slot]).wait()

    for slot in range(n_slots):
        @pl.when(i > 0)
        def _(slot=slot):
            drain(slot)

        buf[slot] = h_ref[slot * tm:(slot + 1) * tm, :]
        base = (i * n_slots + slot) * tm
        for r in range(tm):
            pltpu.make_async_copy(buf.at[slot, pl.ds(r, 1), :],
                                  xs_ref.at[pl.ds(pos_ref[base + r], 1), :],
                                  sem.at[slot]).start(priority=r % 2)

    @pl.when(i == n_steps - 1)
    def _():
        for slot in range(n_slots):
            drain(slot)


def _dispatch(pos, hrow, xs, tm):
    n, w = hrow.shape
    n_slots = 2 if n % (2 * tm) == 0 else 1
    n_steps = n // (n_slots * tm)
    return pl.pallas_call(
        functools.partial(_dispatch_kernel, n_steps),
        out_shape=jax.ShapeDtypeStruct(xs.shape, xs.dtype),
        grid_spec=pltpu.PrefetchScalarGridSpec(
            num_scalar_prefetch=1,
            grid=(n_steps,),
            in_specs=[pl.BlockSpec((n_slots * tm, w), lambda i, pos: (i, 0)),
                      pl.BlockSpec(memory_space=pl.ANY)],
            out_specs=pl.BlockSpec(memory_space=pl.ANY),
            scratch_shapes=[pltpu.VMEM((n_slots, tm, w), F32), pltpu.SemaphoreType.DMA((n_slots,))]),
        input_output_aliases={2: 0},
        compiler_params=_params("arbitrary"),
        name="moe_dispatch",
    )(pos, hrow, xs)


def _experts_kernel(lo_ref, hi_ref, nt_ref, xs_hbm, wgl_ref, wul_ref, wdl_ref, wgh_ref, wuh_ref, wdh_ref,
                    zero_ref, o_ref, xbuf, xsem):
    del lo_ref, hi_ref, zero_ref
    n_slots, tg, _ = xbuf.shape
    i = pl.program_id(0)
    n = nt_ref[0]

    def fetch(tile, slot):
        rows = pl.ds(pl.multiple_of(tile * tg, tg), tg)
        return pltpu.make_async_copy(xs_hbm.at[rows, :], xbuf.at[slot], xsem.at[slot])

    @pl.when(i == 0)
    def _():
        for tile in range(n_slots - 1):
            @pl.when(tile < n)
            def _(tile=tile):
                fetch(tile, tile).start()

    @pl.when(i < n)
    def _():
        ahead = i + (n_slots - 1)

        @pl.when(ahead < n)
        def _():
            fetch(ahead, ahead % n_slots).start()

        slot = i % n_slots
        fetch(i, slot).wait()
        d = D_MODEL
        x = xbuf[slot, :, 0:d].astype(BF16)
        w_lo = xbuf[slot, :, d:d + 1]
        w_hi = xbuf[slot, :, d + 1:d + 2]

        def hidden(wg_ref, wu_ref, w):
            gt = jnp.dot(x, wg_ref[0], preferred_element_type=F32)
            up = jnp.dot(x, wu_ref[0], preferred_element_type=F32)
            return (gt * jax.nn.sigmoid(gt) * up * w).astype(BF16)

        o_ref[...] = (jnp.dot(hidden(wgl_ref, wul_ref, w_lo), wdl_ref[0], preferred_element_type=F32)
                      + jnp.dot(hidden(wgh_ref, wuh_ref, w_hi), wdh_ref[0], preferred_element_type=F32))


def _experts(tile_lo, tile_hi, n_tiles, xs, wg, wu, wdn, out_zero):
    rows, w = xs.shape
    d = D_MODEL
    tg = MOE_TILE
    tile = lambda i, lo, hi, nt: (jnp.minimum(i, nt[0] - 1), 0)
    wspec = lambda shp, which: pl.BlockSpec(
        (1,) + shp, (lambda i, lo, hi, nt: (lo[i], 0, 0)) if which == 0 else (lambda i, lo, hi, nt: (hi[i], 0, 0)))
    return pl.pallas_call(
        _experts_kernel,
        out_shape=jax.ShapeDtypeStruct((rows, d), F32),
        grid_spec=pltpu.PrefetchScalarGridSpec(
            num_scalar_prefetch=3,
            grid=(rows // tg,),
            in_specs=[pl.BlockSpec(memory_space=pl.ANY),
                      wspec((d, D_EXPERT), 0), wspec((d, D_EXPERT), 0), wspec((D_EXPERT, d), 0),
                      wspec((d, D_EXPERT), 1), wspec((d, D_EXPERT), 1), wspec((D_EXPERT, d), 1),
                      pl.BlockSpec(memory_space=pl.ANY)],
            out_specs=pl.BlockSpec((tg, d), tile),
            scratch_shapes=[pltpu.VMEM((EXPERT_ROW_SLOTS, tg, w), F32),
                            pltpu.SemaphoreType.DMA((EXPERT_ROW_SLOTS,))]),
        input_output_aliases={10: 0},
        compiler_params=_params("arbitrary"),
        name="moe_experts",
    )(tile_lo, tile_hi, n_tiles, xs, wg, wu, wdn, wg, wu, wdn, out_zero)


def _final_kernel(n_steps, pos_ref, x1_ref, mod_ref, modf_ref, gf_ref, os_ref, y_ref, buf, sem):
    d = D_MODEL
    n_slots, tm, _ = buf.shape
    i = pl.program_id(0)
    for slot in range(n_slots):
        rows = slice(slot * tm, (slot + 1) * tm)

        @pl.when(i > 0)
        def _(slot=slot, rows=rows):
            pltpu.make_async_copy(os_ref.at[pl.ds(0, tm), :], buf.at[slot], sem.at[slot]).wait()
            ga2 = mod_ref[0, :, 5 * d:6 * d]
            x2 = x1_ref[rows, :] + ga2 * buf[slot]
            shf = modf_ref[0, :, 0:d]
            scf = modf_ref[0, :, d:2 * d]
            y_ref[rows, :] = _rms(x2, gf_ref[...]) * (1.0 + scf) + shf

        @pl.when(i < n_steps)
        def _(slot=slot):
            base = (i * n_slots + slot) * tm
            for r in range(tm):
                pltpu.make_async_copy(os_ref.at[pl.ds(pos_ref[base + r], 1), :],
                                      buf.at[slot, pl.ds(r, 1), :], sem.at[slot]).start(priority=r % 2)


def _final(pos, x1, mod, modf, gf, out_sorted, tm, tiles_per_mod):
    n, d = x1.shape
    r = mod.shape[1]
    n_slots = 2 if n % (2 * tm) == 0 and tiles_per_mod % 2 == 0 else 1
    n_steps = n // (n_slots * tm)
    steps_per_mod = max(tiles_per_mod // n_slots, 1)
    done = lambda i: jnp.maximum(i - 1, 0)
    return pl.pallas_call(
        functools.partial(_final_kernel, n_steps),
        out_shape=jax.ShapeDtypeStruct((n, d), F32),
        grid_spec=pltpu.PrefetchScalarGridSpec(
            num_scalar_prefetch=1,
            grid=(n_steps + 1,),
            in_specs=[pl.BlockSpec((n_slots * tm, d), lambda i, pos: (done(i), 0)),
                      pl.BlockSpec((1, r, 6 * d), lambda i, pos: (done(i) // steps_per_mod, 0, 0)),
                      pl.BlockSpec((1, r, 2 * d), lambda i, pos: (done(i) // steps_per_mod, 0, 0)),
                      pl.BlockSpec((1, d), lambda i, pos: (0, 0)),
                      pl.BlockSpec(memory_space=pl.ANY)],
            out_specs=pl.BlockSpec((n_slots * tm, d), lambda i, pos: (done(i), 0)),
            scratch_shapes=[pltpu.VMEM((n_slots, tm, d), F32), pltpu.SemaphoreType.DMA((n_slots,))]),
        compiler_params=_params("arbitrary"),
        name="moe_combine_final",
    )(pos, x1, mod, modf, gf, out_sorted)


def kernel(x_prompt, x_sample, c_prompt, c_sample, cache_k, cache_v, cache_logf, state_conv, page_table, w_ada, b_ada, g_norm1, w_in, b_forget, conv_w, g_attn_out, g_conv_out, w_out, g_norm2, w_router_group, b_router_group, w_router_expert, b_router_expert, w_expert_gate, w_expert_up, w_expert_down, w_ada_final, b_ada_final, g_final):
    d, a, c, h = D_MODEL, ATTN_WIDTH, CONV_WIDTH, N_HEADS
    bsz, t, _ = x_prompt.shape
    dbs = x_sample.shape[0]
    n_prompt = bsz * t
    assert w_ada.shape[0] == 1 and x_sample.shape[1] == 1

    c_all = jnp.concatenate([c_prompt, c_sample], axis=0)
    mod = _ada(c_all, w_ada[0], b_ada[0])
    modf = _ada(c_all, w_ada_final, b_ada_final)
    mod_p, mod_s = mod[:bsz].reshape(bsz, 1, 6 * d), mod[bsz:].reshape(1, dbs, 6 * d)
    modf_p, modf_s = modf[:bsz].reshape(bsz, 1, 2 * d), modf[bsz:].reshape(1, dbs, 2 * d)

    wt = w_in[0].T.astype(BF16)
    wq = wt[0:a].T
    wkt = wt[a:2 * a]
    wv = wt[2 * a:3 * a].T
    wft = jnp.zeros((16, d), BF16).at[0:h].set(wt[3 * a:3 * a + h])
    wc = wt[3 * a + h:].T
    bf = b_forget[0].reshape(h, 1)
    g1 = g_norm1[0].reshape(1, d)
    g2 = g_norm2[0].reshape(1, d)
    gf = g_final.reshape(1, d)
    ga = g_attn_out[0].reshape(1, a)
    gc = g_conv_out[0].reshape(1, c)
    cw = conv_w[0]
    wout = w_out[0].astype(BF16)
    lane_group = jnp.arange(a) // HEAD_DIM
    bd = (lane_group[:, None] == lane_group[None, :]).astype(BF16)
    wr = jnp.zeros((d, LANES), F32)
    wr = wr.at[:, 0:N_EXPERTS].set(w_router_expert[0])
    wr = wr.at[:, ROUTER_GROUP_LANE0:ROUTER_GROUP_LANE0 + N_GROUPS].set(w_router_group[0])
    wrh = wr.astype(BF16)
    wr = jnp.concatenate([wrh, (wr - wrh.astype(F32)).astype(BF16)], axis=1)
    br = jnp.zeros((1, LANES), F32)
    br = br.at[0, 0:N_EXPERTS].set(b_router_expert[0])
    br = br.at[0, ROUTER_GROUP_LANE0:ROUTER_GROUP_LANE0 + N_GROUPS].set(b_router_group[0])
    n_rows_max = n_prompt + dbs + N_BUCKETS * (MOE_TILE - 1)
    max_tiles = -(-n_rows_max // MOE_TILE)
    tri = lambda m: (jnp.arange(m)[:, None] >= jnp.arange(m)[None, :]).astype(BF16)

    q_p, kt_p, vt_p, ktb_p, vb_p, lft_p, cin_p, gb_p = _inproj(
        True, x_prompt, mod_p, g1, wq, wkt, wv, wft, bf, wc, TOKEN_TILE)
    attn_p, xs, os_zero, wg, wu, wdn = _attn_prompt(
        q_p, ktb_p, _cumsum(lft_p), vb_p, w_expert_gate[0], w_expert_up[0], w_expert_down[0],
        max_tiles * MOE_TILE)
    x1_p, hrow_p, info_p, cnt_p = _merge(
        False, attn_p, cin_p, cin_p, cin_p, gb_p, x_prompt, mod_p, cw, ga, gc, bd, wout, g2, wr, br,
        tri(TOKEN_TILE), jnp.zeros((1, LANES), F32), TOKEN_TILE)

    xs_ = x_sample.reshape(1, dbs, d)
    q_s, k_s, v_s, lft_s, cin_s, gb_s = _inproj(False, xs_, mod_s, g1, wq, wkt, wv, wft, bf, wc, dbs)
    head_of_lane = jnp.arange(a) // HEAD_DIM
    qbd = jnp.where(head_of_lane[None, None, :] == jnp.arange(h)[None, :, None],
                    q_s.reshape(dbs, 1, a), jnp.zeros((), BF16))
    n_pool = cache_k.shape[1]
    page_major = lambda z: z[0].transpose(0, 2, 3, 1).reshape(n_pool, a, PAGE_SIZE)
    attn_s = _attn_sample(
        page_table, qbd, k_s.reshape(dbs, 1, a), v_s.reshape(dbs, 1, a),
        lft_s[0].T.reshape(dbs, h, 1),
        page_major(cache_k), page_major(cache_v), cache_logf[0].transpose(0, 2, 1))
    st = state_conv[0]
    x1_s, hrow_s, info_s, cnt_all = _merge(
        True, attn_s.reshape(1, dbs, a), cin_s, st[:, 1][None], st[:, 0][None], gb_s, xs_, mod_s, cw, ga,
        gc, bd, wout, g2, wr, br, tri(dbs), cnt_p, dbs)

    counts = cnt_all[0, 0:N_BUCKETS].astype(I32)
    padded = (counts + (MOE_TILE - 1)) // MOE_TILE * MOE_TILE
    ends = jnp.cumsum(padded)
    offsets = ends - padded
    n_tiles = (ends[-1] // MOE_TILE).astype(I32)
    tile_ids = jnp.minimum(jnp.arange(max_tiles, dtype=I32), n_tiles - 1)
    tile_bucket = jnp.sum(tile_ids[:, None] >= (ends // MOE_TILE)[None, :], axis=1)
    lo_tab, hi_tab = _pair_tables()
    tile_lo = jnp.asarray(lo_tab)[tile_bucket]
    tile_hi = jnp.asarray(hi_tab)[tile_bucket]
    bucket_ids = jnp.arange(N_BUCKETS, dtype=F32)

    def position(info):
        bucket, rank = info[..., 0:1], info[..., 1].astype(I32)
        base = jnp.sum(jnp.where(bucket == bucket_ids, offsets, 0), axis=-1)
        return (base + rank).reshape(-1)

    pos_p, pos_s = position(info_p), position(info_s)

    xs = _dispatch(pos_p, hrow_p.reshape(n_prompt, ROW_WIDTH), xs, TOKEN_TILE)
    xs = _dispatch(pos_s, hrow_s.reshape(dbs, ROW_WIDTH), xs, dbs)
    out_sorted = _experts(tile_lo, tile_hi, n_tiles.reshape(1), xs, wg, wu, wdn, os_zero)
    y_p = _final(pos_p, x1_p.reshape(n_prompt, d), mod_p, modf_p, gf, out_sorted, TOKEN_TILE,
                 t // TOKEN_TILE)
    y_s = _final(pos_s, x1_s.reshape(dbs, d), mod_s, modf_s, gf, out_sorted, dbs, 1)

    kv_p = lambda zt: zt.reshape(bsz, h, HEAD_DIM, t).transpose(0, 3, 1, 2)[None]
    kv_s = lambda z: z.reshape(1, dbs, 1, h, HEAD_DIM)
    return (y_p.reshape(bsz, t, d), y_s.reshape(dbs, 1, d),
            kv_p(kt_p), kv_p(vt_p),
            lft_p.transpose(0, 2, 1)[None],
            cin_p[:, t - (CONV_K - 1):, :][None],
            kv_s(k_s), kv_s(v_s),
            lft_s[0].T.reshape(1, dbs, 1, h),
            jnp.stack([st[:, 1], cin_s[0]], axis=1)[None])
```
